```python
import math
import jax
import jax.numpy as jnp
from jax import lax
import numpy as np


D_MODEL = 2048
BATCH = 16
SEQ = 2048
DEPTH = 2

N_META = 16
GRID_W = 64
NA_HEADS = 16
NA_HEAD_DIM = 64
NA_WIDTH = NA_HEADS * NA_HEAD_DIM
WIN_ROWS_MAX = 8
WIN_COLS = 16
COL_BLOCK = 16
COL_SPAN = COL_BLOCK + WIN_COLS
SC_WIDTH = 512
HY_WIDTH = 512
HY_ORDER = 2
HY_EMB = 33
HY_HIDDEN = 64
HY_DECAY_TARGET = 1e-2
HY_FAST_DECAY = 0.3
HY_SLOW_DECAY = 1.5
HY_MOD_SHIFT = 0.05
N_BRANCH = 3
PROJ_WIDTHS = (NA_WIDTH, NA_WIDTH, NA_WIDTH, SC_WIDTH, SC_WIDTH, SC_WIDTH, 3 * HY_WIDTH) + (D_MODEL,) * N_BRANCH
PROJ_WIDTH = sum(PROJ_WIDTHS)
D_FF = 5632
N_EXPERTS = 8
TOP_K = 2
D_FF_EXPERT = 7168
MOE_BLOCK = 512
NORM_EPS = 1e-6
NEG_INF = -1e30

kernel_name = 'hybrid_natten_shortconv_hyena_moe_encoder'


def rms_norm(x, g):
    xf = x.astype(jnp.float32)
    y = xf * lax.rsqrt(jnp.mean(xf * xf, axis=-1, keepdims=True) + NORM_EPS)
    return (y * g.astype(jnp.float32)).astype(x.dtype)


def dwconv3(u, w):
    return lax.conv_general_dilated(u, w[:, None, :].astype(u.dtype), window_strides=(1,), padding=[(1, 1)],
                                    dimension_numbers=('NWC', 'WIO', 'NWC'), feature_group_count=u.shape[-1])


def _col_tables():
    n_cb = GRID_W // COL_BLOCK
    q_cols = np.arange(n_cb)[:, None] * COL_BLOCK + np.arange(COL_BLOCK)[None, :]
    span_start = np.clip(np.arange(n_cb) * COL_BLOCK - WIN_COLS // 2, 0, GRID_W - COL_SPAN)
    span_cols = span_start[:, None] + np.arange(COL_SPAN)[None, :]
    win_start = np.clip(q_cols - WIN_COLS // 2, 0, GRID_W - WIN_COLS)
    kc = span_cols[:, None, :]
    inside = (kc >= win_start[..., None]) & (kc < win_start[..., None] + WIN_COLS)
    dc_idx = np.clip(kc - q_cols[..., None], -(WIN_COLS - 1), WIN_COLS - 1) + WIN_COLS - 1
    return span_cols, inside, dc_idx


def neighbourhood_attention(q, k, v, rpb, meta_bias):
    bsz, length, n_h, d_h = q.shape
    n_tok = length - N_META
    rows = n_tok // GRID_W
    kh = min(WIN_ROWS_MAX, rows)
    n_cb = GRID_W // COL_BLOCK
    span_cols, inside, dc_idx = _col_tables()
    col_mask = jnp.asarray(inside)[:, :, None, :]
    q = q * (d_h ** -0.5)
    qm, km, vm = q[:, :N_META], k[:, :N_META], v[:, :N_META]
    qg = q[:, N_META:].reshape(bsz, rows, GRID_W, n_h, d_h)
    kg = k[:, N_META:].reshape(bsz, rows, GRID_W, n_h, d_h)
    vg = v[:, N_META:].reshape(bsz, rows, GRID_W, n_h, d_h)
    mb = meta_bias.astype(jnp.float32)
    s_mm = jnp.einsum('bqhd,bkhd->bhqk', qm, km).astype(jnp.float32) + mb[:, None, :]
    o_meta = jnp.einsum('bhqk,bkhd->bqhd', jax.nn.softmax(s_mm, axis=-1).astype(v.dtype), vm)
    rpb_cols = rpb.astype(jnp.float32)[:, :, dc_idx]

    def row_block(r):
        rs = jnp.clip(r - kh // 2, 0, rows - kh)
        kb = lax.dynamic_slice_in_dim(kg, rs, kh, axis=1)[:, :, span_cols]
        vb = lax.dynamic_slice_in_dim(vg, rs, kh, axis=1)[:, :, span_cols]
        qb = lax.dynamic_index_in_dim(qg, r, axis=1, keepdims=False).reshape(bsz, n_cb, COL_BLOCK, n_h, d_h)
        s_grid = jnp.einsum('bcqhd,bicshd->bhcqis', qb, kb).astype(jnp.float32)
        dr_idx = rs + jnp.arange(kh) - r + WIN_ROWS_MAX - 1
        bias = jnp.take(rpb_cols, dr_idx, axis=1).transpose(0, 2, 3, 1, 4)
        s_grid = jnp.where(col_mask, s_grid + bias, NEG_INF)
        s_meta = jnp.einsum('bcqhd,bmhd->bhcqm', qb, km).astype(jnp.float32) + mb[:, None, None, :]
        s = jnp.concatenate([s_grid.reshape(bsz, n_h, n_cb, COL_BLOCK, kh * COL_SPAN), s_meta], axis=-1)
        p = jax.nn.softmax(s, axis=-1).astype(v.dtype)
        p_grid = p[..., :kh * COL_SPAN].reshape(bsz, n_h, n_cb, COL_BLOCK, kh, COL_SPAN)
        o = (jnp.einsum('bhcqis,bicshd->bcqhd', p_grid, vb)
             + jnp.einsum('bhcqm,bmhd->bcqhd', p[..., kh * COL_SPAN:], vm))
        return o.reshape(bsz, GRID_W, n_h, d_h)

    o_grid = lax.map(row_block, jnp.arange(rows))
    o_grid = jnp.moveaxis(o_grid, 0, 1).reshape(bsz, n_tok, n_h, d_h)
    return jnp.concatenate([o_meta, o_grid], axis=1)


def short_conv_mixer(gate_b, gate_c, u, w_conv):
    return gate_b * dwconv3(gate_c * u, w_conv)


def hyena_filters(length, w1, b1, w2, b2, w3, freq, decay):
    f32 = jnp.float32
    bands = (HY_EMB - 1) // 2
    t = jnp.linspace(0.0, 1.0, length, dtype=f32)[:, None]
    ang = (2.0 * math.pi / length) * jnp.arange(length, dtype=f32)[:, None]
    fb = jnp.linspace(1e-4, bands - 1, bands, dtype=f32)[None, :]
    z = jnp.concatenate([t, jnp.cos(fb * ang), -jnp.sin(fb * ang)], axis=-1)
    fr = freq.astype(f32)
    h = jnp.sin(fr * (z @ w1.astype(f32) + b1.astype(f32)))
    h = jnp.sin(fr * (h @ w2.astype(f32) + b2.astype(f32)))
    h = (h @ w3.astype(f32)).reshape(length, HY_ORDER, 2, HY_WIDTH)
    window = jnp.exp(-t[:, :, None, None] * jnp.abs(decay.astype(f32))) + HY_MOD_SHIFT
    return h * window


def bidir_long_conv(u, h_fwd, h_bwd, skip):
    length, ch = h_fwd.shape
    g = jnp.concatenate([h_fwd, jnp.zeros((1, ch), jnp.float32), h_bwd[:0:-1]], axis=0)
    g = g / jnp.sum(jnp.abs(g), axis=0, keepdims=True)
    uf = u.astype(jnp.float32)
    spec = jnp.fft.rfft(uf, n=2 * length, axis=1) * jnp.fft.rfft(g, axis=0)[None]
    y = jnp.fft.irfft(spec, n=2 * length, axis=1)[:, :length]
    return (y + uf * skip.astype(jnp.float32)).astype(u.dtype)


def hyena_mixer(u, w_conv, w1, b1, w2, b2, w3, freq, decay, skip):
    length = u.shape[1]
    v, x1, x2 = jnp.split(dwconv3(u, w_conv), 3, axis=-1)
    filt = hyena_filters(length, w1, b1, w2, b2, w3, freq, decay)
    z = v
    for o, gate in enumerate((x1, x2)):
        z = gate * bidir_long_conv(z, filt[:, o, 0], filt[:, o, 1], skip[o])
    return z


def hybrid_mixer(xn, w_in, na_rpb, na_meta_bias, sc_conv, hy_conv, hy_w1, hy_b1, hy_w2, hy_b2, hy_w3,
                 hy_freq, hy_decay, hy_skip, w_na_out, w_sc_out, w_hy_out, w_o):
    bsz, length, _ = xn.shape
    split_at = [int(s) for s in np.cumsum(PROJ_WIDTHS)[:-1]]
    q, k, v, sc_b, sc_c, sc_u, hy_u, g_na, g_sc, g_hy = jnp.split(xn @ w_in, split_at, axis=-1)

    def heads(a):
        return a.reshape(bsz, length, NA_HEADS, NA_HEAD_DIM)

    y_na = neighbourhood_attention(heads(q), heads(k), heads(v), na_rpb, na_meta_bias).reshape(bsz, length, NA_WIDTH)
    y_sc = short_conv_mixer(sc_b, sc_c, sc_u, sc_conv)
    y_hy = hyena_mixer(hy_u, hy_conv, hy_w1, hy_b1, hy_w2, hy_b2, hy_w3, hy_freq, hy_decay, hy_skip)
    merged = (jax.nn.sigmoid(g_na) * (y_na @ w_na_out)
              + jax.nn.sigmoid(g_sc) * (y_sc @ w_sc_out)
              + jax.nn.sigmoid(g_hy) * (y_hy @ w_hy_out))
    return merged @ w_o


def swiglu(x, w1, w3, w2):
    return (jax.nn.silu(x @ w1) * (x @ w3)) @ w2


def moe_swiglu(x, w_router, w1, w3, w2):
    bsz, length, d = x.shape
    n_tok = bsz * length
    n_assign = n_tok * TOP_K
    xf = x.reshape(n_tok, d)
    logits = (xf @ w_router).astype(jnp.float32)
    top_val, top_idx = lax.top_k(logits, TOP_K)
    gate = jax.nn.softmax(top_val, axis=-1).astype(x.dtype).reshape(n_assign)
    expert = top_idx.reshape(n_assign).astype(jnp.int32)
    token = jnp.repeat(jnp.arange(n_tok, dtype=jnp.int32), TOP_K)
    order = jnp.argsort(expert)
    e_sorted = expert[order]
    counts = jnp.bincount(expert, length=N_EXPERTS).astype(jnp.int32)
    padded = (counts + MOE_BLOCK - 1) // MOE_BLOCK * MOE_BLOCK
    pad_end = jnp.cumsum(padded)
    pad_start = pad_end - padded
    grp_start = jnp.cumsum(counts) - counts
    slot = pad_start[e_sorted] + jnp.arange(n_assign, dtype=jnp.int32) - grp_start[e_sorted]
    n_blocks = -(-n_assign // MOE_BLOCK) + N_EXPERTS
    n_slots = n_blocks * MOE_BLOCK
    slot_token = jnp.zeros((n_slots,), jnp.int32).at[slot].set(token[order])
    slot_gate = jnp.zeros((n_slots,), x.dtype).at[slot].set(gate[order])
    block_expert = jnp.minimum(
        jnp.searchsorted(pad_end, jnp.arange(n_blocks, dtype=jnp.int32) * MOE_BLOCK, side='right'), N_EXPERTS - 1)
    xb = xf[slot_token].reshape(n_blocks, MOE_BLOCK, d)

    def expert_block(args):
        xe, e = args
        return (jax.nn.silu(xe @ w1[e]) * (xe @ w3[e])) @ w2[e]

    yb = lax.map(expert_block, (xb, block_expert)).reshape(n_slots, d)
    y = jnp.zeros_like(xf).at[slot_token].add(yb * slot_gate[:, None])
    return y.reshape(bsz, length, d)


def setup_inputs(seed: int = 0) -> dict:
    key = jax.random.key(seed)
    keys = iter(jax.random.split(key, 32))
    f32 = jnp.float32

    def nrm(shape, scale):
        return jax.random.normal(next(keys), shape, f32) * scale

    n_dense = (DEPTH + 1) // 2
    n_moe = DEPTH // 2
    min_decay = math.log(HY_DECAY_TARGET) / HY_SLOW_DECAY
    max_decay = math.log(HY_DECAY_TARGET) / HY_FAST_DECAY
    decay_base = jnp.linspace(min_decay, max_decay, HY_WIDTH, dtype=f32)
    return {
        'x': nrm((BATCH, SEQ, D_MODEL), 1.0),
        'meta_tokens': nrm((N_META, D_MODEL), 1.0),
        'norm_mix': 1.0 + nrm((DEPTH, D_MODEL), 0.01),
        'w_in': nrm((DEPTH, D_MODEL, PROJ_WIDTH), D_MODEL ** -0.5),
        'na_rpb': nrm((DEPTH, NA_HEADS, 2 * WIN_ROWS_MAX - 1, 2 * WIN_COLS - 1), 0.02),
        'na_meta_bias': nrm((DEPTH, NA_HEADS, N_META), 0.02),
        'sc_conv': nrm((DEPTH, 3, SC_WIDTH), 3 ** -0.5),
        'hy_conv': nrm((DEPTH, 3, 3 * HY_WIDTH), 3 ** -0.5),
        'hy_w1': nrm((DEPTH, HY_EMB, HY_HIDDEN), HY_EMB ** -0.5),
        'hy_b1': nrm((DEPTH, HY_HIDDEN), 0.02),
        'hy_w2': nrm((DEPTH, HY_HIDDEN, HY_HIDDEN), HY_HIDDEN ** -0.5),
        'hy_b2': nrm((DEPTH, HY_HIDDEN), 0.02),
        'hy_w3': nrm((DEPTH, HY_HIDDEN, HY_ORDER * 2 * HY_WIDTH), HY_HIDDEN ** -0.5),
        'hy_freq': 1.0 + nrm((DEPTH, HY_HIDDEN), 0.1),
        'hy_decay': decay_base + nrm((DEPTH, HY_ORDER, 2, HY_WIDTH), 0.1),
        'hy_skip': nrm((DEPTH, HY_ORDER, HY_WIDTH), 0.1),
        'w_na_out': nrm((DEPTH, NA_WIDTH, D_MODEL), NA_WIDTH ** -0.5),
        'w_sc_out': nrm((DEPTH, SC_WIDTH, D_MODEL), SC_WIDTH ** -0.5),
        'w_hy_out': nrm((DEPTH, HY_WIDTH, D_MODEL), HY_WIDTH ** -0.5),
        'w_o': nrm((DEPTH, D_MODEL, D_MODEL), D_MODEL ** -0.5),
        'norm_ffn': 1.0 + nrm((DEPTH, D_MODEL), 0.01),
        'ffn_w1': nrm((n_dense, D_MODEL, D_FF), D_MODEL ** -0.5),
        'ffn_w3': nrm((n_dense, D_MODEL, D_FF), D_MODEL ** -0.5),
        'ffn_w2': nrm((n_dense, D_FF, D_MODEL), D_FF ** -0.5),
        'router_w': nrm((n_moe, D_MODEL, N_EXPERTS), D_MODEL ** -0.5),
        'moe_w1': nrm((n_moe, N_EXPERTS, D_MODEL, D_FF_EXPERT), D_MODEL ** -0.5),
        'moe_w3': nrm((n_moe, N_EXPERTS, D_MODEL, D_FF_EXPERT), D_MODEL ** -0.5),
        'moe_w2': nrm((n_moe, N_EXPERTS, D_FF_EXPERT, D_MODEL), D_FF_EXPERT ** -0.5),
        'norm_final': 1.0 + nrm((D_MODEL,), 0.01),
    }


def reference(x, meta_tokens, norm_mix, w_in, na_rpb, na_meta_bias, sc_conv, hy_conv, hy_w1, hy_b1, hy_w2,
              hy_b2, hy_w3, hy_freq, hy_decay, hy_skip, w_na_out, w_sc_out, w_hy_out, w_o, norm_ffn,
              ffn_w1, ffn_w3, ffn_w2, router_w, moe_w1, moe_w3, moe_w2, norm_final):
    bsz = x.shape[0]
    meta = jnp.broadcast_to(meta_tokens.astype(x.dtype)[None], (bsz, N_META, D_MODEL))
    h = jnp.concatenate([meta, x], axis=1)
    for layer in range(DEPTH):
        hn = rms_norm(h, norm_mix[layer])
        h = h + hybrid_mixer(hn, w_in[layer], na_rpb[layer], na_meta_bias[layer], sc_conv[layer], hy_conv[layer],
                             hy_w1[layer], hy_b1[layer], hy_w2[layer], hy_b2[layer], hy_w3[layer], hy_freq[layer],
                             hy_decay[layer], hy_skip[layer], w_na_out[layer], w_sc_out[layer], w_hy_out[layer],
                             w_o[layer])
        hn = rms_norm(h, norm_ffn[layer])
        j = layer // 2
        if layer % 2 == 0:
            h = h + swiglu(hn, ffn_w1[j], ffn_w3[j], ffn_w2[j])
        else:
            h = h + moe_swiglu(hn, router_w[j], moe_w1[j], moe_w3[j], moe_w2[j])
    return rms_norm(h, norm_final)[:, N_META:]
```

```python
import functools
import math

import numpy as np
import jax
import jax.numpy as jnp
from jax import lax
from jax.experimental import pallas as pl
from jax.experimental.pallas import tpu as pltpu

D_MODEL = 2048
N_META = 16
GRID_W = 64
GRID_ROWS = 32
NA_HEADS = 16
NA_HEAD_DIM = 64
NA_WIDTH = NA_HEADS * NA_HEAD_DIM
WIN_ROWS = 8
WIN_COLS = 16
SC_WIDTH = 512
HY_WIDTH = 512
HY_ORDER = 2
HY_EMB = 33
HY_HIDDEN = 64
HY_MOD_SHIFT = 0.05
PROJ_WIDTH = 3 * NA_WIDTH + 3 * SC_WIDTH + 3 * HY_WIDTH + 3 * D_MODEL
D_FF = 5632
N_EXPERTS = 8
TOP_K = 2
D_FF_EXPERT = 7168
MOE_BLOCK = 512
NORM_EPS = 1e-6
NEG_INF = -1e30

COL_Q, COL_K, COL_V = 0, NA_WIDTH, 2 * NA_WIDTH
COL_SCB = 3 * NA_WIDTH
COL_SCC = COL_SCB + SC_WIDTH
COL_SCU = COL_SCC + SC_WIDTH
COL_HYU = COL_SCU + SC_WIDTH
COL_GNA = COL_HYU + 3 * HY_WIDTH
COL_GSC = COL_GNA + D_MODEL
COL_GHY = COL_GSC + D_MODEL

V7X_VMEM_BYTES = 64 * 1024 * 1024
VMEM_LIMIT = V7X_VMEM_BYTES - 8 * 1024 * 1024

ROW_TILE = 768
NORM_CHUNK = 16
FFT_N = 2 * (N_META + GRID_ROWS * GRID_W)
FFT_KP = 2080
FFT_KB = 1040
INV_NB = 688

F32 = jnp.float32
BF16 = jnp.bfloat16


def _params(sem):
    return pltpu.CompilerParams(dimension_semantics=sem, vmem_limit_bytes=VMEM_LIMIT)


def _sigmoid(a):
    return 1.0 / (1.0 + jnp.exp(-a))


def _rms_norm_rows(x_ref, g_ref, xn_ref):
    rows = x_ref.shape[0]
    g = g_ref[...]

    def chunk(c, carry):
        r0 = pl.multiple_of(c * NORM_CHUNK, NORM_CHUNK)
        x = x_ref[pl.ds(r0, NORM_CHUNK), :]
        ms = jnp.mean(x * x, axis=-1, keepdims=True)
        xn_ref[pl.ds(r0, NORM_CHUNK), :] = (x * lax.rsqrt(ms + NORM_EPS) * g).astype(xn_ref.dtype)
        return carry

    lax.fori_loop(0, rows // NORM_CHUNK, chunk, 0, unroll=4)


def _norm_mm_kernel(x_ref, g_ref, w_ref, o_ref, xn_ref):
    @pl.when(pl.program_id(1) == 0)
    def _():
        _rms_norm_rows(x_ref, g_ref, xn_ref)

    o_ref[...] = jnp.dot(xn_ref[...], w_ref[...], preferred_element_type=F32).astype(o_ref.dtype)


def norm_matmul(x, g, w, *, tn, out_dtype=BF16, tm=ROW_TILE):
    t, k = x.shape
    n = w.shape[1]
    return pl.pallas_call(
        _norm_mm_kernel,
        out_shape=jax.ShapeDtypeStruct((t, n), out_dtype),
        grid=(t // tm, n // tn),
        in_specs=[
            pl.BlockSpec((tm, k), lambda i, j: (i, 0)),
            pl.BlockSpec((1, k), lambda i, j: (0, 0)),
            pl.BlockSpec((k, tn), lambda i, j: (0, j)),
        ],
        out_specs=pl.BlockSpec((tm, tn), lambda i, j: (i, j)),
        scratch_shapes=[pltpu.VMEM((tm, k), BF16)],
        compiler_params=_params(("parallel", "arbitrary")),
        name="norm_matmul",
    )(x, g.reshape(1, k), w)


def _norm_swiglu_kernel(x_ref, g_ref, w1_ref, w3_ref, o_ref, xn_ref):
    @pl.when(pl.program_id(1) == 0)
    def _():
        _rms_norm_rows(x_ref, g_ref, xn_ref)

    xn = xn_ref[...]
    a = jnp.dot(xn, w1_ref[...], preferred_element_type=F32)
    b = jnp.dot(xn, w3_ref[...], preferred_element_type=F32)
    o_ref[...] = (a * _sigmoid(a) * b).astype(o_ref.dtype)


def norm_swiglu_up(x, g, w1, w3, *, tn, tm=ROW_TILE):
    t, k = x.shape
    n = w1.shape[1]
    return pl.pallas_call(
        _norm_swiglu_kernel,
        out_shape=jax.ShapeDtypeStruct((t, n), BF16),
        grid=(t // tm, n // tn),
        in_specs=[
            pl.BlockSpec((tm, k), lambda i, j: (i, 0)),
            pl.BlockSpec((1, k), lambda i, j: (0, 0)),
            pl.BlockSpec((k, tn), lambda i, j: (0, j)),
            pl.BlockSpec((k, tn), lambda i, j: (0, j)),
        ],
        out_specs=pl.BlockSpec((tm, tn), lambda i, j: (i, j)),
        scratch_shapes=[pltpu.VMEM((tm, k), BF16)],
        compiler_params=_params(("parallel", "arbitrary")),
        name="norm_swiglu_up",
    )(x, g.reshape(1, k), w1, w3)


def _norm_router_kernel(x_ref, g_ref, w_ref, o_ref, xn_ref):
    _rms_norm_rows(x_ref, g_ref, xn_ref)
    o_ref[...] = jnp.dot(xn_ref[...], w_ref[...], preferred_element_type=F32,
                         precision=lax.Precision.HIGHEST)


def norm_router(x, g, w, *, tm=ROW_TILE):
    t, k = x.shape
    n = w.shape[1]
    return pl.pallas_call(
        _norm_router_kernel,
        out_shape=jax.ShapeDtypeStruct((t, n), F32),
        grid=(t // tm,),
        in_specs=[
            pl.BlockSpec((tm, k), lambda i: (i, 0)),
            pl.BlockSpec((1, k), lambda i: (0, 0)),
            pl.BlockSpec((k, n), lambda i: (0, 0)),
        ],
        out_specs=pl.BlockSpec((tm, n), lambda i: (i, 0)),
        scratch_shapes=[pltpu.VMEM((tm, k), F32)],
        compiler_params=_params(("parallel",)),
        name="norm_router",
    )(x, g.reshape(1, k), w)


def _norm_kernel(x_ref, g_ref, o_ref):
    _rms_norm_rows(x_ref, g_ref, o_ref)


def rms_norm_rows(x, g, *, out_dtype, tm=ROW_TILE):
    t, k = x.shape
    return pl.pallas_call(
        _norm_kernel,
        out_shape=jax.ShapeDtypeStruct((t, k), out_dtype),
        grid=(t // tm,),
        in_specs=[pl.BlockSpec((tm, k), lambda i: (i, 0)), pl.BlockSpec((1, k), lambda i: (0, 0))],
        out_specs=pl.BlockSpec((tm, k), lambda i: (i, 0)),
        compiler_params=_params(("parallel",)),
        name="rms_norm",
    )(x, g.reshape(1, k))


def _res_mm_kernel(r_ref, x_ref, w_ref, o_ref):
    o_ref[...] = r_ref[...] + jnp.dot(x_ref[...], w_ref[...], preferred_element_type=F32)


def residual_matmul(res, x, w, *, tn, tm=ROW_TILE):
    t, k = x.shape
    n = w.shape[1]
    return pl.pallas_call(
        _res_mm_kernel,
        out_shape=jax.ShapeDtypeStruct((t, n), F32),
        grid=(t // tm, n // tn),
        in_specs=[
            pl.BlockSpec((tm, tn), lambda i, j: (i, j)),
            pl.BlockSpec((tm, k), lambda i, j: (i, 0)),
            pl.BlockSpec((k, tn), lambda i, j: (0, j)),
        ],
        out_specs=pl.BlockSpec((tm, tn), lambda i, j: (i, j)),
        compiler_params=_params(("parallel", "parallel")),
        name="residual_matmul",
    )(res, x, w)


def _merge_kernel(yna_ref, ysc_ref, yhy_ref, gna_ref, gsc_ref, ghy_ref, wna_ref, wsc_ref, why_ref, o_ref):
    acc = _sigmoid(gna_ref[...].astype(F32)) * jnp.dot(yna_ref[...], wna_ref[...], preferred_element_type=F32)
    acc += _sigmoid(gsc_ref[...].astype(F32)) * jnp.dot(ysc_ref[...], wsc_ref[...], preferred_element_type=F32)
    acc += _sigmoid(ghy_ref[...].astype(F32)) * jnp.dot(yhy_ref[...], why_ref[...], preferred_element_type=F32)
    o_ref[...] = acc.astype(o_ref.dtype)


def merge_branches(y_na, y_sc, y_hy, proj, w_na, w_sc, w_hy, *, tn=512, tm=ROW_TILE):
    t = y_na.shape[0]
    n = D_MODEL
    return pl.pallas_call(
        _merge_kernel,
        out_shape=jax.ShapeDtypeStruct((t, n), BF16),
        grid=(t // tm, n // tn),
        in_specs=[
            pl.BlockSpec((tm, NA_WIDTH), lambda i, j: (i, 0)),
            pl.BlockSpec((tm, SC_WIDTH), lambda i, j: (i, 0)),
            pl.BlockSpec((tm, HY_WIDTH), lambda i, j: (i, 0)),
            pl.BlockSpec((tm, tn), lambda i, j: (i, COL_GNA // tn + j)),
            pl.BlockSpec((tm, tn), lambda i, j: (i, COL_GSC // tn + j)),
            pl.BlockSpec((tm, tn), lambda i, j: (i, COL_GHY // tn + j)),
            pl.BlockSpec((NA_WIDTH, tn), lambda i, j: (0, j)),
            pl.BlockSpec((SC_WIDTH, tn), lambda i, j: (0, j)),
            pl.BlockSpec((HY_WIDTH, tn), lambda i, j: (0, j)),
        ],
        out_specs=pl.BlockSpec((tm, tn), lambda i, j: (i, j)),
        compiler_params=_params(("parallel", "parallel")),
        name="merge_branches",
    )(y_na, y_sc, y_hy, proj, proj, proj, w_na, w_sc, w_hy)


def _na_bias_table(rpb):
    qc = np.arange(GRID_W)[:, None]
    kc = np.arange(GRID_W)[None, :]
    win_start = np.clip(qc - WIN_COLS // 2, 0, GRID_W - WIN_COLS)
    inside = (kc >= win_start) & (kc < win_start + WIN_COLS)
    dc = np.clip(kc - qc, -(WIN_COLS - 1), WIN_COLS - 1) + WIN_COLS - 1
    dr = np.arange(WIN_ROWS)[None, :] - np.arange(WIN_ROWS)[:, None] + WIN_ROWS - 1
    tbl = rpb.astype(F32)[:, dr][:, :, :, dc]
    tbl = jnp.where(jnp.asarray(inside)[None, None, None], tbl, NEG_INF)
    tbl = tbl.transpose(0, 1, 3, 2, 4)
    return tbl.reshape(NA_HEADS, WIN_ROWS, GRID_W, WIN_ROWS * GRID_W)


def _na_kernel(q_ref, k_ref, v_ref, tbl_ref, mb_ref, o_ref):
    scale = NA_HEAD_DIM ** -0.5
    lane = lax.broadcasted_iota(jnp.int32, (1, 2 * NA_HEAD_DIM), 1)
    head_lanes = (lane < NA_HEAD_DIM, lane >= NA_HEAD_DIM)
    nt = (((1,), (1,)), ((), ()))
    k_meta = k_ref[0:N_META, :]
    v_meta = v_ref[0:N_META, :]
    band = WIN_ROWS * GRID_W

    def softmax_pv(q, hh, s_grid, v_grid):
        qh = jnp.where(head_lanes[hh], q, jnp.zeros_like(q))
        s_meta = lax.dot_general(qh, k_meta, nt, preferred_element_type=F32) * scale + mb_ref[hh]
        m = jnp.max(s_meta, axis=-1, keepdims=True)
        if s_grid is not None:
            s_grid = s_grid(qh)
            m = jnp.maximum(m, jnp.max(s_grid, axis=-1, keepdims=True))
        p_meta = jnp.exp(s_meta - m)
        den = jnp.sum(p_meta, axis=-1, keepdims=True)
        out = jnp.dot(p_meta.astype(BF16), v_meta, preferred_element_type=F32)
        if s_grid is not None:
            p_grid = jnp.exp(s_grid - m)
            den = den + jnp.sum(p_grid, axis=-1, keepdims=True)
            out = out + jnp.dot(p_grid.astype(BF16), v_grid, preferred_element_type=F32)
        return out / den

    q_meta = q_ref[0:N_META, :]
    outs = [softmax_pv(q_meta, hh, None, None) for hh in range(2)]
    o_ref[0:N_META, :] = jnp.where(head_lanes[0], outs[0], outs[1]).astype(o_ref.dtype)

    def row(r, carry):
        rs = jnp.clip(r - WIN_ROWS // 2, 0, GRID_ROWS - WIN_ROWS)
        q0 = pl.multiple_of(N_META + r * GRID_W, 16)
        k0 = pl.multiple_of(N_META + rs * GRID_W, 16)
        q = q_ref[pl.ds(q0, GRID_W), :]
        kb = k_ref[pl.ds(k0, band), :]
        vb = v_ref[pl.ds(k0, band), :]
        outs = []
        for hh in range(2):
            def s_grid(qh, hh=hh):
                return lax.dot_general(qh, kb, nt, preferred_element_type=F32) * scale + tbl_ref[hh, r - rs]
            outs.append(softmax_pv(q, hh, s_grid, vb))
        o_ref[pl.ds(q0, GRID_W), :] = jnp.where(head_lanes[0], outs[0], outs[1]).astype(o_ref.dtype)
        return carry

    lax.fori_loop(0, GRID_ROWS, row, 0)


def neighbourhood_attention(proj, rpb, meta_bias, *, bsz, length):
    tbl = _na_bias_table(rpb)
    mb = meta_bias.astype(F32).reshape(NA_HEADS, 1, N_META)
    hp = 2 * NA_HEAD_DIM
    n_pairs = NA_HEADS // 2
    return pl.pallas_call(
        _na_kernel,
        out_shape=jax.ShapeDtypeStruct((bsz * length, NA_WIDTH), BF16),
        grid=(n_pairs, bsz),
        in_specs=[
            pl.BlockSpec((length, hp), lambda h, b: (b, COL_Q // hp + h)),
            pl.BlockSpec((length, hp), lambda h, b: (b, COL_K // hp + h)),
            pl.BlockSpec((length, hp), lambda h, b: (b, COL_V // hp + h)),
            pl.BlockSpec((2, WIN_ROWS, GRID_W, WIN_ROWS * GRID_W), lambda h, b: (h, 0, 0, 0)),
            pl.BlockSpec((2, 1, N_META), lambda h, b: (h, 0, 0)),
        ],
        out_specs=pl.BlockSpec((length, hp), lambda h, b: (b, h)),
        compiler_params=_params(("parallel", "parallel")),
        name="neighbourhood_attention",
    )(proj, proj, proj, tbl, mb)


def _dwconv3(u, w):
    length = u.shape[0]
    row = lax.broadcasted_iota(jnp.int32, u.shape, 0)
    prev = jnp.where(row == 0, 0.0, pltpu.roll(u, 1, 0))
    nxt = jnp.where(row == length - 1, 0.0, pltpu.roll(u, length - 1, 0))
    return prev * w[0:1] + u * w[1:2] + nxt * w[2:3]


def _gated_conv_kernel(b_ref, c_ref, u_ref, w_ref, o_ref):
    u = c_ref[...].astype(F32) * u_ref[...].astype(F32)
    o_ref[...] = (b_ref[...].astype(F32) * _dwconv3(u, w_ref[...])).astype(o_ref.dtype)


def _plain_conv_kernel(u_ref, w_ref, o_ref):
    o_ref[...] = _dwconv3(u_ref[...].astype(F32), w_ref[...]).astype(o_ref.dtype)


CONV_CB = 256


def short_conv_mixer(proj, w_conv, *, bsz, length):
    cb = CONV_CB
    return pl.pallas_call(
        _gated_conv_kernel,
        out_shape=jax.ShapeDtypeStruct((bsz * length, SC_WIDTH), BF16),
        grid=(bsz, SC_WIDTH // cb),
        in_specs=[
            pl.BlockSpec((length, cb), lambda b, c: (b, COL_SCB // cb + c)),
            pl.BlockSpec((length, cb), lambda b, c: (b, COL_SCC // cb + c)),
            pl.BlockSpec((length, cb), lambda b, c: (b, COL_SCU // cb + c)),
            pl.BlockSpec((3, cb), lambda b, c: (0, c)),
        ],
        out_specs=pl.BlockSpec((length, cb), lambda b, c: (b, c)),
        compiler_params=_params(("parallel", "parallel")),
        name="short_conv_mixer",
    )(proj, proj, proj, w_conv.astype(F32))


def hyena_in_conv(proj, w_conv, *, bsz, length):
    cb = CONV_CB
    return pl.pallas_call(
        _plain_conv_kernel,
        out_shape=jax.ShapeDtypeStruct((bsz * length, 3 * HY_WIDTH), BF16),
        grid=(bsz, 3 * HY_WIDTH // cb),
        in_specs=[
            pl.BlockSpec((length, cb), lambda b, c: (b, COL_HYU // cb + c)),
            pl.BlockSpec((3, cb), lambda b, c: (0, c)),
        ],
        out_specs=pl.BlockSpec((length, cb), lambda b, c: (b, c)),
        compiler_params=_params(("parallel", "parallel")),
        name="hyena_in_conv",
    )(proj, w_conv.astype(F32))


def _filter_kernel(z_ref, w1_ref, b1_ref, w2_ref, b2_ref, w3_ref, fr_ref, dec_ref, f_ref, s_ref):
    hi = lax.Precision.HIGHEST
    z = z_ref[...]
    fr = fr_ref[...]
    h = jnp.sin(fr * (jnp.dot(z, w1_ref[...], preferred_element_type=F32, precision=hi) + b1_ref[...]))
    h = jnp.sin(fr * (jnp.dot(h, w2_ref[...], preferred_element_type=F32, precision=hi) + b2_ref[...]))
    h = jnp.dot(h, w3_ref[...], preferred_element_type=F32, precision=hi)
    t = z[:, 0:1]
    f = h * (jnp.exp(-t * jnp.abs(dec_ref[0])) + HY_MOD_SHIFT)
    row = lax.broadcasted_iota(jnp.int32, f.shape, 0)
    backward = pl.program_id(0) % 2 == 1
    f = jnp.where(jnp.logical_and(row == 0, backward), 0.0, f)
    f_ref[0] = f.astype(f_ref.dtype)
    s_ref[0] = jnp.sum(jnp.abs(f), axis=0, keepdims=True)


def hyena_filters(length, w1, b1, w2, b2, w3, freq, decay):
    bands = (HY_EMB - 1) // 2
    t = jnp.linspace(0.0, 1.0, length, dtype=F32)[:, None]
    ang = (2.0 * math.pi / length) * jnp.arange(length, dtype=F32)[:, None]
    fb = jnp.linspace(1e-4, bands - 1, bands, dtype=F32)[None, :]
    z = jnp.concatenate([t, jnp.cos(fb * ang), -jnp.sin(fb * ang)], axis=-1)
    emb = 128
    z = jnp.pad(z, ((0, 0), (0, emb - HY_EMB)))
    w1p = jnp.pad(w1.astype(F32), ((0, emb - HY_EMB), (0, 0)))
    n_slab = HY_ORDER * 2
    row = lambda a: a.astype(F32).reshape(1, -1)
    return pl.pallas_call(
        _filter_kernel,
        out_shape=(jax.ShapeDtypeStruct((n_slab, length, HY_WIDTH), BF16),
                   jax.ShapeDtypeStruct((n_slab, 1, HY_WIDTH), F32)),
        grid=(n_slab,),
        in_specs=[
            pl.BlockSpec((length, emb), lambda j: (0, 0)),
            pl.BlockSpec((emb, HY_HIDDEN), lambda j: (0, 0)),
            pl.BlockSpec((1, HY_HIDDEN), lambda j: (0, 0)),
            pl.BlockSpec((HY_HIDDEN, HY_HIDDEN), lambda j: (0, 0)),
            pl.BlockSpec((1, HY_HIDDEN), lambda j: (0, 0)),
            pl.BlockSpec((HY_HIDDEN, HY_WIDTH), lambda j: (0, j)),
            pl.BlockSpec((1, HY_HIDDEN), lambda j: (0, 0)),
            pl.BlockSpec((1, 1, HY_WIDTH), lambda j: (j, 0, 0)),
        ],
        out_specs=(pl.BlockSpec((1, length, HY_WIDTH), lambda j: (j, 0, 0)),
                   pl.BlockSpec((1, 1, HY_WIDTH), lambda j: (j, 0, 0))),
        compiler_params=_params(("parallel",)),
        name="hyena_filters",
    )(z, w1p, row(b1), w2.astype(F32), row(b2), w3.astype(F32), row(freq),
      decay.astype(F32).reshape(n_slab, 1, HY_WIDTH))


def _dft_matrices(length):
    k = jnp.arange(FFT_KP, dtype=jnp.int32)[:, None]
    n = jnp.arange(length, dtype=jnp.int32)[None, :]
    theta = (2.0 * math.pi / FFT_N) * ((k * n) % FFT_N).astype(F32)
    valid = k <= FFT_N // 2
    fc = jnp.where(valid, jnp.cos(theta), 0.0)
    fs = jnp.where(valid, -jnp.sin(theta), 0.0)
    return fc.astype(BF16), fs.astype(BF16), fc.T.astype(BF16), fs.T.astype(BF16)


def _dft_raw_kernel(fc_ref, fs_ref, u_ref, re_ref, im_ref):
    u = u_ref[0]
    re_ref[0] = jnp.dot(fc_ref[...], u, preferred_element_type=F32)
    im_ref[0] = jnp.dot(fs_ref[...], u, preferred_element_type=F32)


def dft_filters(fc, fs, filt):
    n_slab, length, c = filt.shape
    out = jax.ShapeDtypeStruct((n_slab, FFT_KP, c), F32)
    return pl.pallas_call(
        _dft_raw_kernel,
        out_shape=(out, out),
        grid=(FFT_KP // FFT_KB, n_slab),
        in_specs=[
            pl.BlockSpec((FFT_KB, length), lambda kb, s: (kb, 0)),
            pl.BlockSpec((FFT_KB, length), lambda kb, s: (kb, 0)),
            pl.BlockSpec((1, length, c), lambda kb, s: (s, 0, 0)),
        ],
        out_specs=(pl.BlockSpec((1, FFT_KB, c), lambda kb, s: (s, kb, 0)),
                   pl.BlockSpec((1, FFT_KB, c), lambda kb, s: (s, kb, 0))),
        compiler_params=_params(("parallel", "parallel")),
        name="dft_filters",
    )(fc, fs, filt)


def _dft_mul_kernel(fc_ref, fs_ref, u_ref, are_ref, aim_ref, s_ref, yre_ref, yim_ref, g_ref):
    @pl.when(pl.program_id(1) == 0)
    def _():
        k = pl.program_id(0) * FFT_KB + lax.broadcasted_iota(jnp.int32, (FFT_KB, 1), 0)
        wk = jnp.where(jnp.logical_or(k == 0, k == FFT_N // 2), 1.0, 2.0)
        wk = jnp.where(k > FFT_N // 2, 0.0, wk) / FFT_N
        inv = wk / (s_ref[0] + s_ref[1])
        g_ref[0] = (are_ref[0] + are_ref[1]) * inv
        g_ref[1] = (aim_ref[0] - aim_ref[1]) * inv

    u = u_ref[...]
    re = jnp.dot(fc_ref[...], u, preferred_element_type=F32)
    im = jnp.dot(fs_ref[...], u, preferred_element_type=F32)
    gre = g_ref[0]
    gim = g_ref[1]
    yre_ref[0] = (re * gre - im * gim).astype(yre_ref.dtype)
    yim_ref[0] = (re * gim + im * gre).astype(yim_ref.dtype)


def dft_forward_mul(fc, fs, u, col, a_re, a_im, abs_sum, order, *, bsz, length):
    c = HY_WIDTH
    out = jax.ShapeDtypeStruct((bsz, FFT_KP, c), BF16)
    return pl.pallas_call(
        _dft_mul_kernel,
        out_shape=(out, out),
        grid=(FFT_KP // FFT_KB, bsz),
        in_specs=[
            pl.BlockSpec((FFT_KB, length), lambda kb, b: (kb, 0)),
            pl.BlockSpec((FFT_KB, length), lambda kb, b: (kb, 0)),
            pl.BlockSpec((length, c), lambda kb, b: (b, col)),
            pl.BlockSpec((2, FFT_KB, c), lambda kb, b: (order, kb, 0)),
            pl.BlockSpec((2, FFT_KB, c), lambda kb, b: (order, kb, 0)),
            pl.BlockSpec((2, 1, c), lambda kb, b: (order, 0, 0)),
        ],
        out_specs=(pl.BlockSpec((1, FFT_KB, c), lambda kb, b: (b, kb, 0)),
                   pl.BlockSpec((1, FFT_KB, c), lambda kb, b: (b, kb, 0))),
        scratch_shapes=[pltpu.VMEM((2, FFT_KB, c), F32)],
        compiler_params=_params(("parallel", "arbitrary")),
        name="dft_forward_mul",
    )(fc, fs, u, a_re, a_im, abs_sum)


def _dft_inv_kernel(ct_ref, st_ref, yre_ref, yim_ref, u_ref, gate_ref, skip_ref, o_ref):
    y = jnp.dot(ct_ref[...], yre_ref[0], preferred_element_type=F32)
    y += jnp.dot(st_ref[...], yim_ref[0], preferred_element_type=F32)
    y += u_ref[...].astype(F32) * skip_ref[0]
    o_ref[...] = (gate_ref[...].astype(F32) * y).astype(o_ref.dtype)


def dft_inverse_gate(fct, fst, yre, yim, u, u_col, gate, gate_col, skip, order, *, bsz, length):
    c = HY_WIDTH
    nblk = length // INV_NB
    return pl.pallas_call(
        _dft_inv_kernel,
        out_shape=jax.ShapeDtypeStruct((bsz * length, c), BF16),
        grid=(nblk, bsz),
        in_specs=[
            pl.BlockSpec((INV_NB, FFT_KP), lambda nb, b: (nb, 0)),
            pl.BlockSpec((INV_NB, FFT_KP), lambda nb, b: (nb, 0)),
            pl.BlockSpec((1, FFT_KP, c), lambda nb, b: (b, 0, 0)),
            pl.BlockSpec((1, FFT_KP, c), lambda nb, b: (b, 0, 0)),
            pl.BlockSpec((INV_NB, c), lambda nb, b: (b * nblk + nb, u_col)),
            pl.BlockSpec((INV_NB, c), lambda nb, b: (b * nblk + nb, gate_col)),
            pl.BlockSpec((1, 1, c), lambda nb, b: (order, 0, 0)),
        ],
        out_specs=pl.BlockSpec((INV_NB, c), lambda nb, b: (b * nblk + nb, 0)),
        compiler_params=_params(("parallel", "parallel")),
        name="dft_inverse_gate",
    )(fct, fst, yre, yim, u, gate, skip)


def hyena_mixer(proj, w_conv, w1, b1, w2, b2, w3, freq, decay, skip, dft, *, bsz, length):
    fc, fs, fct, fst = dft
    vxx = hyena_in_conv(proj, w_conv, bsz=bsz, length=length)
    filt, abs_sum = hyena_filters(length, w1, b1, w2, b2, w3, freq, decay)
    a_re, a_im = dft_filters(fc, fs, filt)
    skip = skip.astype(F32).reshape(HY_ORDER, 1, HY_WIDTH)
    kw = dict(bsz=bsz, length=length)
    yre, yim = dft_forward_mul(fc, fs, vxx, 0, a_re, a_im, abs_sum, 0, **kw)
    z = dft_inverse_gate(fct, fst, yre, yim, vxx, 0, vxx, 1, skip, 0, **kw)
    yre, yim = dft_forward_mul(fc, fs, z, 0, a_re, a_im, abs_sum, 1, **kw)
    return dft_inverse_gate(fct, fst, yre, yim, z, 0, vxx, 2, skip, 1, **kw)


def _moe_kernel(be_ref, x_ref, w1_ref, w3_ref, w2_ref, o_ref, acc_ref):
    j = pl.program_id(1)
    x = x_ref[...]
    a = jnp.dot(x, w1_ref[0], preferred_element_type=F32)
    b = jnp.dot(x, w3_ref[0], preferred_element_type=F32)
    hid = (a * _sigmoid(a) * b).astype(BF16)
    part = jnp.dot(hid, w2_ref[0], preferred_element_type=F32)

    @pl.when(j == 0)
    def _():
        acc_ref[...] = part

    @pl.when(j > 0)
    def _():
        acc_ref[...] += part

    @pl.when(j == pl.num_programs(1) - 1)
    def _():
        o_ref[...] = acc_ref[...].astype(o_ref.dtype)


MOE_FC = 1024


def moe_experts(xb, block_expert, w1, w3, w2):
    n_slots, d = xb.shape
    n_blocks = n_slots // MOE_BLOCK
    fc = MOE_FC
    grid_spec = pltpu.PrefetchScalarGridSpec(
        num_scalar_prefetch=1,
        grid=(n_blocks, D_FF_EXPERT // fc),
        in_specs=[
            pl.BlockSpec((MOE_BLOCK, d), lambda i, j, be: (i, 0)),
            pl.BlockSpec((1, d, fc), lambda i, j, be: (be[i], 0, j)),
            pl.BlockSpec((1, d, fc), lambda i, j, be: (be[i], 0, j)),
            pl.BlockSpec((1, fc, d), lambda i, j, be: (be[i], j, 0)),
        ],
        out_specs=pl.BlockSpec((MOE_BLOCK, d), lambda i, j, be: (i, 0)),
        scratch_shapes=[pltpu.VMEM((MOE_BLOCK, d), F32)],
    )
    return pl.pallas_call(
        _moe_kernel,
        out_shape=jax.ShapeDtypeStruct((n_slots, d), F32),
        grid_spec=grid_spec,
        compiler_params=_params(("parallel", "arbitrary")),
        name="moe_experts",
    )(block_expert, xb, w1, w3, w2)


def moe_swiglu(h, g_norm, w_router, w1, w3, w2):
    n_tok, d = h.shape
    n_assign = n_tok * TOP_K
    router_pad = 128
    wr = jnp.pad(w_router.astype(F32), ((0, 0), (0, router_pad - N_EXPERTS)))
    logits = norm_router(h, g_norm, wr)[:, :N_EXPERTS]
    hn = rms_norm_rows(h, g_norm, out_dtype=BF16)
    top_val, top_idx = lax.top_k(logits, TOP_K)
    gate = jax.nn.softmax(top_val, axis=-1).reshape(n_assign)
    expert = top_idx.reshape(n_assign).astype(jnp.int32)
    token = jnp.repeat(jnp.arange(n_tok, dtype=jnp.int32), TOP_K)
    order = jnp.argsort(expert)
    e_sorted = expert[order]
    counts = jnp.bincount(expert, length=N_EXPERTS).astype(jnp.int32)
    padded = (counts + MOE_BLOCK - 1) // MOE_BLOCK * MOE_BLOCK
    pad_end = jnp.cumsum(padded)
    pad_start = pad_end - padded
    grp_start = jnp.cumsum(counts) - counts
    slot = pad_start[e_sorted] + jnp.arange(n_assign, dtype=jnp.int32) - grp_start[e_sorted]
    n_blocks = -(-n_assign // MOE_BLOCK) + N_EXPERTS
    n_slots = n_blocks * MOE_BLOCK
    slot_token = jnp.zeros((n_slots,), jnp.int32).at[slot].set(token[order])
    block_expert = jnp.minimum(
        jnp.searchsorted(pad_end, jnp.arange(n_blocks, dtype=jnp.int32) * MOE_BLOCK, side='right'),
        N_EXPERTS - 1).astype(jnp.int32)
    xb = hn[slot_token]
    yb = moe_experts(xb, block_expert, w1, w3, w2)
    slot_of_assign = jnp.zeros((n_assign,), jnp.int32).at[order].set(slot)
    y = (yb[slot_of_assign] * gate[:, None]).reshape(n_tok, TOP_K, d)
    return h + y[:, 0] + y[:, 1]


def kernel(x, meta_tokens, norm_mix, w_in, na_rpb, na_meta_bias, sc_conv, hy_conv, hy_w1, hy_b1, hy_w2, hy_b2,
           hy_w3, hy_freq, hy_decay, hy_skip, w_na_out, w_sc_out, w_hy_out, w_o, norm_ffn, ffn_w1, ffn_w3,
           ffn_w2, router_w, moe_w1, moe_w3, moe_w2, norm_final):
    bsz, seq, d = x.shape
    length = N_META + seq
    depth = w_in.shape[0]
    meta = jnp.broadcast_to(meta_tokens.astype(x.dtype)[None], (bsz, N_META, d))
    h = jnp.concatenate([meta, x], axis=1).reshape(bsz * length, d)
    dft = _dft_matrices(length)
    kw = dict(bsz=bsz, length=length)
    for layer in range(depth):
        proj = norm_matmul(h, norm_mix[layer], w_in[layer].astype(BF16), tn=1024)
        y_na = neighbourhood_attention(proj, na_rpb[layer], na_meta_bias[layer], **kw)
        y_sc = short_conv_mixer(proj, sc_conv[layer], **kw)
        y_hy = hyena_mixer(proj, hy_conv[layer], hy_w1[layer], hy_b1[layer], hy_w2[layer], hy_b2[layer],
                           hy_w3[layer], hy_freq[layer], hy_decay[layer], hy_skip[layer], dft, **kw)
        merged = merge_branches(y_na, y_sc, y_hy, proj, w_na_out[layer].astype(BF16),
                                w_sc_out[layer].astype(BF16), w_hy_out[layer].astype(BF16))
        h = residual_matmul(h, merged, w_o[layer].astype(BF16), tn=1024)
        j = layer // 2
        if layer % 2 == 0:
            up = norm_swiglu_up(h, norm_ffn[layer], ffn_w1[j].astype(BF16), ffn_w3[j].astype(BF16), tn=512)
            h = residual_matmul(h, up, ffn_w2[j].astype(BF16), tn=512)
        else:
            h = moe_swiglu(h, norm_ffn[layer], router_w[j], moe_w1[j].astype(BF16), moe_w3[j].astype(BF16),
                           moe_w2[j].astype(BF16))
    out = rms_norm_rows(h, norm_final, out_dtype=x.dtype)
    return out.reshape(bsz, length, d)[:, N_META:]
```

```python
import functools
import math

import numpy as np
import jax
import jax.numpy as jnp
from jax import lax
from jax.experimental import pallas as pl
from jax.experimental.pallas import tpu as pltpu

D_MODEL = 2048
N_META = 16
GRID_W = 64
GRID_ROWS = 32
NA_HEADS = 16
NA_HEAD_DIM = 64
NA_WIDTH = NA_HEADS * NA_HEAD_DIM
WIN_ROWS = 8
WIN_COLS = 16
SC_WIDTH = 512
HY_WIDTH = 512
HY_ORDER = 2
HY_EMB = 33
HY_HIDDEN = 64
HY_MOD_SHIFT = 0.05
PROJ_WIDTH = 3 * NA_WIDTH + 3 * SC_WIDTH + 3 * HY_WIDTH + 3 * D_MODEL
D_FF = 5632
N_EXPERTS = 8
TOP_K = 2
D_FF_EXPERT = 7168
MOE_BLOCK = 512
NORM_EPS = 1e-6
NEG_INF = -1e30

COL_Q, COL_K, COL_V = 0, NA_WIDTH, 2 * NA_WIDTH
COL_SCB = 3 * NA_WIDTH
COL_SCC = COL_SCB + SC_WIDTH
COL_SCU = COL_SCC + SC_WIDTH
COL_HYU = COL_SCU + SC_WIDTH
COL_GNA = COL_HYU + 3 * HY_WIDTH
COL_GSC = COL_GNA + D_MODEL
COL_GHY = COL_GSC + D_MODEL

V7X_VMEM_BYTES = 64 * 1024 * 1024
VMEM_LIMIT = V7X_VMEM_BYTES - 8 * 1024 * 1024

ROW_TILE = 768
NORM_CHUNK = 16
FFT_N = 2 * (N_META + GRID_ROWS * GRID_W)
FFT_KP = 2080
FFT_KB = 1040
INV_NB = 688

F32 = jnp.float32
BF16 = jnp.bfloat16


def _params(sem):
    return pltpu.CompilerParams(dimension_semantics=sem, vmem_limit_bytes=VMEM_LIMIT)


def _sigmoid(a):
    return 1.0 / (1.0 + jnp.exp(-a))


def _rms_norm_rows(x_ref, g_ref, xn_ref):
    rows = x_ref.shape[0]
    g = g_ref[...]

    def chunk(c, carry):
        r0 = pl.multiple_of(c * NORM_CHUNK, NORM_CHUNK)
        x = x_ref[pl.ds(r0, NORM_CHUNK), :]
        ms = jnp.mean(x * x, axis=-1, keepdims=True)
        xn_ref[pl.ds(r0, NORM_CHUNK), :] = (x * lax.rsqrt(ms + NORM_EPS) * g).astype(xn_ref.dtype)
        return carry

    lax.fori_loop(0, rows // NORM_CHUNK, chunk, 0, unroll=4)


def _norm_mm_kernel(x_ref, g_ref, w_ref, o_ref, xn_ref):
    @pl.when(pl.program_id(1) == 0)
    def _():
        _rms_norm_rows(x_ref, g_ref, xn_ref)

    o_ref[...] = jnp.dot(xn_ref[...], w_ref[...], preferred_element_type=F32).astype(o_ref.dtype)


def norm_matmul(x, g, w, *, tn, out_dtype=BF16, tm=ROW_TILE):
    t, k = x.shape
    n = w.shape[1]
    return pl.pallas_call(
        _norm_mm_kernel,
        out_shape=jax.ShapeDtypeStruct((t, n), out_dtype),
        grid=(t // tm, n // tn),
        in_specs=[
            pl.BlockSpec((tm, k), lambda i, j: (i, 0)),
            pl.BlockSpec((1, k), lambda i, j: (0, 0)),
            pl.BlockSpec((k, tn), lambda i, j: (0, j)),
        ],
        out_specs=pl.BlockSpec((tm, tn), lambda i, j: (i, j)),
        scratch_shapes=[pltpu.VMEM((tm, k), BF16)],
        compiler_params=_params(("parallel", "arbitrary")),
        name="norm_matmul",
    )(x, g.reshape(1, k), w)


def _norm_swiglu_kernel(x_ref, g_ref, w1_ref, w3_ref, o_ref, xn_ref):
    @pl.when(pl.program_id(1) == 0)
    def _():
        _rms_norm_rows(x_ref, g_ref, xn_ref)

    xn = xn_ref[...]
    a = jnp.dot(xn, w1_ref[...], preferred_element_type=F32)
    b = jnp.dot(xn, w3_ref[...], preferred_element_type=F32)
    o_ref[...] = (a * _sigmoid(a) * b).astype(o_ref.dtype)


def norm_swiglu_up(x, g, w1, w3, *, tn, tm=ROW_TILE):
    t, k = x.shape
    n = w1.shape[1]
    return pl.pallas_call(
        _norm_swiglu_kernel,
        out_shape=jax.ShapeDtypeStruct((t, n), BF16),
        grid=(t // tm, n // tn),
        in_specs=[
            pl.BlockSpec((tm, k), lambda i, j: (i, 0)),
            pl.BlockSpec((1, k), lambda i, j: (0, 0)),
            pl.BlockSpec((k, tn), lambda i, j: (0, j)),
            pl.BlockSpec((k, tn), lambda i, j: (0, j)),
        ],
        out_specs=pl.BlockSpec((tm, tn), lambda i, j: (i, j)),
        scratch_shapes=[pltpu.VMEM((tm, k), BF16)],
        compiler_params=_params(("parallel", "arbitrary")),
        name="norm_swiglu_up",
    )(x, g.reshape(1, k), w1, w3)


def _norm_router_kernel(x_ref, g_ref, w_ref, o_ref, xb_ref, xn_ref):
    _rms_norm_rows(x_ref, g_ref, xn_ref)
    xn = xn_ref[...]
    xb_ref[...] = xn.astype(xb_ref.dtype)
    o_ref[...] = jnp.dot(xn, w_ref[...], preferred_element_type=F32, precision=lax.Precision.HIGHEST)


def norm_router(x, g, w, *, tm=ROW_TILE):
    t, k = x.shape
    n = w.shape[1]
    return pl.pallas_call(
        _norm_router_kernel,
        out_shape=(jax.ShapeDtypeStruct((t, n), F32), jax.ShapeDtypeStruct((t, k), BF16)),
        grid=(t // tm,),
        in_specs=[
            pl.BlockSpec((tm, k), lambda i: (i, 0)),
            pl.BlockSpec((1, k), lambda i: (0, 0)),
            pl.BlockSpec((k, n), lambda i: (0, 0)),
        ],
        out_specs=(pl.BlockSpec((tm, n), lambda i: (i, 0)), pl.BlockSpec((tm, k), lambda i: (i, 0))),
        scratch_shapes=[pltpu.VMEM((tm, k), F32)],
        compiler_params=_params(("parallel",)),
        name="norm_router",
    )(x, g.reshape(1, k), w)


def _norm_kernel(x_ref, g_ref, o_ref):
    _rms_norm_rows(x_ref, g_ref, o_ref)


def rms_norm_rows(x, g, *, out_dtype, tm=ROW_TILE):
    t, k = x.shape
    return pl.pallas_call(
        _norm_kernel,
        out_shape=jax.ShapeDtypeStruct((t, k), out_dtype),
        grid=(t // tm,),
        in_specs=[pl.BlockSpec((tm, k), lambda i: (i, 0)), pl.BlockSpec((1, k), lambda i: (0, 0))],
        out_specs=pl.BlockSpec((tm, k), lambda i: (i, 0)),
        compiler_params=_params(("parallel",)),
        name="rms_norm",
    )(x, g.reshape(1, k))


def _res_mm_kernel(r_ref, x_ref, w_ref, o_ref):
    o_ref[...] = r_ref[...] + jnp.dot(x_ref[...], w_ref[...], preferred_element_type=F32)


def residual_matmul(res, x, w, *, tn, tm=ROW_TILE):
    t, k = x.shape
    n = w.shape[1]
    return pl.pallas_call(
        _res_mm_kernel,
        out_shape=jax.ShapeDtypeStruct((t, n), F32),
        grid=(t // tm, n // tn),
        in_specs=[
            pl.BlockSpec((tm, tn), lambda i, j: (i, j)),
            pl.BlockSpec((tm, k), lambda i, j: (i, 0)),
            pl.BlockSpec((k, tn), lambda i, j: (0, j)),
        ],
        out_specs=pl.BlockSpec((tm, tn), lambda i, j: (i, j)),
        compiler_params=_params(("parallel", "parallel")),
        name="residual_matmul",
    )(res, x, w)


def _merge_kernel(yna_ref, ysc_ref, yhy_ref, gna_ref, gsc_ref, ghy_ref, wna_ref, wsc_ref, why_ref, o_ref):
    acc = _sigmoid(gna_ref[...].astype(F32)) * jnp.dot(yna_ref[...], wna_ref[...], preferred_element_type=F32)
    acc += _sigmoid(gsc_ref[...].astype(F32)) * jnp.dot(ysc_ref[...], wsc_ref[...], preferred_element_type=F32)
    acc += _sigmoid(ghy_ref[...].astype(F32)) * jnp.dot(yhy_ref[...], why_ref[...], preferred_element_type=F32)
    o_ref[...] = acc.astype(o_ref.dtype)


def merge_branches(y_na, y_sc, y_hy, proj, w_na, w_sc, w_hy, *, tn=512, tm=ROW_TILE):
    t = y_na.shape[0]
    n = D_MODEL
    return pl.pallas_call(
        _merge_kernel,
        out_shape=jax.ShapeDtypeStruct((t, n), BF16),
        grid=(t // tm, n // tn),
        in_specs=[
            pl.BlockSpec((tm, NA_WIDTH), lambda i, j: (i, 0)),
            pl.BlockSpec((tm, SC_WIDTH), lambda i, j: (i, 0)),
            pl.BlockSpec((tm, HY_WIDTH), lambda i, j: (i, 0)),
            pl.BlockSpec((tm, tn), lambda i, j: (i, COL_GNA // tn + j)),
            pl.BlockSpec((tm, tn), lambda i, j: (i, COL_GSC // tn + j)),
            pl.BlockSpec((tm, tn), lambda i, j: (i, COL_GHY // tn + j)),
            pl.BlockSpec((NA_WIDTH, tn), lambda i, j: (0, j)),
            pl.BlockSpec((SC_WIDTH, tn), lambda i, j: (0, j)),
            pl.BlockSpec((HY_WIDTH, tn), lambda i, j: (0, j)),
        ],
        out_specs=pl.BlockSpec((tm, tn), lambda i, j: (i, j)),
        compiler_params=_params(("parallel", "parallel")),
        name="merge_branches",
    )(y_na, y_sc, y_hy, proj, proj, proj, w_na, w_sc, w_hy)


def _na_bias_table(rpb):
    qc = np.arange(GRID_W)[:, None]
    kc = np.arange(GRID_W)[None, :]
    win_start = np.clip(qc - WIN_COLS // 2, 0, GRID_W - WIN_COLS)
    inside = (kc >= win_start) & (kc < win_start + WIN_COLS)
    dc = np.clip(kc - qc, -(WIN_COLS - 1), WIN_COLS - 1) + WIN_COLS - 1
    dr = np.arange(WIN_ROWS)[None, :] - np.arange(WIN_ROWS)[:, None] + WIN_ROWS - 1
    tbl = rpb.astype(F32)[:, dr][:, :, :, dc]
    tbl = jnp.where(jnp.asarray(inside)[None, None, None], tbl, NEG_INF)
    tbl = tbl.transpose(0, 1, 3, 2, 4)
    return tbl.reshape(NA_HEADS, WIN_ROWS, GRID_W, WIN_ROWS * GRID_W)


NA_ROW_GROUP = 4


def _na_kernel(q_ref, k_ref, v_ref, tbl_ref, mb_ref, o_ref):
    scale = NA_HEAD_DIM ** -0.5
    lane = lax.broadcasted_iota(jnp.int32, (1, 2 * NA_HEAD_DIM), 1)
    first = lane < NA_HEAD_DIM
    nt = (((1,), (1,)), ((), ()))
    k_meta = k_ref[0:N_META, :]
    v_meta = v_ref[0:N_META, :]
    mb_grid = mb_ref[0, 0]
    band = WIN_ROWS * GRID_W

    def stack_heads(q):
        q = q * scale
        zero = jnp.zeros_like(q)
        return jnp.concatenate([jnp.where(first, q, zero), jnp.where(first, zero, q)], axis=0)

    def unstack_heads(out, den, n):
        out = out / den
        return jnp.where(first, out[:n], out[n:])

    def softmax_parts(s_grid, s_meta):
        m = jnp.max(s_meta, axis=-1, keepdims=True)
        if s_grid is not None:
            m = jnp.maximum(m, jnp.max(s_grid, axis=-1, keepdims=True))
        p_meta = jnp.exp(s_meta - m)
        den = jnp.sum(p_meta, axis=-1, keepdims=True)
        p_grid = None
        if s_grid is not None:
            p_grid = jnp.exp(s_grid - m)
            den = den + jnp.sum(p_grid, axis=-1, keepdims=True)
            p_grid = p_grid.astype(BF16)
        return p_grid, p_meta.astype(BF16), den

    def attend(rows, with_meta_queries):
        geo = []
        for r in rows:
            rs = min(max(r - WIN_ROWS // 2, 0), GRID_ROWS - WIN_ROWS)
            geo.append((N_META + r * GRID_W, N_META + rs * GRID_W, r - rs))
        scores = []
        if with_meta_queries:
            q2 = stack_heads(q_ref[0:N_META, :])
            s_meta = lax.dot_general(q2, k_meta, nt, preferred_element_type=F32) + mb_ref[0, 1, 0:2 * N_META, :]
            scores.append((None, s_meta))
        for q0, k0, typ in geo:
            q2 = stack_heads(q_ref[q0:q0 + GRID_W, :])
            s_grid = lax.dot_general(q2, k_ref[k0:k0 + band, :], nt, preferred_element_type=F32) + tbl_ref[0, typ]
            s_meta = lax.dot_general(q2, k_meta, nt, preferred_element_type=F32) + mb_grid
            scores.append((s_grid, s_meta))
        probs = [softmax_parts(*sc) for sc in scores]
        if with_meta_queries:
            _, p_meta, den = probs.pop(0)
            out = jnp.dot(p_meta, v_meta, preferred_element_type=F32)
            o_ref[0:N_META, :] = unstack_heads(out, den, N_META).astype(o_ref.dtype)
        for (q0, k0, typ), (p_grid, p_meta, den) in zip(geo, probs):
            out = jnp.dot(p_grid, v_ref[k0:k0 + band, :], preferred_element_type=F32)
            out += jnp.dot(p_meta, v_meta, preferred_element_type=F32)
            o_ref[q0:q0 + GRID_W, :] = unstack_heads(out, den, GRID_W).astype(o_ref.dtype)

    for g0 in range(0, GRID_ROWS, NA_ROW_GROUP):
        attend(range(g0, g0 + NA_ROW_GROUP), with_meta_queries=(g0 == 0))


def neighbourhood_attention(proj, rpb, meta_bias, *, bsz, length):
    hp = 2 * NA_HEAD_DIM
    n_pairs = NA_HEADS // 2
    band = WIN_ROWS * GRID_W
    tbl = _na_bias_table(rpb).reshape(n_pairs, 2, WIN_ROWS, GRID_W, band).transpose(0, 2, 1, 3, 4)
    tbl = tbl.reshape(n_pairs, WIN_ROWS, 2 * GRID_W, band)
    mb = meta_bias.astype(F32).reshape(n_pairs, 2, 1, N_META)
    mb_grid = jnp.broadcast_to(mb, (n_pairs, 2, GRID_W, N_META)).reshape(n_pairs, 2 * GRID_W, N_META)
    mb_meta = jnp.broadcast_to(mb, (n_pairs, 2, N_META, N_META)).reshape(n_pairs, 2 * N_META, N_META)
    mb_meta = jnp.pad(mb_meta, ((0, 0), (0, 2 * GRID_W - 2 * N_META), (0, 0)))
    mb2 = jnp.stack([mb_grid, mb_meta], axis=1)
    return pl.pallas_call(
        _na_kernel,
        out_shape=jax.ShapeDtypeStruct((bsz * length, NA_WIDTH), BF16),
        grid=(n_pairs, bsz),
        in_specs=[
            pl.BlockSpec((length, hp), lambda h, b: (b, COL_Q // hp + h)),
            pl.BlockSpec((length, hp), lambda h, b: (b, COL_K // hp + h)),
            pl.BlockSpec((length, hp), lambda h, b: (b, COL_V // hp + h)),
            pl.BlockSpec((1, WIN_ROWS, 2 * GRID_W, band), lambda h, b: (h, 0, 0, 0)),
            pl.BlockSpec((1, 2, 2 * GRID_W, N_META), lambda h, b: (h, 0, 0, 0)),
        ],
        out_specs=pl.BlockSpec((length, hp), lambda h, b: (b, h)),
        compiler_params=_params(("parallel", "parallel")),
        name="neighbourhood_attention",
    )(proj, proj, proj, tbl, mb2)


def _dwconv3(u, w):
    length = u.shape[0]
    row = lax.broadcasted_iota(jnp.int32, u.shape, 0)
    prev = jnp.where(row == 0, 0.0, pltpu.roll(u, 1, 0))
    nxt = jnp.where(row == length - 1, 0.0, pltpu.roll(u, length - 1, 0))
    return prev * w[0:1] + u * w[1:2] + nxt * w[2:3]


def _gated_conv_kernel(b_ref, c_ref, u_ref, w_ref, o_ref):
    u = c_ref[...].astype(F32) * u_ref[...].astype(F32)
    o_ref[...] = (b_ref[...].astype(F32) * _dwconv3(u, w_ref[...])).astype(o_ref.dtype)


def _plain_conv_kernel(u_ref, w_ref, o_ref):
    o_ref[...] = _dwconv3(u_ref[...].astype(F32), w_ref[...]).astype(o_ref.dtype)


CONV_CB = 256


def short_conv_mixer(proj, w_conv, *, bsz, length):
    cb = CONV_CB
    return pl.pallas_call(
        _gated_conv_kernel,
        out_shape=jax.ShapeDtypeStruct((bsz * length, SC_WIDTH), BF16),
        grid=(bsz, SC_WIDTH // cb),
        in_specs=[
            pl.BlockSpec((length, cb), lambda b, c: (b, COL_SCB // cb + c)),
            pl.BlockSpec((length, cb), lambda b, c: (b, COL_SCC // cb + c)),
            pl.BlockSpec((length, cb), lambda b, c: (b, COL_SCU // cb + c)),
            pl.BlockSpec((3, cb), lambda b, c: (0, c)),
        ],
        out_specs=pl.BlockSpec((length, cb), lambda b, c: (b, c)),
        compiler_params=_params(("parallel", "parallel")),
        name="short_conv_mixer",
    )(proj, proj, proj, w_conv.astype(F32))


def hyena_in_conv(proj, w_conv, *, bsz, length):
    cb = CONV_CB
    return pl.pallas_call(
        _plain_conv_kernel,
        out_shape=jax.ShapeDtypeStruct((bsz * length, 3 * HY_WIDTH), BF16),
        grid=(bsz, 3 * HY_WIDTH // cb),
        in_specs=[
            pl.BlockSpec((length, cb), lambda b, c: (b, COL_HYU // cb + c)),
            pl.BlockSpec((3, cb), lambda b, c: (0, c)),
        ],
        out_specs=pl.BlockSpec((length, cb), lambda b, c: (b, c)),
        compiler_params=_params(("parallel", "parallel")),
        name="hyena_in_conv",
    )(proj, w_conv.astype(F32))


def _filter_kernel(z_ref, w1_ref, b1_ref, w2_ref, b2_ref, w3_ref, fr_ref, dec_ref, f_ref, s_ref):
    hi = lax.Precision.HIGHEST
    z = z_ref[...]
    fr = fr_ref[...]
    h = jnp.sin(fr * (jnp.dot(z, w1_ref[...], preferred_element_type=F32, precision=hi) + b1_ref[...]))
    h = jnp.sin(fr * (jnp.dot(h, w2_ref[...], preferred_element_type=F32, precision=hi) + b2_ref[...]))
    h = jnp.dot(h, w3_ref[...], preferred_element_type=F32, precision=hi)
    t = z[:, 0:1]
    f = h * (jnp.exp(-t * jnp.abs(dec_ref[0])) + HY_MOD_SHIFT)
    row = lax.broadcasted_iota(jnp.int32, f.shape, 0)
    backward = pl.program_id(0) % 2 == 1
    f = jnp.where(jnp.logical_and(row == 0, backward), 0.0, f)
    f_ref[0] = f.astype(f_ref.dtype)
    s_ref[0] = jnp.sum(jnp.abs(f), axis=0, keepdims=True)


def hyena_filters(length, w1, b1, w2, b2, w3, freq, decay):
    bands = (HY_EMB - 1) // 2
    t = jnp.linspace(0.0, 1.0, length, dtype=F32)[:, None]
    ang = (2.0 * math.pi / length) * jnp.arange(length, dtype=F32)[:, None]
    fb = jnp.linspace(1e-4, bands - 1, bands, dtype=F32)[None, :]
    z = jnp.concatenate([t, jnp.cos(fb * ang), -jnp.sin(fb * ang)], axis=-1)
    emb = 128
    z = jnp.pad(z, ((0, 0), (0, emb - HY_EMB)))
    w1p = jnp.pad(w1.astype(F32), ((0, emb - HY_EMB), (0, 0)))
    n_slab = HY_ORDER * 2
    row = lambda a: a.astype(F32).reshape(1, -1)
    return pl.pallas_call(
        _filter_kernel,
        out_shape=(jax.ShapeDtypeStruct((n_slab, length, HY_WIDTH), BF16),
                   jax.ShapeDtypeStruct((n_slab, 1, HY_WIDTH), F32)),
        grid=(n_slab,),
        in_specs=[
            pl.BlockSpec((length, emb), lambda j: (0, 0)),
            pl.BlockSpec((emb, HY_HIDDEN), lambda j: (0, 0)),
            pl.BlockSpec((1, HY_HIDDEN), lambda j: (0, 0)),
            pl.BlockSpec((HY_HIDDEN, HY_HIDDEN), lambda j: (0, 0)),
            pl.BlockSpec((1, HY_HIDDEN), lambda j: (0, 0)),
            pl.BlockSpec((HY_HIDDEN, HY_WIDTH), lambda j: (0, j)),
            pl.BlockSpec((1, HY_HIDDEN), lambda j: (0, 0)),
            pl.BlockSpec((1, 1, HY_WIDTH), lambda j: (j, 0, 0)),
        ],
        out_specs=(pl.BlockSpec((1, length, HY_WIDTH), lambda j: (j, 0, 0)),
                   pl.BlockSpec((1, 1, HY_WIDTH), lambda j: (j, 0, 0))),
        compiler_params=_params(("parallel",)),
        name="hyena_filters",
    )(z, w1p, row(b1), w2.astype(F32), row(b2), w3.astype(F32), row(freq),
      decay.astype(F32).reshape(n_slab, 1, HY_WIDTH))


def _dft_matrices(length):
    k = jnp.arange(FFT_KP, dtype=jnp.int32)[:, None]
    n = jnp.arange(length, dtype=jnp.int32)[None, :]
    theta = (2.0 * math.pi / FFT_N) * ((k * n) % FFT_N).astype(F32)
    valid = k <= FFT_N // 2
    fc = jnp.where(valid, jnp.cos(theta), 0.0)
    fs = jnp.where(valid, -jnp.sin(theta), 0.0)
    return fc.astype(BF16), fs.astype(BF16), fc.T.astype(BF16), fs.T.astype(BF16)


def _dft_raw_kernel(fc_ref, fs_ref, u_ref, re_ref, im_ref):
    u = u_ref[0]
    re_ref[0] = jnp.dot(fc_ref[...], u, preferred_element_type=F32)
    im_ref[0] = jnp.dot(fs_ref[...], u, preferred_element_type=F32)


def dft_filters(fc, fs, filt):
    n_slab, length, c = filt.shape
    out = jax.ShapeDtypeStruct((n_slab, FFT_KP, c), F32)
    return pl.pallas_call(
        _dft_raw_kernel,
        out_shape=(out, out),
        grid=(FFT_KP // FFT_KB, n_slab),
        in_specs=[
            pl.BlockSpec((FFT_KB, length), lambda kb, s: (kb, 0)),
            pl.BlockSpec((FFT_KB, length), lambda kb, s: (kb, 0)),
            pl.BlockSpec((1, length, c), lambda kb, s: (s, 0, 0)),
        ],
        out_specs=(pl.BlockSpec((1, FFT_KB, c), lambda kb, s: (s, kb, 0)),
                   pl.BlockSpec((1, FFT_KB, c), lambda kb, s: (s, kb, 0))),
        compiler_params=_params(("parallel", "parallel")),
        name="dft_filters",
    )(fc, fs, filt)


def _dft_mul_kernel(fc_ref, fs_ref, u_ref, are_ref, aim_ref, s_ref, yre_ref, yim_ref, g_ref):
    @pl.when(pl.program_id(1) == 0)
    def _():
        k = pl.program_id(0) * FFT_KB + lax.broadcasted_iota(jnp.int32, (FFT_KB, 1), 0)
        wk = jnp.where(jnp.logical_or(k == 0, k == FFT_N // 2), 1.0, 2.0)
        wk = jnp.where(k > FFT_N // 2, 0.0, wk) / FFT_N
        inv = wk / (s_ref[0] + s_ref[1])
        g_ref[0] = (are_ref[0] + are_ref[1]) * inv
        g_ref[1] = (aim_ref[0] - aim_ref[1]) * inv

    u = u_ref[...]
    re = jnp.dot(fc_ref[...], u, preferred_element_type=F32)
    im = jnp.dot(fs_ref[...], u, preferred_element_type=F32)
    gre = g_ref[0]
    gim = g_ref[1]
    yre_ref[0] = (re * gre - im * gim).astype(yre_ref.dtype)
    yim_ref[0] = (re * gim + im * gre).astype(yim_ref.dtype)


def dft_forward_mul(fc, fs, u, col, a_re, a_im, abs_sum, order, *, bsz, length):
    c = HY_WIDTH
    out = jax.ShapeDtypeStruct((bsz, FFT_KP, c), BF16)
    return pl.pallas_call(
        _dft_mul_kernel,
        out_shape=(out, out),
        grid=(FFT_KP // FFT_KB, bsz),
        in_specs=[
            pl.BlockSpec((FFT_KB, length), lambda kb, b: (kb, 0)),
            pl.BlockSpec((FFT_KB, length), lambda kb, b: (kb, 0)),
            pl.BlockSpec((length, c), lambda kb, b: (b, col)),
            pl.BlockSpec((2, FFT_KB, c), lambda kb, b: (order, kb, 0)),
            pl.BlockSpec((2, FFT_KB, c), lambda kb, b: (order, kb, 0)),
            pl.BlockSpec((2, 1, c), lambda kb, b: (order, 0, 0)),
        ],
        out_specs=(pl.BlockSpec((1, FFT_KB, c), lambda kb, b: (b, kb, 0)),
                   pl.BlockSpec((1, FFT_KB, c), lambda kb, b: (b, kb, 0))),
        scratch_shapes=[pltpu.VMEM((2, FFT_KB, c), F32)],
        compiler_params=_params(("parallel", "arbitrary")),
        name="dft_forward_mul",
    )(fc, fs, u, a_re, a_im, abs_sum)


def _dft_inv_kernel(ct_ref, st_ref, yre_ref, yim_ref, u_ref, gate_ref, skip_ref, o_ref):
    y = jnp.dot(ct_ref[...], yre_ref[0], preferred_element_type=F32)
    y += jnp.dot(st_ref[...], yim_ref[0], preferred_element_type=F32)
    y += u_ref[...].astype(F32) * skip_ref[0]
    o_ref[...] = (gate_ref[...].astype(F32) * y).astype(o_ref.dtype)


def dft_inverse_gate(fct, fst, yre, yim, u, u_col, gate, gate_col, skip, order, *, bsz, length):
    c = HY_WIDTH
    nblk = length // INV_NB
    return pl.pallas_call(
        _dft_inv_kernel,
        out_shape=jax.ShapeDtypeStruct((bsz * length, c), BF16),
        grid=(nblk, bsz),
        in_specs=[
            pl.BlockSpec((INV_NB, FFT_KP), lambda nb, b: (nb, 0)),
            pl.BlockSpec((INV_NB, FFT_KP), lambda nb, b: (nb, 0)),
            pl.BlockSpec((1, FFT_KP, c), lambda nb, b: (b, 0, 0)),
            pl.BlockSpec((1, FFT_KP, c), lambda nb, b: (b, 0, 0)),
            pl.BlockSpec((INV_NB, c), lambda nb, b: (b * nblk + nb, u_col)),
            pl.BlockSpec((INV_NB, c), lambda nb, b: (b * nblk + nb, gate_col)),
            pl.BlockSpec((1, 1, c), lambda nb, b: (order, 0, 0)),
        ],
        out_specs=pl.BlockSpec((INV_NB, c), lambda nb, b: (b * nblk + nb, 0)),
        compiler_params=_params(("parallel", "parallel")),
        name="dft_inverse_gate",
    )(fct, fst, yre, yim, u, gate, skip)


def hyena_mixer(proj, w_conv, w1, b1, w2, b2, w3, freq, decay, skip, dft, *, bsz, length):
    fc, fs, fct, fst = dft
    vxx = hyena_in_conv(proj, w_conv, bsz=bsz, length=length)
    filt, abs_sum = hyena_filters(length, w1, b1, w2, b2, w3, freq, decay)
    a_re, a_im = dft_filters(fc, fs, filt)
    skip = skip.astype(F32).reshape(HY_ORDER, 1, HY_WIDTH)
    kw = dict(bsz=bsz, length=length)
    yre, yim = dft_forward_mul(fc, fs, vxx, 0, a_re, a_im, abs_sum, 0, **kw)
    z = dft_inverse_gate(fct, fst, yre, yim, vxx, 0, vxx, 1, skip, 0, **kw)
    yre, yim = dft_forward_mul(fc, fs, z, 0, a_re, a_im, abs_sum, 1, **kw)
    return dft_inverse_gate(fct, fst, yre, yim, z, 0, vxx, 2, skip, 1, **kw)


def _moe_kernel(be_ref, nb_ref, x_ref, w1_ref, w3_ref, w2_ref, o_ref, acc_ref):
    i = pl.program_id(0)
    j = pl.program_id(1)
    last = pl.num_programs(1) - 1
    used = i < nb_ref[0]

    @pl.when(used)
    def _():
        x = x_ref[...]
        a = jnp.dot(x, w1_ref[0], preferred_element_type=F32)
        b = jnp.dot(x, w3_ref[0], preferred_element_type=F32)
        hid = (a * _sigmoid(a) * b).astype(BF16)
        part = jnp.dot(hid, w2_ref[0], preferred_element_type=F32)

        @pl.when(j == 0)
        def _():
            acc_ref[...] = part

        @pl.when(j > 0)
        def _():
            acc_ref[...] += part

        @pl.when(j == last)
        def _():
            o_ref[...] = acc_ref[...].astype(o_ref.dtype)

    @pl.when(jnp.logical_and(jnp.logical_not(used), j == last))
    def _():
        o_ref[...] = jnp.zeros_like(o_ref)


MOE_FC = 1024


def moe_experts(xb, block_expert, n_used, w1, w3, w2):
    n_slots, d = xb.shape
    n_blocks = n_slots // MOE_BLOCK
    fc = MOE_FC
    n_fc = D_FF_EXPERT // fc

    def chunk(i, j, nb):
        return jnp.where(i < nb[0], j, n_fc - 1)

    grid_spec = pltpu.PrefetchScalarGridSpec(
        num_scalar_prefetch=2,
        grid=(n_blocks, n_fc),
        in_specs=[
            pl.BlockSpec((MOE_BLOCK, d), lambda i, j, be, nb: (i, 0)),
            pl.BlockSpec((1, d, fc), lambda i, j, be, nb: (be[i], 0, chunk(i, j, nb))),
            pl.BlockSpec((1, d, fc), lambda i, j, be, nb: (be[i], 0, chunk(i, j, nb))),
            pl.BlockSpec((1, fc, d), lambda i, j, be, nb: (be[i], chunk(i, j, nb), 0)),
        ],
        out_specs=pl.BlockSpec((MOE_BLOCK, d), lambda i, j, be, nb: (i, 0)),
        scratch_shapes=[pltpu.VMEM((MOE_BLOCK, d), F32)],
    )
    return pl.pallas_call(
        _moe_kernel,
        out_shape=jax.ShapeDtypeStruct((n_slots, d), BF16),
        grid_spec=grid_spec,
        compiler_params=_params(("parallel", "arbitrary")),
        name="moe_experts",
    )(block_expert, n_used, xb, w1, w3, w2)


COMBINE_ROWS = 256


def _moe_combine_kernel(h_ref, y0_ref, y1_ref, g0_ref, g1_ref, gn_ref, o_ref, *, final):
    rows = o_ref.shape[0]
    gn = gn_ref[...]

    def chunk(c, carry):
        r0 = pl.multiple_of(c * NORM_CHUNK, NORM_CHUNK)
        sl = pl.ds(r0, NORM_CHUNK)
        h = h_ref[0, sl, :] if final else h_ref[sl, :]
        y = h + g0_ref[sl, :] * y0_ref[sl, :].astype(F32) + g1_ref[sl, :] * y1_ref[sl, :].astype(F32)
        if final:
            ms = jnp.mean(y * y, axis=-1, keepdims=True)
            y = y * lax.rsqrt(ms + NORM_EPS) * gn
        o_ref[sl, :] = y.astype(o_ref.dtype)
        return carry

    lax.fori_loop(0, rows // NORM_CHUNK, chunk, 0, unroll=4)


def moe_combine(h, y0, y1, g0, g1, *, tm=ROW_TILE):
    t, d = h.shape
    row = lambda w: pl.BlockSpec((tm, w), lambda i: (i, 0))
    return pl.pallas_call(
        functools.partial(_moe_combine_kernel, final=False),
        out_shape=jax.ShapeDtypeStruct((t, d), F32),
        grid=(t // tm,),
        in_specs=[row(d), row(d), row(d), row(1), row(1), pl.BlockSpec((1, d), lambda i: (0, 0))],
        out_specs=row(d),
        compiler_params=_params(("parallel",)),
        name="moe_combine",
    )(h, y0, y1, g0, g1, jnp.ones((1, d), F32))


def moe_combine_final(h, y0, y1, g0, g1, g_final, *, bsz, length):
    d = h.shape[1]
    seq = length - N_META
    r = COMBINE_ROWS
    nr = seq // r
    row = lambda w: pl.BlockSpec((r, w), lambda b, i: (b * nr + i, 0))
    h_spec = pl.BlockSpec((pl.Element(1), pl.Element(r), pl.Element(d)),
                          lambda b, i: (b, pl.multiple_of(N_META + i * r, N_META), 0))
    return pl.pallas_call(
        functools.partial(_moe_combine_kernel, final=True),
        out_shape=jax.ShapeDtypeStruct((bsz, seq, d), F32),
        grid=(bsz, nr),
        in_specs=[h_spec, row(d), row(d), row(1), row(1), pl.BlockSpec((1, d), lambda b, i: (0, 0))],
        out_specs=pl.BlockSpec((None, r, d), lambda b, i: (b, i, 0)),
        compiler_params=_params(("parallel", "parallel")),
        name="moe_combine_final",
    )(h.reshape(bsz, length, d), y0, y1, g0, g1, g_final.astype(F32).reshape(1, d))


SCAN_CHUNK = 256


def _expert_ranks(expert):
    n = expert.shape[0]
    onehot = (expert[:, None] == jnp.arange(N_EXPERTS, dtype=jnp.int32)[None, :]).astype(F32)
    chunks = onehot.reshape(n // SCAN_CHUNK, SCAN_CHUNK, N_EXPERTS)
    tri = jnp.tril(jnp.ones((SCAN_CHUNK, SCAN_CHUNK), F32), k=-1)
    within = jnp.einsum('ij,cjk->cik', tri, chunks)
    totals = jnp.sum(chunks, axis=1)
    before = jnp.cumsum(totals, axis=0) - totals
    rank = jnp.sum((within + before[:, None, :]) * chunks, axis=-1).reshape(n)
    return rank.astype(jnp.int32), jnp.sum(totals, axis=0).astype(jnp.int32)


def moe_swiglu(h, g_norm, w_router, w1, w3, w2, g_final, *, bsz, length):
    n_tok, d = h.shape
    n_assign = n_tok * TOP_K
    router_pad = 128
    wr = jnp.pad(w_router.astype(F32), ((0, 0), (0, router_pad - N_EXPERTS)))
    logits, hn = norm_router(h, g_norm, wr)
    top_val, top_idx = lax.top_k(logits[:, :N_EXPERTS], TOP_K)
    gate = jax.nn.softmax(top_val, axis=-1)
    expert = top_idx.astype(jnp.int32).reshape(n_assign)
    rank, counts = _expert_ranks(expert)
    padded = (counts + MOE_BLOCK - 1) // MOE_BLOCK * MOE_BLOCK
    pad_end = jnp.cumsum(padded)
    pad_start = pad_end - padded
    slot = (pad_start[expert] + rank).reshape(n_tok, TOP_K)
    n_blocks = -(-n_assign // MOE_BLOCK) + N_EXPERTS
    n_slots = n_blocks * MOE_BLOCK
    token = jnp.repeat(jnp.arange(n_tok, dtype=jnp.int32), TOP_K)
    slot_token = jnp.zeros((n_slots,), jnp.int32).at[slot.reshape(n_assign)].set(token)
    n_used = pad_end[-1:] // MOE_BLOCK
    blk = jnp.minimum(jnp.arange(n_blocks, dtype=jnp.int32), n_used - 1)
    block_expert = jnp.minimum(jnp.searchsorted(pad_end, blk * MOE_BLOCK, side='right'),
                               N_EXPERTS - 1).astype(jnp.int32)
    xb = hn[slot_token]
    yb = moe_experts(xb, block_expert, n_used.astype(jnp.int32), w1, w3, w2)
    if g_final is None:
        return moe_combine(h, yb[slot[:, 0]], yb[slot[:, 1]], gate[:, 0:1], gate[:, 1:2])
    keep = lambda a: a.reshape(bsz, length, -1)[:, N_META:].reshape(bsz * (length - N_META), -1)
    slot, gate = keep(slot), keep(gate)
    return moe_combine_final(h, yb[slot[:, 0]], yb[slot[:, 1]], gate[:, 0:1], gate[:, 1:2], g_final,
                             bsz=bsz, length=length)


def kernel(x, meta_tokens, norm_mix, w_in, na_rpb, na_meta_bias, sc_conv, hy_conv, hy_w1, hy_b1, hy_w2, hy_b2,
           hy_w3, hy_freq, hy_decay, hy_skip, w_na_out, w_sc_out, w_hy_out, w_o, norm_ffn, ffn_w1, ffn_w3,
           ffn_w2, router_w, moe_w1, moe_w3, moe_w2, norm_final):
    bsz, seq, d = x.shape
    length = N_META + seq
    depth = w_in.shape[0]
    meta = jnp.broadcast_to(meta_tokens.astype(x.dtype)[None], (bsz, N_META, d))
    h = jnp.concatenate([meta, x], axis=1).reshape(bsz * length, d)
    dft = _dft_matrices(length)
    kw = dict(bsz=bsz, length=length)
    for layer in range(depth):
        proj = norm_matmul(h, norm_mix[layer], w_in[layer].astype(BF16), tn=1024)
        y_na = neighbourhood_attention(proj, na_rpb[layer], na_meta_bias[layer], **kw)
        y_sc = short_conv_mixer(proj, sc_conv[layer], **kw)
        y_hy = hyena_mixer(proj, hy_conv[layer], hy_w1[layer], hy_b1[layer], hy_w2[layer], hy_b2[layer],
                           hy_w3[layer], hy_freq[layer], hy_decay[layer], hy_skip[layer], dft, **kw)
        merged = merge_branches(y_na, y_sc, y_hy, proj, w_na_out[layer].astype(BF16),
                                w_sc_out[layer].astype(BF16), w_hy_out[layer].astype(BF16))
        h = residual_matmul(h, merged, w_o[layer].astype(BF16), tn=1024)
        j = layer // 2
        if layer % 2 == 0:
            up = norm_swiglu_up(h, norm_ffn[layer], ffn_w1[j].astype(BF16), ffn_w3[j].astype(BF16), tn=512)
            h = residual_matmul(h, up, ffn_w2[j].astype(BF16), tn=512)
        else:
            g_final = norm_final if layer == depth - 1 else None
            h = moe_swiglu(h, norm_ffn[layer], router_w[j], moe_w1[j].astype(BF16), moe_w3[j].astype(BF16),
                           moe_w2[j].astype(BF16), g_final, **kw)
    if depth % 2 == 0:
        return h
    out = rms_norm_rows(h, norm_final, out_dtype=x.dtype)
    return out.reshape(bsz, length, d)[:, N_META:]
```

```python
import functools
import math

import numpy as np
import jax
import jax.numpy as jnp
from jax import lax
from jax.experimental import pallas as pl
from jax.experimental.pallas import tpu as pltpu

D_MODEL = 2048
N_META = 16
GRID_W = 64
GRID_ROWS = 32
NA_HEADS = 16
NA_HEAD_DIM = 64
NA_WIDTH = NA_HEADS * NA_HEAD_DIM
WIN_ROWS = 8
WIN_COLS = 16
SC_WIDTH = 512
HY_WIDTH = 512
HY_ORDER = 2
HY_EMB = 33
HY_HIDDEN = 64
HY_MOD_SHIFT = 0.05
PROJ_WIDTH = 3 * NA_WIDTH + 3 * SC_WIDTH + 3 * HY_WIDTH + 3 * D_MODEL
D_FF = 5632
N_EXPERTS = 8
TOP_K = 2
D_FF_EXPERT = 7168
MOE_BLOCK = 512
NORM_EPS = 1e-6
NEG_INF = -1e30

COL_Q, COL_K, COL_V = 0, NA_WIDTH, 2 * NA_WIDTH
COL_SCB = 3 * NA_WIDTH
COL_SCC = COL_SCB + SC_WIDTH
COL_SCU = COL_SCC + SC_WIDTH
COL_HYU = COL_SCU + SC_WIDTH
COL_GNA = COL_HYU + 3 * HY_WIDTH
COL_GSC = COL_GNA + D_MODEL
COL_GHY = COL_GSC + D_MODEL

V7X_VMEM_BYTES = 64 * 1024 * 1024
VMEM_LIMIT = V7X_VMEM_BYTES - 8 * 1024 * 1024

ROW_TILE = 768
NORM_CHUNK = 16
FFT_N = 2 * (N_META + GRID_ROWS * GRID_W)
FFT_KP = 2080
FFT_KB = 1040
INV_NB = 688

F32 = jnp.float32
BF16 = jnp.bfloat16


def _params(sem):
    return pltpu.CompilerParams(dimension_semantics=sem, vmem_limit_bytes=VMEM_LIMIT)


def _sigmoid(a):
    return 0.5 * jnp.tanh(0.5 * a) + 0.5


CAST_BLOCK_BYTES = 8 * 1024 * 1024


def _cast_kernel(x_ref, o_ref):
    o_ref[...] = x_ref[...].astype(o_ref.dtype)


def to_bf16(w):
    cols = w.shape[-1]
    w2 = w.reshape(-1, cols)
    rows = w2.shape[0]
    tr = min(rows, CAST_BLOCK_BYTES // (4 * cols)) // 16 * 16
    while rows % tr:
        tr -= 16
    out = pl.pallas_call(
        _cast_kernel,
        out_shape=jax.ShapeDtypeStruct((rows, cols), BF16),
        grid=(rows // tr,),
        in_specs=[pl.BlockSpec((tr, cols), lambda i: (i, 0))],
        out_specs=pl.BlockSpec((tr, cols), lambda i: (i, 0)),
        compiler_params=_params(("parallel",)),
        name="to_bf16",
    )(w2)
    return out.reshape(w.shape)


def _rms_norm_rows(x_ref, g_ref, xn_ref):
    rows = x_ref.shape[0]
    g = g_ref[...]

    def chunk(c, carry):
        r0 = pl.multiple_of(c * NORM_CHUNK, NORM_CHUNK)
        x = x_ref[pl.ds(r0, NORM_CHUNK), :]
        ms = jnp.mean(x * x, axis=-1, keepdims=True)
        xn_ref[pl.ds(r0, NORM_CHUNK), :] = (x * lax.rsqrt(ms + NORM_EPS) * g).astype(xn_ref.dtype)
        return carry

    lax.fori_loop(0, rows // NORM_CHUNK, chunk, 0, unroll=4)


def _norm_mm_kernel(x_ref, g_ref, w_ref, o_ref, xn_ref):
    @pl.when(pl.program_id(1) == 0)
    def _():
        _rms_norm_rows(x_ref, g_ref, xn_ref)

    o_ref[...] = jnp.dot(xn_ref[...], w_ref[...], preferred_element_type=F32).astype(o_ref.dtype)


def norm_matmul(x, g, w, *, tn, out_dtype=BF16, tm=ROW_TILE):
    t, k = x.shape
    n = w.shape[1]
    return pl.pallas_call(
        _norm_mm_kernel,
        out_shape=jax.ShapeDtypeStruct((t, n), out_dtype),
        grid=(t // tm, n // tn),
        in_specs=[
            pl.BlockSpec((tm, k), lambda i, j: (i, 0)),
            pl.BlockSpec((1, k), lambda i, j: (0, 0)),
            pl.BlockSpec((k, tn), lambda i, j: (0, j)),
        ],
        out_specs=pl.BlockSpec((tm, tn), lambda i, j: (i, j)),
        scratch_shapes=[pltpu.VMEM((tm, k), BF16)],
        compiler_params=_params(("parallel", "arbitrary")),
        name="norm_matmul",
    )(x, g.reshape(1, k), w)


def _norm_swiglu_kernel(x_ref, g_ref, w1_ref, w3_ref, o_ref, xn_ref):
    @pl.when(pl.program_id(1) == 0)
    def _():
        _rms_norm_rows(x_ref, g_ref, xn_ref)

    xn = xn_ref[...]
    a = jnp.dot(xn, w1_ref[...], preferred_element_type=F32)
    b = jnp.dot(xn, w3_ref[...], preferred_element_type=F32)
    o_ref[...] = (a * _sigmoid(a) * b).astype(o_ref.dtype)


def norm_swiglu_up(x, g, w1, w3, *, tn, tm=ROW_TILE):
    t, k = x.shape
    n = w1.shape[1]
    return pl.pallas_call(
        _norm_swiglu_kernel,
        out_shape=jax.ShapeDtypeStruct((t, n), BF16),
        grid=(t // tm, n // tn),
        in_specs=[
            pl.BlockSpec((tm, k), lambda i, j: (i, 0)),
            pl.BlockSpec((1, k), lambda i, j: (0, 0)),
            pl.BlockSpec((k, tn), lambda i, j: (0, j)),
            pl.BlockSpec((k, tn), lambda i, j: (0, j)),
        ],
        out_specs=pl.BlockSpec((tm, tn), lambda i, j: (i, j)),
        scratch_shapes=[pltpu.VMEM((tm, k), BF16)],
        compiler_params=_params(("parallel", "arbitrary")),
        name="norm_swiglu_up",
    )(x, g.reshape(1, k), w1, w3)


def _norm_router_kernel(x_ref, g_ref, w_ref, o_ref, xb_ref, xn_ref):
    _rms_norm_rows(x_ref, g_ref, xn_ref)
    xn = xn_ref[...]
    xb_ref[...] = xn.astype(xb_ref.dtype)
    o_ref[...] = jnp.dot(xn, w_ref[...], preferred_element_type=F32, precision=lax.Precision.HIGHEST)


def norm_router(x, g, w, *, tm=ROW_TILE):
    t, k = x.shape
    n = w.shape[1]
    return pl.pallas_call(
        _norm_router_kernel,
        out_shape=(jax.ShapeDtypeStruct((t, n), F32), jax.ShapeDtypeStruct((t, k), BF16)),
        grid=(t // tm,),
        in_specs=[
            pl.BlockSpec((tm, k), lambda i: (i, 0)),
            pl.BlockSpec((1, k), lambda i: (0, 0)),
            pl.BlockSpec((k, n), lambda i: (0, 0)),
        ],
        out_specs=(pl.BlockSpec((tm, n), lambda i: (i, 0)), pl.BlockSpec((tm, k), lambda i: (i, 0))),
        scratch_shapes=[pltpu.VMEM((tm, k), F32)],
        compiler_params=_params(("parallel",)),
        name="norm_router",
    )(x, g.reshape(1, k), w)


def _norm_kernel(x_ref, g_ref, o_ref):
    _rms_norm_rows(x_ref, g_ref, o_ref)


def rms_norm_rows(x, g, *, out_dtype, tm=ROW_TILE):
    t, k = x.shape
    return pl.pallas_call(
        _norm_kernel,
        out_shape=jax.ShapeDtypeStruct((t, k), out_dtype),
        grid=(t // tm,),
        in_specs=[pl.BlockSpec((tm, k), lambda i: (i, 0)), pl.BlockSpec((1, k), lambda i: (0, 0))],
        out_specs=pl.BlockSpec((tm, k), lambda i: (i, 0)),
        compiler_params=_params(("parallel",)),
        name="rms_norm",
    )(x, g.reshape(1, k))


def _res_mm_kernel(r_ref, x_ref, w_ref, o_ref):
    o_ref[...] = r_ref[...] + jnp.dot(x_ref[...], w_ref[...], preferred_element_type=F32)


def residual_matmul(res, x, w, *, tn, tm=ROW_TILE):
    t, k = x.shape
    n = w.shape[1]
    return pl.pallas_call(
        _res_mm_kernel,
        out_shape=jax.ShapeDtypeStruct((t, n), F32),
        grid=(t // tm, n // tn),
        in_specs=[
            pl.BlockSpec((tm, tn), lambda i, j: (i, j)),
            pl.BlockSpec((tm, k), lambda i, j: (i, 0)),
            pl.BlockSpec((k, tn), lambda i, j: (0, j)),
        ],
        out_specs=pl.BlockSpec((tm, tn), lambda i, j: (i, j)),
        compiler_params=_params(("parallel", "parallel")),
        name="residual_matmul",
    )(res, x, w)


def _merge_kernel(yna_ref, ysc_ref, yhy_ref, gna_ref, gsc_ref, ghy_ref, wna_ref, wsc_ref, why_ref, o_ref):
    acc = _sigmoid(gna_ref[...].astype(F32)) * jnp.dot(yna_ref[...], wna_ref[...], preferred_element_type=F32)
    acc += _sigmoid(gsc_ref[...].astype(F32)) * jnp.dot(ysc_ref[...], wsc_ref[...], preferred_element_type=F32)
    acc += _sigmoid(ghy_ref[...].astype(F32)) * jnp.dot(yhy_ref[...], why_ref[...], preferred_element_type=F32)
    o_ref[...] = acc.astype(o_ref.dtype)


def merge_branches(y_na, y_sc, y_hy, proj, w_na, w_sc, w_hy, *, tn=1024, tm=ROW_TILE):
    t = y_na.shape[0]
    n = D_MODEL
    return pl.pallas_call(
        _merge_kernel,
        out_shape=jax.ShapeDtypeStruct((t, n), BF16),
        grid=(t // tm, n // tn),
        in_specs=[
            pl.BlockSpec((tm, NA_WIDTH), lambda i, j: (i, 0)),
            pl.BlockSpec((tm, SC_WIDTH), lambda i, j: (i, 0)),
            pl.BlockSpec((tm, HY_WIDTH), lambda i, j: (i, 0)),
            pl.BlockSpec((tm, tn), lambda i, j: (i, COL_GNA // tn + j)),
            pl.BlockSpec((tm, tn), lambda i, j: (i, COL_GSC // tn + j)),
            pl.BlockSpec((tm, tn), lambda i, j: (i, COL_GHY // tn + j)),
            pl.BlockSpec((NA_WIDTH, tn), lambda i, j: (0, j)),
            pl.BlockSpec((SC_WIDTH, tn), lambda i, j: (0, j)),
            pl.BlockSpec((HY_WIDTH, tn), lambda i, j: (0, j)),
        ],
        out_specs=pl.BlockSpec((tm, tn), lambda i, j: (i, j)),
        compiler_params=_params(("parallel", "parallel")),
        name="merge_branches",
    )(y_na, y_sc, y_hy, proj, proj, proj, w_na, w_sc, w_hy)


def _na_bias_table(rpb):
    qc = np.arange(GRID_W)[:, None]
    kc = np.arange(GRID_W)[None, :]
    win_start = np.clip(qc - WIN_COLS // 2, 0, GRID_W - WIN_COLS)
    inside = (kc >= win_start) & (kc < win_start + WIN_COLS)
    dc = np.clip(kc - qc, -(WIN_COLS - 1), WIN_COLS - 1) + WIN_COLS - 1
    dr = np.arange(WIN_ROWS)[None, :] - np.arange(WIN_ROWS)[:, None] + WIN_ROWS - 1
    onehot = np.zeros((2 * WIN_COLS - 1, GRID_W * GRID_W), np.float32)
    onehot[dc.reshape(-1), np.arange(GRID_W * GRID_W)] = 1.0
    tbl = jnp.einsum('htid,dj->htij', rpb.astype(F32)[:, dr], jnp.asarray(onehot),
                     precision=lax.Precision.HIGHEST)
    tbl = tbl.reshape(NA_HEADS, WIN_ROWS, WIN_ROWS, GRID_W, GRID_W)
    tbl = jnp.where(jnp.asarray(inside)[None, None, None], tbl, NEG_INF)
    tbl = tbl.transpose(0, 1, 3, 2, 4)
    return tbl.reshape(NA_HEADS, WIN_ROWS, GRID_W, WIN_ROWS * GRID_W)


NA_ROW_GROUP = 4


def _na_kernel(q_ref, k_ref, v_ref, tbl_ref, mb_ref, o_ref):
    scale = NA_HEAD_DIM ** -0.5
    lane = lax.broadcasted_iota(jnp.int32, (1, 2 * NA_HEAD_DIM), 1)
    first = lane < NA_HEAD_DIM
    nt = (((1,), (1,)), ((), ()))
    k_meta = k_ref[0:N_META, :]
    v_meta = v_ref[0:N_META, :]
    mb_grid = mb_ref[0, 0]
    band = WIN_ROWS * GRID_W

    def stack_heads(q):
        q = q * scale
        zero = jnp.zeros_like(q)
        return jnp.concatenate([jnp.where(first, q, zero), jnp.where(first, zero, q)], axis=0)

    def unstack_heads(out, den, n):
        out = out / den
        return jnp.where(first, out[:n], out[n:])

    def softmax_parts(s_grid, s_meta):
        m = jnp.max(s_meta, axis=-1, keepdims=True)
        if s_grid is not None:
            m = jnp.maximum(m, jnp.max(s_grid, axis=-1, keepdims=True))
        p_meta = jnp.exp(s_meta - m)
        den = jnp.sum(p_meta, axis=-1, keepdims=True)
        p_grid = None
        if s_grid is not None:
            p_grid = jnp.exp(s_grid - m)
            den = den + jnp.sum(p_grid, axis=-1, keepdims=True)
            p_grid = p_grid.astype(BF16)
        return p_grid, p_meta.astype(BF16), den

    def attend(rows, with_meta_queries):
        geo = []
        for r in rows:
            rs = min(max(r - WIN_ROWS // 2, 0), GRID_ROWS - WIN_ROWS)
            geo.append((N_META + r * GRID_W, N_META + rs * GRID_W, r - rs))
        scores = []
        if with_meta_queries:
            q2 = stack_heads(q_ref[0:N_META, :])
            s_meta = lax.dot_general(q2, k_meta, nt, preferred_element_type=F32) + mb_ref[0, 1, 0:2 * N_META, :]
            scores.append((None, s_meta))
        for q0, k0, typ in geo:
            q2 = stack_heads(q_ref[q0:q0 + GRID_W, :])
            s_grid = lax.dot_general(q2, k_ref[k0:k0 + band, :], nt, preferred_element_type=F32) + tbl_ref[0, typ]
            s_meta = lax.dot_general(q2, k_meta, nt, preferred_element_type=F32) + mb_grid
            scores.append((s_grid, s_meta))
        probs = [softmax_parts(*sc) for sc in scores]
        if with_meta_queries:
            _, p_meta, den = probs.pop(0)
            out = jnp.dot(p_meta, v_meta, preferred_element_type=F32)
            o_ref[0:N_META, :] = unstack_heads(out, den, N_META).astype(o_ref.dtype)
        for (q0, k0, typ), (p_grid, p_meta, den) in zip(geo, probs):
            out = jnp.dot(p_grid, v_ref[k0:k0 + band, :], preferred_element_type=F32)
            out += jnp.dot(p_meta, v_meta, preferred_element_type=F32)
            o_ref[q0:q0 + GRID_W, :] = unstack_heads(out, den, GRID_W).astype(o_ref.dtype)

    for g0 in range(0, GRID_ROWS, NA_ROW_GROUP):
        attend(range(g0, g0 + NA_ROW_GROUP), with_meta_queries=(g0 == 0))


def neighbourhood_attention(proj, rpb, meta_bias, *, bsz, length):
    hp = 2 * NA_HEAD_DIM
    n_pairs = NA_HEADS // 2
    band = WIN_ROWS * GRID_W
    tbl = _na_bias_table(rpb).reshape(n_pairs, 2, WIN_ROWS, GRID_W, band).transpose(0, 2, 1, 3, 4)
    tbl = tbl.reshape(n_pairs, WIN_ROWS, 2 * GRID_W, band)
    mb = meta_bias.astype(F32).reshape(n_pairs, 2, 1, N_META)
    mb_grid = jnp.broadcast_to(mb, (n_pairs, 2, GRID_W, N_META)).reshape(n_pairs, 2 * GRID_W, N_META)
    mb_meta = jnp.broadcast_to(mb, (n_pairs, 2, N_META, N_META)).reshape(n_pairs, 2 * N_META, N_META)
    mb_meta = jnp.pad(mb_meta, ((0, 0), (0, 2 * GRID_W - 2 * N_META), (0, 0)))
    mb2 = jnp.stack([mb_grid, mb_meta], axis=1)
    return pl.pallas_call(
        _na_kernel,
        out_shape=jax.ShapeDtypeStruct((bsz * length, NA_WIDTH), BF16),
        grid=(n_pairs, bsz),
        in_specs=[
            pl.BlockSpec((length, hp), lambda h, b: (b, COL_Q // hp + h)),
            pl.BlockSpec((length, hp), lambda h, b: (b, COL_K // hp + h)),
            pl.BlockSpec((length, hp), lambda h, b: (b, COL_V // hp + h)),
            pl.BlockSpec((1, WIN_ROWS, 2 * GRID_W, band), lambda h, b: (h, 0, 0, 0)),
            pl.BlockSpec((1, 2, 2 * GRID_W, N_META), lambda h, b: (h, 0, 0, 0)),
        ],
        out_specs=pl.BlockSpec((length, hp), lambda h, b: (b, h)),
        compiler_params=_params(("parallel", "parallel")),
        name="neighbourhood_attention",
    )(proj, proj, proj, tbl, mb2)


def _dwconv3(u, w):
    length = u.shape[0]
    row = lax.broadcasted_iota(jnp.int32, u.shape, 0)
    prev = jnp.where(row == 0, 0.0, pltpu.roll(u, 1, 0))
    nxt = jnp.where(row == length - 1, 0.0, pltpu.roll(u, length - 1, 0))
    return prev * w[0:1] + u * w[1:2] + nxt * w[2:3]


def _gated_conv_kernel(b_ref, c_ref, u_ref, w_ref, o_ref):
    u = c_ref[...].astype(F32) * u_ref[...].astype(F32)
    o_ref[...] = (b_ref[...].astype(F32) * _dwconv3(u, w_ref[...])).astype(o_ref.dtype)


def _plain_conv_kernel(u_ref, w_ref, o_ref):
    o_ref[...] = _dwconv3(u_ref[...].astype(F32), w_ref[...]).astype(o_ref.dtype)


CONV_CB = 256


def short_conv_mixer(proj, w_conv, *, bsz, length):
    cb = CONV_CB
    return pl.pallas_call(
        _gated_conv_kernel,
        out_shape=jax.ShapeDtypeStruct((bsz * length, SC_WIDTH), BF16),
        grid=(bsz, SC_WIDTH // cb),
        in_specs=[
            pl.BlockSpec((length, cb), lambda b, c: (b, COL_SCB // cb + c)),
            pl.BlockSpec((length, cb), lambda b, c: (b, COL_SCC // cb + c)),
            pl.BlockSpec((length, cb), lambda b, c: (b, COL_SCU // cb + c)),
            pl.BlockSpec((3, cb), lambda b, c: (0, c)),
        ],
        out_specs=pl.BlockSpec((length, cb), lambda b, c: (b, c)),
        compiler_params=_params(("parallel", "parallel")),
        name="short_conv_mixer",
    )(proj, proj, proj, w_conv.astype(F32))


def hyena_in_conv(proj, w_conv, *, bsz, length):
    cb = CONV_CB
    return pl.pallas_call(
        _plain_conv_kernel,
        out_shape=jax.ShapeDtypeStruct((bsz * length, 3 * HY_WIDTH), BF16),
        grid=(bsz, 3 * HY_WIDTH // cb),
        in_specs=[
            pl.BlockSpec((length, cb), lambda b, c: (b, COL_HYU // cb + c)),
            pl.BlockSpec((3, cb), lambda b, c: (0, c)),
        ],
        out_specs=pl.BlockSpec((length, cb), lambda b, c: (b, c)),
        compiler_params=_params(("parallel", "parallel")),
        name="hyena_in_conv",
    )(proj, w_conv.astype(F32))


def _filter_kernel(z_ref, w1_ref, b1_ref, w2_ref, b2_ref, w3_ref, fr_ref, dec_ref, f_ref, s_ref):
    hi = lax.Precision.HIGHEST
    z = z_ref[...]
    fr = fr_ref[...]
    h = jnp.sin(fr * (jnp.dot(z, w1_ref[...], preferred_element_type=F32, precision=hi) + b1_ref[...]))
    h = jnp.sin(fr * (jnp.dot(h, w2_ref[...], preferred_element_type=F32, precision=hi) + b2_ref[...]))
    h = jnp.dot(h, w3_ref[...], preferred_element_type=F32, precision=hi)
    t = z[:, 0:1]
    f = h * (jnp.exp(-t * jnp.abs(dec_ref[0])) + HY_MOD_SHIFT)
    row = lax.broadcasted_iota(jnp.int32, f.shape, 0)
    backward = pl.program_id(0) % 2 == 1
    f = jnp.where(jnp.logical_and(row == 0, backward), 0.0, f)
    f_ref[0] = f.astype(f_ref.dtype)
    s_ref[0] = jnp.sum(jnp.abs(f), axis=0, keepdims=True)


def hyena_filters(length, w1, b1, w2, b2, w3, freq, decay):
    bands = (HY_EMB - 1) // 2
    t = jnp.linspace(0.0, 1.0, length, dtype=F32)[:, None]
    ang = (2.0 * math.pi / length) * jnp.arange(length, dtype=F32)[:, None]
    fb = jnp.linspace(1e-4, bands - 1, bands, dtype=F32)[None, :]
    z = jnp.concatenate([t, jnp.cos(fb * ang), -jnp.sin(fb * ang)], axis=-1)
    emb = 128
    z = jnp.pad(z, ((0, 0), (0, emb - HY_EMB)))
    w1p = jnp.pad(w1.astype(F32), ((0, emb - HY_EMB), (0, 0)))
    n_slab = HY_ORDER * 2
    row = lambda a: a.astype(F32).reshape(1, -1)
    return pl.pallas_call(
        _filter_kernel,
        out_shape=(jax.ShapeDtypeStruct((n_slab, length, HY_WIDTH), BF16),
                   jax.ShapeDtypeStruct((n_slab, 1, HY_WIDTH), F32)),
        grid=(n_slab,),
        in_specs=[
            pl.BlockSpec((length, emb), lambda j: (0, 0)),
            pl.BlockSpec((emb, HY_HIDDEN), lambda j: (0, 0)),
            pl.BlockSpec((1, HY_HIDDEN), lambda j: (0, 0)),
            pl.BlockSpec((HY_HIDDEN, HY_HIDDEN), lambda j: (0, 0)),
            pl.BlockSpec((1, HY_HIDDEN), lambda j: (0, 0)),
            pl.BlockSpec((HY_HIDDEN, HY_WIDTH), lambda j: (0, j)),
            pl.BlockSpec((1, HY_HIDDEN), lambda j: (0, 0)),
            pl.BlockSpec((1, 1, HY_WIDTH), lambda j: (j, 0, 0)),
        ],
        out_specs=(pl.BlockSpec((1, length, HY_WIDTH), lambda j: (j, 0, 0)),
                   pl.BlockSpec((1, 1, HY_WIDTH), lambda j: (j, 0, 0))),
        compiler_params=_params(("parallel",)),
        name="hyena_filters",
    )(z, w1p, row(b1), w2.astype(F32), row(b2), w3.astype(F32), row(freq),
      decay.astype(F32).reshape(n_slab, 1, HY_WIDTH))


def _dft_matrices(length):
    k = jnp.arange(FFT_KP, dtype=jnp.int32)[:, None]
    n = jnp.arange(length, dtype=jnp.int32)[None, :]
    theta = (2.0 * math.pi / FFT_N) * ((k * n) % FFT_N).astype(F32)
    valid = k <= FFT_N // 2
    fc = jnp.where(valid, jnp.cos(theta), 0.0)
    fs = jnp.where(valid, -jnp.sin(theta), 0.0)
    return fc.astype(BF16), fs.astype(BF16), fc.T.astype(BF16), fs.T.astype(BF16)


def _dft_raw_kernel(fc_ref, fs_ref, u_ref, re_ref, im_ref):
    u = u_ref[0]
    re_ref[0] = jnp.dot(fc_ref[...], u, preferred_element_type=F32)
    im_ref[0] = jnp.dot(fs_ref[...], u, preferred_element_type=F32)


def dft_filters(fc, fs, filt):
    n_slab, length, c = filt.shape
    out = jax.ShapeDtypeStruct((n_slab, FFT_KP, c), F32)
    return pl.pallas_call(
        _dft_raw_kernel,
        out_shape=(out, out),
        grid=(FFT_KP // FFT_KB, n_slab),
        in_specs=[
            pl.BlockSpec((FFT_KB, length), lambda kb, s: (kb, 0)),
            pl.BlockSpec((FFT_KB, length), lambda kb, s: (kb, 0)),
            pl.BlockSpec((1, length, c), lambda kb, s: (s, 0, 0)),
        ],
        out_specs=(pl.BlockSpec((1, FFT_KB, c), lambda kb, s: (s, kb, 0)),
                   pl.BlockSpec((1, FFT_KB, c), lambda kb, s: (s, kb, 0))),
        compiler_params=_params(("parallel", "parallel")),
        name="dft_filters",
    )(fc, fs, filt)


def _dft_mul_kernel(fc_ref, fs_ref, u_ref, are_ref, aim_ref, s_ref, yre_ref, yim_ref, g_ref):
    @pl.when(pl.program_id(1) == 0)
    def _():
        k = pl.program_id(0) * FFT_KB + lax.broadcasted_iota(jnp.int32, (FFT_KB, 1), 0)
        wk = jnp.where(jnp.logical_or(k == 0, k == FFT_N // 2), 1.0, 2.0)
        wk = jnp.where(k > FFT_N // 2, 0.0, wk) / FFT_N
        inv = wk / (s_ref[0] + s_ref[1])
        g_ref[0] = (are_ref[0] + are_ref[1]) * inv
        g_ref[1] = (aim_ref[0] - aim_ref[1]) * inv

    u = u_ref[...]
    re = jnp.dot(fc_ref[...], u, preferred_element_type=F32)
    im = jnp.dot(fs_ref[...], u, preferred_element_type=F32)
    gre = g_ref[0]
    gim = g_ref[1]
    yre_ref[0] = (re * gre - im * gim).astype(yre_ref.dtype)
    yim_ref[0] = (re * gim + im * gre).astype(yim_ref.dtype)


def dft_forward_mul(fc, fs, u, col, a_re, a_im, abs_sum, order, *, bsz, length):
    c = HY_WIDTH
    out = jax.ShapeDtypeStruct((bsz, FFT_KP, c), BF16)
    return pl.pallas_call(
        _dft_mul_kernel,
        out_shape=(out, out),
        grid=(FFT_KP // FFT_KB, bsz),
        in_specs=[
            pl.BlockSpec((FFT_KB, length), lambda kb, b: (kb, 0)),
            pl.BlockSpec((FFT_KB, length), lambda kb, b: (kb, 0)),
            pl.BlockSpec((length, c), lambda kb, b: (b, col)),
            pl.BlockSpec((2, FFT_KB, c), lambda kb, b: (order, kb, 0)),
            pl.BlockSpec((2, FFT_KB, c), lambda kb, b: (order, kb, 0)),
            pl.BlockSpec((2, 1, c), lambda kb, b: (order, 0, 0)),
        ],
        out_specs=(pl.BlockSpec((1, FFT_KB, c), lambda kb, b: (b, kb, 0)),
                   pl.BlockSpec((1, FFT_KB, c), lambda kb, b: (b, kb, 0))),
        scratch_shapes=[pltpu.VMEM((2, FFT_KB, c), F32)],
        compiler_params=_params(("parallel", "arbitrary")),
        name="dft_forward_mul",
    )(fc, fs, u, a_re, a_im, abs_sum)


def _dft_inv_kernel(ct_ref, st_ref, yre_ref, yim_ref, u_ref, gate_ref, skip_ref, o_ref):
    y = jnp.dot(ct_ref[...], yre_ref[0], preferred_element_type=F32)
    y += jnp.dot(st_ref[...], yim_ref[0], preferred_element_type=F32)
    y += u_ref[...].astype(F32) * skip_ref[0]
    o_ref[...] = (gate_ref[...].astype(F32) * y).astype(o_ref.dtype)


def dft_inverse_gate(fct, fst, yre, yim, u, u_col, gate, gate_col, skip, order, *, bsz, length):
    c = HY_WIDTH
    nblk = length // INV_NB
    return pl.pallas_call(
        _dft_inv_kernel,
        out_shape=jax.ShapeDtypeStruct((bsz * length, c), BF16),
        grid=(nblk, bsz),
        in_specs=[
            pl.BlockSpec((INV_NB, FFT_KP), lambda nb, b: (nb, 0)),
            pl.BlockSpec((INV_NB, FFT_KP), lambda nb, b: (nb, 0)),
            pl.BlockSpec((1, FFT_KP, c), lambda nb, b: (b, 0, 0)),
            pl.BlockSpec((1, FFT_KP, c), lambda nb, b: (b, 0, 0)),
            pl.BlockSpec((INV_NB, c), lambda nb, b: (b * nblk + nb, u_col)),
            pl.BlockSpec((INV_NB, c), lambda nb, b: (b * nblk + nb, gate_col)),
            pl.BlockSpec((1, 1, c), lambda nb, b: (order, 0, 0)),
        ],
        out_specs=pl.BlockSpec((INV_NB, c), lambda nb, b: (b * nblk + nb, 0)),
        compiler_params=_params(("parallel", "parallel")),
        name="dft_inverse_gate",
    )(fct, fst, yre, yim, u, gate, skip)


def hyena_mixer(proj, w_conv, w1, b1, w2, b2, w3, freq, decay, skip, dft, *, bsz, length):
    fc, fs, fct, fst = dft
    vxx = hyena_in_conv(proj, w_conv, bsz=bsz, length=length)
    filt, abs_sum = hyena_filters(length, w1, b1, w2, b2, w3, freq, decay)
    a_re, a_im = dft_filters(fc, fs, filt)
    skip = skip.astype(F32).reshape(HY_ORDER, 1, HY_WIDTH)
    kw = dict(bsz=bsz, length=length)
    yre, yim = dft_forward_mul(fc, fs, vxx, 0, a_re, a_im, abs_sum, 0, **kw)
    z = dft_inverse_gate(fct, fst, yre, yim, vxx, 0, vxx, 1, skip, 0, **kw)
    yre, yim = dft_forward_mul(fc, fs, z, 0, a_re, a_im, abs_sum, 1, **kw)
    return dft_inverse_gate(fct, fst, yre, yim, z, 0, vxx, 2, skip, 1, **kw)


def _moe_kernel(be_ref, nb_ref, x_ref, w1_ref, w3_ref, w2_ref, o_ref, acc_ref):
    i = pl.program_id(0)
    j = pl.program_id(1)
    last = pl.num_programs(1) - 1
    used = i < nb_ref[0]

    @pl.when(used)
    def _():
        x = x_ref[...]
        a = jnp.dot(x, w1_ref[0], preferred_element_type=F32)
        b = jnp.dot(x, w3_ref[0], preferred_element_type=F32)
        hid = (a * _sigmoid(a) * b).astype(BF16)
        part = jnp.dot(hid, w2_ref[0], preferred_element_type=F32)

        @pl.when(j == 0)
        def _():
            acc_ref[...] = part

        @pl.when(j > 0)
        def _():
            acc_ref[...] += part

        @pl.when(j == last)
        def _():
            o_ref[...] = acc_ref[...].astype(o_ref.dtype)

    @pl.when(jnp.logical_and(jnp.logical_not(used), j == last))
    def _():
        o_ref[...] = jnp.zeros_like(o_ref)


MOE_FC = 1024


def moe_experts(xb, block_expert, n_used, w1, w3, w2):
    n_slots, d = xb.shape
    n_blocks = n_slots // MOE_BLOCK
    fc = MOE_FC
    n_fc = D_FF_EXPERT // fc

    def chunk(i, j, nb):
        return jnp.where(i < nb[0], j, n_fc - 1)

    grid_spec = pltpu.PrefetchScalarGridSpec(
        num_scalar_prefetch=2,
        grid=(n_blocks, n_fc),
        in_specs=[
            pl.BlockSpec((MOE_BLOCK, d), lambda i, j, be, nb: (i, 0)),
            pl.BlockSpec((1, d, fc), lambda i, j, be, nb: (be[i], 0, chunk(i, j, nb))),
            pl.BlockSpec((1, d, fc), lambda i, j, be, nb: (be[i], 0, chunk(i, j, nb))),
            pl.BlockSpec((1, fc, d), lambda i, j, be, nb: (be[i], chunk(i, j, nb), 0)),
        ],
        out_specs=pl.BlockSpec((MOE_BLOCK, d), lambda i, j, be, nb: (i, 0)),
        scratch_shapes=[pltpu.VMEM((MOE_BLOCK, d), F32)],
    )
    return pl.pallas_call(
        _moe_kernel,
        out_shape=jax.ShapeDtypeStruct((n_slots, d), BF16),
        grid_spec=grid_spec,
        compiler_params=_params(("parallel", "arbitrary")),
        name="moe_experts",
    )(block_expert, n_used, xb, w1, w3, w2)


COMBINE_ROWS = 256


def _moe_combine_kernel(h_ref, y0_ref, y1_ref, g0_ref, g1_ref, gn_ref, o_ref, *, final):
    rows = o_ref.shape[0]
    gn = gn_ref[...]

    def chunk(c, carry):
        r0 = pl.multiple_of(c * NORM_CHUNK, NORM_CHUNK)
        sl = pl.ds(r0, NORM_CHUNK)
        h = h_ref[0, sl, :] if final else h_ref[sl, :]
        y = h + g0_ref[sl, :] * y0_ref[sl, :].astype(F32) + g1_ref[sl, :] * y1_ref[sl, :].astype(F32)
        if final:
            ms = jnp.mean(y * y, axis=-1, keepdims=True)
            y = y * lax.rsqrt(ms + NORM_EPS) * gn
        o_ref[sl, :] = y.astype(o_ref.dtype)
        return carry

    lax.fori_loop(0, rows // NORM_CHUNK, chunk, 0, unroll=4)


def moe_combine(h, y0, y1, g0, g1, *, tm=ROW_TILE):
    t, d = h.shape
    row = lambda w: pl.BlockSpec((tm, w), lambda i: (i, 0))
    return pl.pallas_call(
        functools.partial(_moe_combine_kernel, final=False),
        out_shape=jax.ShapeDtypeStruct((t, d), F32),
        grid=(t // tm,),
        in_specs=[row(d), row(d), row(d), row(1), row(1), pl.BlockSpec((1, d), lambda i: (0, 0))],
        out_specs=row(d),
        compiler_params=_params(("parallel",)),
        name="moe_combine",
    )(h, y0, y1, g0, g1, jnp.ones((1, d), F32))


def moe_combine_final(h, y0, y1, g0, g1, g_final, *, bsz, length):
    d = h.shape[1]
    seq = length - N_META
    r = COMBINE_ROWS
    nr = seq // r
    row = lambda w: pl.BlockSpec((r, w), lambda b, i: (b * nr + i, 0))
    h_spec = pl.BlockSpec((pl.Element(1), pl.Element(r), pl.Element(d)),
                          lambda b, i: (b, pl.multiple_of(N_META + i * r, N_META), 0))
    return pl.pallas_call(
        functools.partial(_moe_combine_kernel, final=True),
        out_shape=jax.ShapeDtypeStruct((bsz, seq, d), F32),
        grid=(bsz, nr),
        in_specs=[h_spec, row(d), row(d), row(1), row(1), pl.BlockSpec((1, d), lambda b, i: (0, 0))],
        out_specs=pl.BlockSpec((None, r, d), lambda b, i: (b, i, 0)),
        compiler_params=_params(("parallel", "parallel")),
        name="moe_combine_final",
    )(h.reshape(bsz, length, d), y0, y1, g0, g1, g_final.astype(F32).reshape(1, d))


SCAN_CHUNK = 256


def _expert_ranks(expert):
    n = expert.shape[0]
    onehot = (expert[:, None] == jnp.arange(N_EXPERTS, dtype=jnp.int32)[None, :]).astype(F32)
    chunks = onehot.reshape(n // SCAN_CHUNK, SCAN_CHUNK, N_EXPERTS)
    tri = jnp.tril(jnp.ones((SCAN_CHUNK, SCAN_CHUNK), F32), k=-1)
    within = jnp.einsum('ij,cjk->cik', tri, chunks)
    totals = jnp.sum(chunks, axis=1)
    before = jnp.cumsum(totals, axis=0) - totals
    rank = jnp.sum((within + before[:, None, :]) * chunks, axis=-1).reshape(n)
    return rank.astype(jnp.int32), jnp.sum(totals, axis=0).astype(jnp.int32)


def moe_swiglu(h, g_norm, w_router, w1, w3, w2, g_final, *, bsz, length):
    n_tok, d = h.shape
    n_assign = n_tok * TOP_K
    router_pad = 128
    wr = jnp.pad(w_router.astype(F32), ((0, 0), (0, router_pad - N_EXPERTS)))
    logits, hn = norm_router(h, g_norm, wr)
    top_val, top_idx = lax.top_k(logits[:, :N_EXPERTS], TOP_K)
    gate = jax.nn.softmax(top_val, axis=-1)
    expert = top_idx.astype(jnp.int32).reshape(n_assign)
    rank, counts = _expert_ranks(expert)
    padded = (counts + MOE_BLOCK - 1) // MOE_BLOCK * MOE_BLOCK
    pad_end = jnp.cumsum(padded)
    pad_start = pad_end - padded
    slot = (pad_start[expert] + rank).reshape(n_tok, TOP_K)
    n_blocks = -(-n_assign // MOE_BLOCK) + N_EXPERTS
    n_slots = n_blocks * MOE_BLOCK
    token = jnp.repeat(jnp.arange(n_tok, dtype=jnp.int32), TOP_K)
    slot_token = jnp.zeros((n_slots,), jnp.int32).at[slot.reshape(n_assign)].set(token)
    n_used = pad_end[-1:] // MOE_BLOCK
    blk = jnp.minimum(jnp.arange(n_blocks, dtype=jnp.int32), n_used - 1)
    block_expert = jnp.minimum(jnp.searchsorted(pad_end, blk * MOE_BLOCK, side='right'),
                               N_EXPERTS - 1).astype(jnp.int32)
    xb = hn[slot_token]
    yb = moe_experts(xb, block_expert, n_used.astype(jnp.int32), w1, w3, w2)
    if g_final is None:
        return moe_combine(h, yb[slot[:, 0]], yb[slot[:, 1]], gate[:, 0:1], gate[:, 1:2])
    keep = lambda a: a.reshape(bsz, length, -1)[:, N_META:].reshape(bsz * (length - N_META), -1)
    slot, gate = keep(slot), keep(gate)
    return moe_combine_final(h, yb[slot[:, 0]], yb[slot[:, 1]], gate[:, 0:1], gate[:, 1:2], g_final,
                             bsz=bsz, length=length)


def kernel(x, meta_tokens, norm_mix, w_in, na_rpb, na_meta_bias, sc_conv, hy_conv, hy_w1, hy_b1, hy_w2, hy_b2,
           hy_w3, hy_freq, hy_decay, hy_skip, w_na_out, w_sc_out, w_hy_out, w_o, norm_ffn, ffn_w1, ffn_w3,
           ffn_w2, router_w, moe_w1, moe_w3, moe_w2, norm_final):
    bsz, seq, d = x.shape
    length = N_META + seq
    depth = w_in.shape[0]
    meta = jnp.broadcast_to(meta_tokens.astype(x.dtype)[None], (bsz, N_META, d))
    h = jnp.concatenate([meta, x], axis=1).reshape(bsz * length, d)
    dft = _dft_matrices(length)
    kw = dict(bsz=bsz, length=length)
    w_in, w_na_out, w_sc_out, w_hy_out, w_o, ffn_w1, ffn_w3, ffn_w2, moe_w1, moe_w3, moe_w2 = map(
        to_bf16, (w_in, w_na_out, w_sc_out, w_hy_out, w_o, ffn_w1, ffn_w3, ffn_w2, moe_w1, moe_w3, moe_w2))
    for layer in range(depth):
        proj = norm_matmul(h, norm_mix[layer], w_in[layer], tn=2048)
        y_na = neighbourhood_attention(proj, na_rpb[layer], na_meta_bias[layer], **kw)
        y_sc = short_conv_mixer(proj, sc_conv[layer], **kw)
        y_hy = hyena_mixer(proj, hy_conv[layer], hy_w1[layer], hy_b1[layer], hy_w2[layer], hy_b2[layer],
                           hy_w3[layer], hy_freq[layer], hy_decay[layer], hy_skip[layer], dft, **kw)
        merged = merge_branches(y_na, y_sc, y_hy, proj, w_na_out[layer], w_sc_out[layer], w_hy_out[layer])
        h = residual_matmul(h, merged, w_o[layer], tn=1024)
        j = layer // 2
        if layer % 2 == 0:
            up = norm_swiglu_up(h, norm_ffn[layer], ffn_w1[j], ffn_w3[j], tn=512)
            h = residual_matmul(h, up, ffn_w2[j], tn=512)
        else:
            g_final = norm_final if layer == depth - 1 else None
            h = moe_swiglu(h, norm_ffn[layer], router_w[j], moe_w1[j], moe_w3[j], moe_w2[j], g_final, **kw)
    if depth % 2 == 0:
        return h
    out = rms_norm_rows(h, norm_final, out_dtype=x.dtype)
    return out.reshape(bsz, length, d)[:, N_META:]
```

```python
import functools
import math

import numpy as np
import jax
import jax.numpy as jnp
from jax import lax
from jax.experimental import pallas as pl
from jax.experimental.pallas import tpu as pltpu

D_MODEL = 2048
N_META = 16
GRID_W = 64
GRID_ROWS = 32
NA_HEADS = 16
NA_HEAD_DIM = 64
NA_WIDTH = NA_HEADS * NA_HEAD_DIM
WIN_ROWS = 8
WIN_COLS = 16
SC_WIDTH = 512
HY_WIDTH = 512
HY_ORDER = 2
HY_EMB = 33
HY_HIDDEN = 64
HY_MOD_SHIFT = 0.05
PROJ_WIDTH = 3 * NA_WIDTH + 3 * SC_WIDTH + 3 * HY_WIDTH + 3 * D_MODEL
D_FF = 5632
N_EXPERTS = 8
TOP_K = 2
D_FF_EXPERT = 7168
MOE_BLOCK = 512
NORM_EPS = 1e-6
NEG_INF = -1e30

COL_Q, COL_K, COL_V = 0, NA_WIDTH, 2 * NA_WIDTH
COL_SCB = 3 * NA_WIDTH
COL_SCC = COL_SCB + SC_WIDTH
COL_SCU = COL_SCC + SC_WIDTH
COL_HYU = COL_SCU + SC_WIDTH
COL_GNA = COL_HYU + 3 * HY_WIDTH
COL_GSC = COL_GNA + D_MODEL
COL_GHY = COL_GSC + D_MODEL

V7X_VMEM_BYTES = 64 * 1024 * 1024
VMEM_LIMIT = V7X_VMEM_BYTES - 8 * 1024 * 1024

ROW_TILE = 768
NORM_CHUNK = 16
FFT_N = 2 * (N_META + GRID_ROWS * GRID_W)
FFT_KP = 2080
FFT_KB = 1040
INV_NB = 688

F32 = jnp.float32
BF16 = jnp.bfloat16


def _params(sem):
    return pltpu.CompilerParams(dimension_semantics=sem, vmem_limit_bytes=VMEM_LIMIT)


def _sigmoid(a):
    return 0.5 * jnp.tanh(0.5 * a) + 0.5


CAST_BLOCK_BYTES = 8 * 1024 * 1024


def _cast_kernel(x_ref, o_ref):
    o_ref[...] = x_ref[...].astype(o_ref.dtype)


def to_bf16(w):
    cols = w.shape[-1]
    w2 = w.reshape(-1, cols)
    rows = w2.shape[0]
    tr = min(rows, CAST_BLOCK_BYTES // (4 * cols)) // 16 * 16
    while rows % tr:
        tr -= 16
    out = pl.pallas_call(
        _cast_kernel,
        out_shape=jax.ShapeDtypeStruct((rows, cols), BF16),
        grid=(rows // tr,),
        in_specs=[pl.BlockSpec((tr, cols), lambda i: (i, 0))],
        out_specs=pl.BlockSpec((tr, cols), lambda i: (i, 0)),
        compiler_params=_params(("parallel",)),
        name="to_bf16",
    )(w2)
    return out.reshape(w.shape)


def _rms_norm_rows(x_ref, g_ref, xn_ref):
    rows = x_ref.shape[0]
    g = g_ref[...]

    def chunk(c, carry):
        r0 = pl.multiple_of(c * NORM_CHUNK, NORM_CHUNK)
        x = x_ref[pl.ds(r0, NORM_CHUNK), :]
        ms = jnp.mean(x * x, axis=-1, keepdims=True)
        xn_ref[pl.ds(r0, NORM_CHUNK), :] = (x * lax.rsqrt(ms + NORM_EPS) * g).astype(xn_ref.dtype)
        return carry

    lax.fori_loop(0, rows // NORM_CHUNK, chunk, 0, unroll=4)


def _norm_mm_kernel(x_ref, g_ref, w_ref, o_ref, xn_ref):
    @pl.when(pl.program_id(1) == 0)
    def _():
        _rms_norm_rows(x_ref, g_ref, xn_ref)

    o_ref[...] = jnp.dot(xn_ref[...], w_ref[...], preferred_element_type=F32).astype(o_ref.dtype)


def norm_matmul(x, g, w, *, tn, out_dtype=BF16, tm=ROW_TILE):
    t, k = x.shape
    n = w.shape[1]
    return pl.pallas_call(
        _norm_mm_kernel,
        out_shape=jax.ShapeDtypeStruct((t, n), out_dtype),
        grid=(t // tm, n // tn),
        in_specs=[
            pl.BlockSpec((tm, k), lambda i, j: (i, 0)),
            pl.BlockSpec((1, k), lambda i, j: (0, 0)),
            pl.BlockSpec((k, tn), lambda i, j: (0, j)),
        ],
        out_specs=pl.BlockSpec((tm, tn), lambda i, j: (i, j)),
        scratch_shapes=[pltpu.VMEM((tm, k), BF16)],
        compiler_params=_params(("parallel", "arbitrary")),
        name="norm_matmul",
    )(x, g.reshape(1, k), w)


def _norm_swiglu_kernel(x_ref, g_ref, w1_ref, w3_ref, o_ref, xn_ref):
    @pl.when(pl.program_id(1) == 0)
    def _():
        _rms_norm_rows(x_ref, g_ref, xn_ref)

    xn = xn_ref[...]
    a = jnp.dot(xn, w1_ref[...], preferred_element_type=F32)
    b = jnp.dot(xn, w3_ref[...], preferred_element_type=F32)
    o_ref[...] = (a * _sigmoid(a) * b).astype(o_ref.dtype)


def norm_swiglu_up(x, g, w1, w3, *, tn, tm=ROW_TILE):
    t, k = x.shape
    n = w1.shape[1]
    return pl.pallas_call(
        _norm_swiglu_kernel,
        out_shape=jax.ShapeDtypeStruct((t, n), BF16),
        grid=(t // tm, n // tn),
        in_specs=[
            pl.BlockSpec((tm, k), lambda i, j: (i, 0)),
            pl.BlockSpec((1, k), lambda i, j: (0, 0)),
            pl.BlockSpec((k, tn), lambda i, j: (0, j)),
            pl.BlockSpec((k, tn), lambda i, j: (0, j)),
        ],
        out_specs=pl.BlockSpec((tm, tn), lambda i, j: (i, j)),
        scratch_shapes=[pltpu.VMEM((tm, k), BF16)],
        compiler_params=_params(("parallel", "arbitrary")),
        name="norm_swiglu_up",
    )(x, g.reshape(1, k), w1, w3)


def _norm_router_kernel(x_ref, g_ref, w_ref, o_ref, xb_ref, xn_ref):
    _rms_norm_rows(x_ref, g_ref, xn_ref)
    xn = xn_ref[...]
    xb_ref[...] = xn.astype(xb_ref.dtype)
    o_ref[...] = jnp.dot(xn, w_ref[...], preferred_element_type=F32, precision=lax.Precision.HIGHEST)


def norm_router(x, g, w, *, tm=ROW_TILE):
    t, k = x.shape
    n = w.shape[1]
    return pl.pallas_call(
        _norm_router_kernel,
        out_shape=(jax.ShapeDtypeStruct((t, n), F32), jax.ShapeDtypeStruct((t, k), BF16)),
        grid=(t // tm,),
        in_specs=[
            pl.BlockSpec((tm, k), lambda i: (i, 0)),
            pl.BlockSpec((1, k), lambda i: (0, 0)),
            pl.BlockSpec((k, n), lambda i: (0, 0)),
        ],
        out_specs=(pl.BlockSpec((tm, n), lambda i: (i, 0)), pl.BlockSpec((tm, k), lambda i: (i, 0))),
        scratch_shapes=[pltpu.VMEM((tm, k), F32)],
        compiler_params=_params(("parallel",)),
        name="norm_router",
    )(x, g.reshape(1, k), w)


def _norm_kernel(x_ref, g_ref, o_ref):
    _rms_norm_rows(x_ref, g_ref, o_ref)


def rms_norm_rows(x, g, *, out_dtype, tm=ROW_TILE):
    t, k = x.shape
    return pl.pallas_call(
        _norm_kernel,
        out_shape=jax.ShapeDtypeStruct((t, k), out_dtype),
        grid=(t // tm,),
        in_specs=[pl.BlockSpec((tm, k), lambda i: (i, 0)), pl.BlockSpec((1, k), lambda i: (0, 0))],
        out_specs=pl.BlockSpec((tm, k), lambda i: (i, 0)),
        compiler_params=_params(("parallel",)),
        name="rms_norm",
    )(x, g.reshape(1, k))


def _res_mm_kernel(r_ref, x_ref, w_ref, o_ref):
    o_ref[...] = r_ref[...] + jnp.dot(x_ref[...], w_ref[...], preferred_element_type=F32)


def residual_matmul(res, x, w, *, tn, tm=ROW_TILE):
    t, k = x.shape
    n = w.shape[1]
    return pl.pallas_call(
        _res_mm_kernel,
        out_shape=jax.ShapeDtypeStruct((t, n), F32),
        grid=(t // tm, n // tn),
        in_specs=[
            pl.BlockSpec((tm, tn), lambda i, j: (i, j)),
            pl.BlockSpec((tm, k), lambda i, j: (i, 0)),
            pl.BlockSpec((k, tn), lambda i, j: (0, j)),
        ],
        out_specs=pl.BlockSpec((tm, tn), lambda i, j: (i, j)),
        compiler_params=_params(("parallel", "parallel")),
        name="residual_matmul",
    )(res, x, w)


def _merge_kernel(yna_ref, ysc_ref, yhy_ref, gna_ref, gsc_ref, ghy_ref, wna_ref, wsc_ref, why_ref, o_ref):
    acc = _sigmoid(gna_ref[...].astype(F32)) * jnp.dot(yna_ref[...], wna_ref[...], preferred_element_type=F32)
    acc += _sigmoid(gsc_ref[...].astype(F32)) * jnp.dot(ysc_ref[...], wsc_ref[...], preferred_element_type=F32)
    acc += _sigmoid(ghy_ref[...].astype(F32)) * jnp.dot(yhy_ref[...], why_ref[...], preferred_element_type=F32)
    o_ref[...] = acc.astype(o_ref.dtype)


def merge_branches(y_na, y_sc, y_hy, proj, w_na, w_sc, w_hy, *, tn=1024, tm=ROW_TILE):
    t = y_na.shape[0]
    n = D_MODEL
    return pl.pallas_call(
        _merge_kernel,
        out_shape=jax.ShapeDtypeStruct((t, n), BF16),
        grid=(t // tm, n // tn),
        in_specs=[
            pl.BlockSpec((tm, NA_WIDTH), lambda i, j: (i, 0)),
            pl.BlockSpec((tm, SC_WIDTH), lambda i, j: (i, 0)),
            pl.BlockSpec((tm, HY_WIDTH), lambda i, j: (i, 0)),
            pl.BlockSpec((tm, tn), lambda i, j: (i, COL_GNA // tn + j)),
            pl.BlockSpec((tm, tn), lambda i, j: (i, COL_GSC // tn + j)),
            pl.BlockSpec((tm, tn), lambda i, j: (i, COL_GHY // tn + j)),
            pl.BlockSpec((NA_WIDTH, tn), lambda i, j: (0, j)),
            pl.BlockSpec((SC_WIDTH, tn), lambda i, j: (0, j)),
            pl.BlockSpec((HY_WIDTH, tn), lambda i, j: (0, j)),
        ],
        out_specs=pl.BlockSpec((tm, tn), lambda i, j: (i, j)),
        compiler_params=_params(("parallel", "parallel")),
        name="merge_branches",
    )(y_na, y_sc, y_hy, proj, proj, proj, w_na, w_sc, w_hy)


def _na_bias_table(rpb):
    qc = np.arange(GRID_W)[:, None]
    kc = np.arange(GRID_W)[None, :]
    win_start = np.clip(qc - WIN_COLS // 2, 0, GRID_W - WIN_COLS)
    inside = (kc >= win_start) & (kc < win_start + WIN_COLS)
    dc = np.clip(kc - qc, -(WIN_COLS - 1), WIN_COLS - 1) + WIN_COLS - 1
    dr = np.arange(WIN_ROWS)[None, :] - np.arange(WIN_ROWS)[:, None] + WIN_ROWS - 1
    onehot = np.zeros((2 * WIN_COLS - 1, GRID_W * GRID_W), np.float32)
    onehot[dc.reshape(-1), np.arange(GRID_W * GRID_W)] = 1.0
    tbl = jnp.einsum('htid,dj->htij', rpb.astype(F32)[:, dr], jnp.asarray(onehot),
                     precision=lax.Precision.HIGHEST)
    tbl = tbl.reshape(NA_HEADS, WIN_ROWS, WIN_ROWS, GRID_W, GRID_W)
    tbl = jnp.where(jnp.asarray(inside)[None, None, None], tbl, NEG_INF)
    tbl = tbl.transpose(0, 1, 3, 2, 4)
    return tbl.reshape(NA_HEADS, WIN_ROWS, GRID_W, WIN_ROWS * GRID_W)


NA_ROW_GROUP = 4


def _na_kernel(q_ref, k_ref, v_ref, tbl_ref, mb_ref, o_ref):
    scale = NA_HEAD_DIM ** -0.5
    lane = lax.broadcasted_iota(jnp.int32, (1, 2 * NA_HEAD_DIM), 1)
    first = lane < NA_HEAD_DIM
    nt = (((1,), (1,)), ((), ()))
    k_meta = k_ref[0:N_META, :]
    v_meta = v_ref[0:N_META, :]
    mb_grid = mb_ref[0, 0]
    band = WIN_ROWS * GRID_W

    def stack_heads(q):
        q = q * scale
        zero = jnp.zeros_like(q)
        return jnp.concatenate([jnp.where(first, q, zero), jnp.where(first, zero, q)], axis=0)

    def unstack_heads(out, den, n):
        out = out / den
        return jnp.where(first, out[:n], out[n:])

    def softmax_parts(s_grid, s_meta):
        m = jnp.max(s_meta, axis=-1, keepdims=True)
        if s_grid is not None:
            m = jnp.maximum(m, jnp.max(s_grid, axis=-1, keepdims=True))
        p_meta = jnp.exp(s_meta - m)
        den = jnp.sum(p_meta, axis=-1, keepdims=True)
        p_grid = None
        if s_grid is not None:
            p_grid = jnp.exp(s_grid - m)
            den = den + jnp.sum(p_grid, axis=-1, keepdims=True)
            p_grid = p_grid.astype(BF16)
        return p_grid, p_meta.astype(BF16), den

    def attend(rows, with_meta_queries):
        geo = []
        for r in rows:
            rs = min(max(r - WIN_ROWS // 2, 0), GRID_ROWS - WIN_ROWS)
            geo.append((N_META + r * GRID_W, N_META + rs * GRID_W, r - rs))
        scores = []
        if with_meta_queries:
            q2 = stack_heads(q_ref[0:N_META, :])
            s_meta = lax.dot_general(q2, k_meta, nt, preferred_element_type=F32) + mb_ref[0, 1, 0:2 * N_META, :]
            scores.append((None, s_meta))
        for q0, k0, typ in geo:
            q2 = stack_heads(q_ref[q0:q0 + GRID_W, :])
            s_grid = lax.dot_general(q2, k_ref[k0:k0 + band, :], nt, preferred_element_type=F32) + tbl_ref[0, typ]
            s_meta = lax.dot_general(q2, k_meta, nt, preferred_element_type=F32) + mb_grid
            scores.append((s_grid, s_meta))
        probs = [softmax_parts(*sc) for sc in scores]
        if with_meta_queries:
            _, p_meta, den = probs.pop(0)
            out = jnp.dot(p_meta, v_meta, preferred_element_type=F32)
            o_ref[0:N_META, :] = unstack_heads(out, den, N_META).astype(o_ref.dtype)
        for (q0, k0, typ), (p_grid, p_meta, den) in zip(geo, probs):
            out = jnp.dot(p_grid, v_ref[k0:k0 + band, :], preferred_element_type=F32)
            out += jnp.dot(p_meta, v_meta, preferred_element_type=F32)
            o_ref[q0:q0 + GRID_W, :] = unstack_heads(out, den, GRID_W).astype(o_ref.dtype)

    for g0 in range(0, GRID_ROWS, NA_ROW_GROUP):
        attend(range(g0, g0 + NA_ROW_GROUP), with_meta_queries=(g0 == 0))


def neighbourhood_attention(proj, rpb, meta_bias, *, bsz, length):
    hp = 2 * NA_HEAD_DIM
    n_pairs = NA_HEADS // 2
    band = WIN_ROWS * GRID_W
    tbl = _na_bias_table(rpb).reshape(n_pairs, 2, WIN_ROWS, GRID_W, band).transpose(0, 2, 1, 3, 4)
    tbl = tbl.reshape(n_pairs, WIN_ROWS, 2 * GRID_W, band)
    mb = meta_bias.astype(F32).reshape(n_pairs, 2, 1, N_META)
    mb_grid = jnp.broadcast_to(mb, (n_pairs, 2, GRID_W, N_META)).reshape(n_pairs, 2 * GRID_W, N_META)
    mb_meta = jnp.broadcast_to(mb, (n_pairs, 2, N_META, N_META)).reshape(n_pairs, 2 * N_META, N_META)
    mb_meta = jnp.pad(mb_meta, ((0, 0), (0, 2 * GRID_W - 2 * N_META), (0, 0)))
    mb2 = jnp.stack([mb_grid, mb_meta], axis=1)
    return pl.pallas_call(
        _na_kernel,
        out_shape=jax.ShapeDtypeStruct((bsz * length, NA_WIDTH), BF16),
        grid=(n_pairs, bsz),
        in_specs=[
            pl.BlockSpec((length, hp), lambda h, b: (b, COL_Q // hp + h)),
            pl.BlockSpec((length, hp), lambda h, b: (b, COL_K // hp + h)),
            pl.BlockSpec((length, hp), lambda h, b: (b, COL_V // hp + h)),
            pl.BlockSpec((1, WIN_ROWS, 2 * GRID_W, band), lambda h, b: (h, 0, 0, 0)),
            pl.BlockSpec((1, 2, 2 * GRID_W, N_META), lambda h, b: (h, 0, 0, 0)),
        ],
        out_specs=pl.BlockSpec((length, hp), lambda h, b: (b, h)),
        compiler_params=_params(("parallel", "parallel")),
        name="neighbourhood_attention",
    )(proj, proj, proj, tbl, mb2)


def _dwconv3(u, w):
    length = u.shape[0]
    row = lax.broadcasted_iota(jnp.int32, u.shape, 0)
    prev = jnp.where(row == 0, 0.0, pltpu.roll(u, 1, 0))
    nxt = jnp.where(row == length - 1, 0.0, pltpu.roll(u, length - 1, 0))
    return prev * w[0:1] + u * w[1:2] + nxt * w[2:3]


def _gated_conv_kernel(b_ref, c_ref, u_ref, w_ref, o_ref):
    u = c_ref[...].astype(F32) * u_ref[...].astype(F32)
    o_ref[...] = (b_ref[...].astype(F32) * _dwconv3(u, w_ref[...])).astype(o_ref.dtype)


def _plain_conv_kernel(u_ref, w_ref, o_ref):
    o_ref[...] = _dwconv3(u_ref[...].astype(F32), w_ref[...]).astype(o_ref.dtype)


CONV_CB = 256


def short_conv_mixer(proj, w_conv, *, bsz, length):
    cb = CONV_CB
    return pl.pallas_call(
        _gated_conv_kernel,
        out_shape=jax.ShapeDtypeStruct((bsz * length, SC_WIDTH), BF16),
        grid=(bsz, SC_WIDTH // cb),
        in_specs=[
            pl.BlockSpec((length, cb), lambda b, c: (b, COL_SCB // cb + c)),
            pl.BlockSpec((length, cb), lambda b, c: (b, COL_SCC // cb + c)),
            pl.BlockSpec((length, cb), lambda b, c: (b, COL_SCU // cb + c)),
            pl.BlockSpec((3, cb), lambda b, c: (0, c)),
        ],
        out_specs=pl.BlockSpec((length, cb), lambda b, c: (b, c)),
        compiler_params=_params(("parallel", "parallel")),
        name="short_conv_mixer",
    )(proj, proj, proj, w_conv.astype(F32))


def hyena_in_conv(proj, w_conv, *, bsz, length):
    cb = CONV_CB
    return pl.pallas_call(
        _plain_conv_kernel,
        out_shape=jax.ShapeDtypeStruct((bsz * length, 3 * HY_WIDTH), BF16),
        grid=(bsz, 3 * HY_WIDTH // cb),
        in_specs=[
            pl.BlockSpec((length, cb), lambda b, c: (b, COL_HYU // cb + c)),
            pl.BlockSpec((3, cb), lambda b, c: (0, c)),
        ],
        out_specs=pl.BlockSpec((length, cb), lambda b, c: (b, c)),
        compiler_params=_params(("parallel", "parallel")),
        name="hyena_in_conv",
    )(proj, w_conv.astype(F32))


def _filter_kernel(z_ref, w1_ref, b1_ref, w2_ref, b2_ref, w3_ref, fr_ref, dec_ref, f_ref, s_ref):
    hi = lax.Precision.HIGHEST
    z = z_ref[...]
    fr = fr_ref[...]
    h = jnp.sin(fr * (jnp.dot(z, w1_ref[...], preferred_element_type=F32, precision=hi) + b1_ref[...]))
    h = jnp.sin(fr * (jnp.dot(h, w2_ref[...], preferred_element_type=F32, precision=hi) + b2_ref[...]))
    h = jnp.dot(h, w3_ref[...], preferred_element_type=F32, precision=hi)
    t = z[:, 0:1]
    f = h * (jnp.exp(-t * jnp.abs(dec_ref[0])) + HY_MOD_SHIFT)
    row = lax.broadcasted_iota(jnp.int32, f.shape, 0)
    backward = pl.program_id(0) % 2 == 1
    f = jnp.where(jnp.logical_and(row == 0, backward), 0.0, f)
    f_ref[0] = f.astype(f_ref.dtype)
    s_ref[0] = jnp.sum(jnp.abs(f), axis=0, keepdims=True)


def hyena_filters(length, w1, b1, w2, b2, w3, freq, decay):
    bands = (HY_EMB - 1) // 2
    t = jnp.linspace(0.0, 1.0, length, dtype=F32)[:, None]
    ang = (2.0 * math.pi / length) * jnp.arange(length, dtype=F32)[:, None]
    fb = jnp.linspace(1e-4, bands - 1, bands, dtype=F32)[None, :]
    z = jnp.concatenate([t, jnp.cos(fb * ang), -jnp.sin(fb * ang)], axis=-1)
    emb = 128
    z = jnp.pad(z, ((0, 0), (0, emb - HY_EMB)))
    w1p = jnp.pad(w1.astype(F32), ((0, emb - HY_EMB), (0, 0)))
    n_slab = HY_ORDER * 2
    row = lambda a: a.astype(F32).reshape(1, -1)
    return pl.pallas_call(
        _filter_kernel,
        out_shape=(jax.ShapeDtypeStruct((n_slab, length, HY_WIDTH), BF16),
                   jax.ShapeDtypeStruct((n_slab, 1, HY_WIDTH), F32)),
        grid=(n_slab,),
        in_specs=[
            pl.BlockSpec((length, emb), lambda j: (0, 0)),
            pl.BlockSpec((emb, HY_HIDDEN), lambda j: (0, 0)),
            pl.BlockSpec((1, HY_HIDDEN), lambda j: (0, 0)),
            pl.BlockSpec((HY_HIDDEN, HY_HIDDEN), lambda j: (0, 0)),
            pl.BlockSpec((1, HY_HIDDEN), lambda j: (0, 0)),
            pl.BlockSpec((HY_HIDDEN, HY_WIDTH), lambda j: (0, j)),
            pl.BlockSpec((1, HY_HIDDEN), lambda j: (0, 0)),
            pl.BlockSpec((1, 1, HY_WIDTH), lambda j: (j, 0, 0)),
        ],
        out_specs=(pl.BlockSpec((1, length, HY_WIDTH), lambda j: (j, 0, 0)),
                   pl.BlockSpec((1, 1, HY_WIDTH), lambda j: (j, 0, 0))),
        compiler_params=_params(("parallel",)),
        name="hyena_filters",
    )(z, w1p, row(b1), w2.astype(F32), row(b2), w3.astype(F32), row(freq),
      decay.astype(F32).reshape(n_slab, 1, HY_WIDTH))


def _dft_matrices(length):
    k = jnp.arange(FFT_KP, dtype=jnp.int32)[:, None]
    n = jnp.arange(length, dtype=jnp.int32)[None, :]
    theta = (2.0 * math.pi / FFT_N) * ((k * n) % FFT_N).astype(F32)
    valid = k <= FFT_N // 2
    fc = jnp.where(valid, jnp.cos(theta), 0.0)
    fs = jnp.where(valid, -jnp.sin(theta), 0.0)
    return fc.astype(BF16), fs.astype(BF16), fc.T.astype(BF16), fs.T.astype(BF16)


def _dft_raw_kernel(fc_ref, fs_ref, u_ref, re_ref, im_ref):
    u = u_ref[0]
    re_ref[0] = jnp.dot(fc_ref[...], u, preferred_element_type=F32)
    im_ref[0] = jnp.dot(fs_ref[...], u, preferred_element_type=F32)


def dft_filters(fc, fs, filt):
    n_slab, length, c = filt.shape
    out = jax.ShapeDtypeStruct((n_slab, FFT_KP, c), F32)
    return pl.pallas_call(
        _dft_raw_kernel,
        out_shape=(out, out),
        grid=(FFT_KP // FFT_KB, n_slab),
        in_specs=[
            pl.BlockSpec((FFT_KB, length), lambda kb, s: (kb, 0)),
            pl.BlockSpec((FFT_KB, length), lambda kb, s: (kb, 0)),
            pl.BlockSpec((1, length, c), lambda kb, s: (s, 0, 0)),
        ],
        out_specs=(pl.BlockSpec((1, FFT_KB, c), lambda kb, s: (s, kb, 0)),
                   pl.BlockSpec((1, FFT_KB, c), lambda kb, s: (s, kb, 0))),
        compiler_params=_params(("parallel", "parallel")),
        name="dft_filters",
    )(fc, fs, filt)


def _dft_mul_kernel(fc_ref, fs_ref, u_ref, are_ref, aim_ref, s_ref, yre_ref, yim_ref, g_ref):
    @pl.when(pl.program_id(1) == 0)
    def _():
        k = pl.program_id(0) * FFT_KB + lax.broadcasted_iota(jnp.int32, (FFT_KB, 1), 0)
        wk = jnp.where(jnp.logical_or(k == 0, k == FFT_N // 2), 1.0, 2.0)
        wk = jnp.where(k > FFT_N // 2, 0.0, wk) / FFT_N
        inv = wk / (s_ref[0] + s_ref[1])
        g_ref[0] = (are_ref[0] + are_ref[1]) * inv
        g_ref[1] = (aim_ref[0] - aim_ref[1]) * inv

    u = u_ref[...]
    re = jnp.dot(fc_ref[...], u, preferred_element_type=F32)
    im = jnp.dot(fs_ref[...], u, preferred_element_type=F32)
    gre = g_ref[0]
    gim = g_ref[1]
    yre_ref[0] = (re * gre - im * gim).astype(yre_ref.dtype)
    yim_ref[0] = (re * gim + im * gre).astype(yim_ref.dtype)


def dft_forward_mul(fc, fs, u, col, a_re, a_im, abs_sum, order, *, bsz, length):
    c = HY_WIDTH
    out = jax.ShapeDtypeStruct((bsz, FFT_KP, c), BF16)
    return pl.pallas_call(
        _dft_mul_kernel,
        out_shape=(out, out),
        grid=(FFT_KP // FFT_KB, bsz),
        in_specs=[
            pl.BlockSpec((FFT_KB, length), lambda kb, b: (kb, 0)),
            pl.BlockSpec((FFT_KB, length), lambda kb, b: (kb, 0)),
            pl.BlockSpec((length, c), lambda kb, b: (b, col)),
            pl.BlockSpec((2, FFT_KB, c), lambda kb, b: (order, kb, 0)),
            pl.BlockSpec((2, FFT_KB, c), lambda kb, b: (order, kb, 0)),
            pl.BlockSpec((2, 1, c), lambda kb, b: (order, 0, 0)),
        ],
        out_specs=(pl.BlockSpec((1, FFT_KB, c), lambda kb, b: (b, kb, 0)),
                   pl.BlockSpec((1, FFT_KB, c), lambda kb, b: (b, kb, 0))),
        scratch_shapes=[pltpu.VMEM((2, FFT_KB, c), F32)],
        compiler_params=_params(("parallel", "arbitrary")),
        name="dft_forward_mul",
    )(fc, fs, u, a_re, a_im, abs_sum)


def _dft_inv_kernel(ct_ref, st_ref, yre_ref, yim_ref, u_ref, gate_ref, skip_ref, o_ref):
    y = jnp.dot(ct_ref[...], yre_ref[0], preferred_element_type=F32)
    y += jnp.dot(st_ref[...], yim_ref[0], preferred_element_type=F32)
    y += u_ref[...].astype(F32) * skip_ref[0]
    o_ref[...] = (gate_ref[...].astype(F32) * y).astype(o_ref.dtype)


def dft_inverse_gate(fct, fst, yre, yim, u, u_col, gate, gate_col, skip, order, *, bsz, length):
    c = HY_WIDTH
    nblk = length // INV_NB
    return pl.pallas_call(
        _dft_inv_kernel,
        out_shape=jax.ShapeDtypeStruct((bsz * length, c), BF16),
        grid=(nblk, bsz),
        in_specs=[
            pl.BlockSpec((INV_NB, FFT_KP), lambda nb, b: (nb, 0)),
            pl.BlockSpec((INV_NB, FFT_KP), lambda nb, b: (nb, 0)),
            pl.BlockSpec((1, FFT_KP, c), lambda nb, b: (b, 0, 0)),
            pl.BlockSpec((1, FFT_KP, c), lambda nb, b: (b, 0, 0)),
            pl.BlockSpec((INV_NB, c), lambda nb, b: (b * nblk + nb, u_col)),
            pl.BlockSpec((INV_NB, c), lambda nb, b: (b * nblk + nb, gate_col)),
            pl.BlockSpec((1, 1, c), lambda nb, b: (order, 0, 0)),
        ],
        out_specs=pl.BlockSpec((INV_NB, c), lambda nb, b: (b * nblk + nb, 0)),
        compiler_params=_params(("parallel", "parallel")),
        name="dft_inverse_gate",
    )(fct, fst, yre, yim, u, gate, skip)


def hyena_mixer(proj, w_conv, w1, b1, w2, b2, w3, freq, decay, skip, dft, *, bsz, length):
    fc, fs, fct, fst = dft
    vxx = hyena_in_conv(proj, w_conv, bsz=bsz, length=length)
    filt, abs_sum = hyena_filters(length, w1, b1, w2, b2, w3, freq, decay)
    a_re, a_im = dft_filters(fc, fs, filt)
    skip = skip.astype(F32).reshape(HY_ORDER, 1, HY_WIDTH)
    kw = dict(bsz=bsz, length=length)
    yre, yim = dft_forward_mul(fc, fs, vxx, 0, a_re, a_im, abs_sum, 0, **kw)
    z = dft_inverse_gate(fct, fst, yre, yim, vxx, 0, vxx, 1, skip, 0, **kw)
    yre, yim = dft_forward_mul(fc, fs, z, 0, a_re, a_im, abs_sum, 1, **kw)
    return dft_inverse_gate(fct, fst, yre, yim, z, 0, vxx, 2, skip, 1, **kw)


MOE_SUPER = 2 * MOE_BLOCK
MOE_FC = 512


def _moe_kernel(be_ref, nh_ref, x_ref, w1_ref, w3_ref, w2_ref, o_ref, acc_ref):
    i = pl.program_id(0)
    j = pl.program_id(1)
    halves = nh_ref[i]

    @pl.when(j == 0)
    def _():
        acc_ref[...] = jnp.zeros_like(acc_ref)

    def ffn(rows):
        x = x_ref[0:rows, :]
        a = jnp.dot(x, w1_ref[0].astype(BF16), preferred_element_type=F32)
        b = jnp.dot(x, w3_ref[0].astype(BF16), preferred_element_type=F32)
        hid = (a * _sigmoid(a) * b).astype(BF16)
        acc_ref[0:rows, :] += jnp.dot(hid, w2_ref[0], preferred_element_type=F32)

    @pl.when(halves == 2)
    def _():
        ffn(MOE_SUPER)

    @pl.when(halves == 1)
    def _():
        ffn(MOE_BLOCK)

    @pl.when(j == pl.num_programs(1) - 1)
    def _():
        o_ref[...] = acc_ref[...].astype(o_ref.dtype)


def moe_experts(xb, block_expert, n_halves, w1, w3, w2):
    n_slots, d = xb.shape
    n_super = n_slots // MOE_SUPER
    fc = MOE_FC
    n_fc = D_FF_EXPERT // fc

    def chunk(i, j, nh):
        return jnp.where(nh[i] > 0, j, n_fc - 1)

    grid_spec = pltpu.PrefetchScalarGridSpec(
        num_scalar_prefetch=2,
        grid=(n_super, n_fc),
        in_specs=[
            pl.BlockSpec((MOE_SUPER, d), lambda i, j, be, nh: (i, 0)),
            pl.BlockSpec((1, d, fc), lambda i, j, be, nh: (be[i], 0, chunk(i, j, nh))),
            pl.BlockSpec((1, d, fc), lambda i, j, be, nh: (be[i], 0, chunk(i, j, nh))),
            pl.BlockSpec((1, fc, d), lambda i, j, be, nh: (be[i], chunk(i, j, nh), 0)),
        ],
        out_specs=pl.BlockSpec((MOE_SUPER, d), lambda i, j, be, nh: (i, 0)),
        scratch_shapes=[pltpu.VMEM((MOE_SUPER, d), F32)],
    )
    return pl.pallas_call(
        _moe_kernel,
        out_shape=jax.ShapeDtypeStruct((n_slots, d), BF16),
        grid_spec=grid_spec,
        compiler_params=_params(("parallel", "arbitrary")),
        name="moe_experts",
    )(block_expert, n_halves, xb, w1, w3, w2)


COMBINE_ROWS = 256


def _moe_combine_kernel(h_ref, y0_ref, y1_ref, g0_ref, g1_ref, gn_ref, o_ref, *, final):
    rows = o_ref.shape[0]
    gn = gn_ref[...]

    def chunk(c, carry):
        r0 = pl.multiple_of(c * NORM_CHUNK, NORM_CHUNK)
        sl = pl.ds(r0, NORM_CHUNK)
        h = h_ref[0, sl, :] if final else h_ref[sl, :]
        y = h + g0_ref[sl, :] * y0_ref[sl, :].astype(F32) + g1_ref[sl, :] * y1_ref[sl, :].astype(F32)
        if final:
            ms = jnp.mean(y * y, axis=-1, keepdims=True)
            y = y * lax.rsqrt(ms + NORM_EPS) * gn
        o_ref[sl, :] = y.astype(o_ref.dtype)
        return carry

    lax.fori_loop(0, rows // NORM_CHUNK, chunk, 0, unroll=4)


def moe_combine(h, y0, y1, g0, g1, *, tm=ROW_TILE):
    t, d = h.shape
    row = lambda w: pl.BlockSpec((tm, w), lambda i: (i, 0))
    return pl.pallas_call(
        functools.partial(_moe_combine_kernel, final=False),
        out_shape=jax.ShapeDtypeStruct((t, d), F32),
        grid=(t // tm,),
        in_specs=[row(d), row(d), row(d), row(1), row(1), pl.BlockSpec((1, d), lambda i: (0, 0))],
        out_specs=row(d),
        compiler_params=_params(("parallel",)),
        name="moe_combine",
    )(h, y0, y1, g0, g1, jnp.ones((1, d), F32))


def moe_combine_final(h, y0, y1, g0, g1, g_final, *, bsz, length):
    d = h.shape[1]
    seq = length - N_META
    r = COMBINE_ROWS
    nr = seq // r
    row = lambda w: pl.BlockSpec((r, w), lambda b, i: (b * nr + i, 0))
    h_spec = pl.BlockSpec((pl.Element(1), pl.Element(r), pl.Element(d)),
                          lambda b, i: (b, pl.multiple_of(N_META + i * r, N_META), 0))
    return pl.pallas_call(
        functools.partial(_moe_combine_kernel, final=True),
        out_shape=jax.ShapeDtypeStruct((bsz, seq, d), F32),
        grid=(bsz, nr),
        in_specs=[h_spec, row(d), row(d), row(1), row(1), pl.BlockSpec((1, d), lambda b, i: (0, 0))],
        out_specs=pl.BlockSpec((None, r, d), lambda b, i: (b, i, 0)),
        compiler_params=_params(("parallel", "parallel")),
        name="moe_combine_final",
    )(h.reshape(bsz, length, d), y0, y1, g0, g1, g_final.astype(F32).reshape(1, d))


SCAN_CHUNK = 256


def _expert_ranks(expert):
    n = expert.shape[0]
    onehot = (expert[:, None] == jnp.arange(N_EXPERTS, dtype=jnp.int32)[None, :]).astype(F32)
    chunks = onehot.reshape(n // SCAN_CHUNK, SCAN_CHUNK, N_EXPERTS)
    tri = jnp.tril(jnp.ones((SCAN_CHUNK, SCAN_CHUNK), F32), k=-1)
    within = jnp.einsum('ij,cjk->cik', tri, chunks)
    totals = jnp.sum(chunks, axis=1)
    before = jnp.cumsum(totals, axis=0) - totals
    rank = jnp.sum((within + before[:, None, :]) * chunks, axis=-1).reshape(n)
    return rank.astype(jnp.int32), jnp.sum(totals, axis=0).astype(jnp.int32)


def moe_swiglu(h, g_norm, w_router, w1, w3, w2, g_final, *, bsz, length):
    n_tok, d = h.shape
    n_assign = n_tok * TOP_K
    router_pad = 128
    wr = jnp.pad(w_router.astype(F32), ((0, 0), (0, router_pad - N_EXPERTS)))
    logits, hn = norm_router(h, g_norm, wr)
    top_val, top_idx = lax.top_k(logits[:, :N_EXPERTS], TOP_K)
    gate = jax.nn.softmax(top_val, axis=-1)
    expert = top_idx.astype(jnp.int32).reshape(n_assign)
    rank, counts = _expert_ranks(expert)
    padded = (counts + MOE_SUPER - 1) // MOE_SUPER * MOE_SUPER
    pad_end = jnp.cumsum(padded)
    pad_start = pad_end - padded
    slot = (pad_start[expert] + rank).reshape(n_tok, TOP_K)
    n_super = -(-n_assign // MOE_SUPER) + N_EXPERTS
    n_slots = n_super * MOE_SUPER
    token = jnp.repeat(jnp.arange(n_tok, dtype=jnp.int32), TOP_K)
    slot_token = jnp.zeros((n_slots,), jnp.int32).at[slot.reshape(n_assign)].set(token)
    n_used = pad_end[-1] // MOE_SUPER
    sb = jnp.arange(n_super, dtype=jnp.int32)
    block_expert = jnp.minimum(jnp.searchsorted(pad_end, jnp.minimum(sb, n_used - 1) * MOE_SUPER, side='right'),
                               N_EXPERTS - 1).astype(jnp.int32)
    rows_here = jnp.clip(counts[block_expert] - (sb * MOE_SUPER - pad_start[block_expert]), 0, MOE_SUPER)
    n_halves = jnp.where(sb < n_used, (rows_here + MOE_BLOCK - 1) // MOE_BLOCK, 0).astype(jnp.int32)
    xb = hn[slot_token]
    yb = moe_experts(xb, block_expert, n_halves, w1, w3, w2)
    if g_final is None:
        return moe_combine(h, yb[slot[:, 0]], yb[slot[:, 1]], gate[:, 0:1], gate[:, 1:2])
    keep = lambda a: a.reshape(bsz, length, -1)[:, N_META:].reshape(bsz * (length - N_META), -1)
    slot, gate = keep(slot), keep(gate)
    return moe_combine_final(h, yb[slot[:, 0]], yb[slot[:, 1]], gate[:, 0:1], gate[:, 1:2], g_final,
                             bsz=bsz, length=length)


def kernel(x, meta_tokens, norm_mix, w_in, na_rpb, na_meta_bias, sc_conv, hy_conv, hy_w1, hy_b1, hy_w2, hy_b2,
           hy_w3, hy_freq, hy_decay, hy_skip, w_na_out, w_sc_out, w_hy_out, w_o, norm_ffn, ffn_w1, ffn_w3,
           ffn_w2, router_w, moe_w1, moe_w3, moe_w2, norm_final):
    bsz, seq, d = x.shape
    length = N_META + seq
    depth = w_in.shape[0]
    meta = jnp.broadcast_to(meta_tokens.astype(x.dtype)[None], (bsz, N_META, d))
    h = jnp.concatenate([meta, x], axis=1).reshape(bsz * length, d)
    dft = _dft_matrices(length)
    kw = dict(bsz=bsz, length=length)
    w_in, w_na_out, w_sc_out, w_hy_out, w_o, ffn_w1, ffn_w3, ffn_w2, moe_w2 = map(
        to_bf16, (w_in, w_na_out, w_sc_out, w_hy_out, w_o, ffn_w1, ffn_w3, ffn_w2, moe_w2))
    for layer in range(depth):
        proj = norm_matmul(h, norm_mix[layer], w_in[layer], tn=2048)
        y_na = neighbourhood_attention(proj, na_rpb[layer], na_meta_bias[layer], **kw)
        y_sc = short_conv_mixer(proj, sc_conv[layer], **kw)
        y_hy = hyena_mixer(proj, hy_conv[layer], hy_w1[layer], hy_b1[layer], hy_w2[layer], hy_b2[layer],
                           hy_w3[layer], hy_freq[layer], hy_decay[layer], hy_skip[layer], dft, **kw)
        merged = merge_branches(y_na, y_sc, y_hy, proj, w_na_out[layer], w_sc_out[layer], w_hy_out[layer])
        h = residual_matmul(h, merged, w_o[layer], tn=1024)
        j = layer // 2
        if layer % 2 == 0:
            up = norm_swiglu_up(h, norm_ffn[layer], ffn_w1[j], ffn_w3[j], tn=512)
            h = residual_matmul(h, up, ffn_w2[j], tn=512)
        else:
            g_final = norm_final if layer == depth - 1 else None
            h = moe_swiglu(h, norm_ffn[layer], router_w[j], moe_w1[j], moe_w3[j], moe_w2[j], g_final, **kw)
    if depth % 2 == 0:
        return h
    out = rms_norm_rows(h, norm_final, out_dtype=x.dtype)
    return out.reshape(bsz, length, d)[:, N_META:]
```

```python
import functools
import math

import numpy as np
import jax
import jax.numpy as jnp
from jax import lax
from jax.experimental import pallas as pl
from jax.experimental.pallas import tpu as pltpu

D_MODEL = 2048
N_META = 16
GRID_W = 64
GRID_ROWS = 32
NA_HEADS = 16
NA_HEAD_DIM = 64
NA_WIDTH = NA_HEADS * NA_HEAD_DIM
WIN_ROWS = 8
WIN_COLS = 16
SC_WIDTH = 512
HY_WIDTH = 512
HY_ORDER = 2
HY_EMB = 33
HY_HIDDEN = 64
HY_MOD_SHIFT = 0.05
PROJ_WIDTH = 3 * NA_WIDTH + 3 * SC_WIDTH + 3 * HY_WIDTH + 3 * D_MODEL
D_FF = 5632
N_EXPERTS = 8
TOP_K = 2
D_FF_EXPERT = 7168
MOE_BLOCK = 512
NORM_EPS = 1e-6
NEG_INF = -1e30

COL_Q, COL_K, COL_V = 0, NA_WIDTH, 2 * NA_WIDTH
COL_SCB = 3 * NA_WIDTH
COL_SCC = COL_SCB + SC_WIDTH
COL_SCU = COL_SCC + SC_WIDTH
COL_HYU = COL_SCU + SC_WIDTH
COL_GNA = COL_HYU + 3 * HY_WIDTH
COL_GSC = COL_GNA + D_MODEL
COL_GHY = COL_GSC + D_MODEL

V7X_VMEM_BYTES = 64 * 1024 * 1024
VMEM_LIMIT = V7X_VMEM_BYTES - 8 * 1024 * 1024

ROW_TILE = 768
NORM_CHUNK = 16
FFT_N = 2 * (N_META + GRID_ROWS * GRID_W)
FFT_KP = 2080
FFT_KB = 1040
INV_NB = 688

F32 = jnp.float32
BF16 = jnp.bfloat16


def _params(sem, vmem_limit=VMEM_LIMIT):
    return pltpu.CompilerParams(dimension_semantics=sem, vmem_limit_bytes=vmem_limit)


def _sigmoid(a):
    return 0.5 * jnp.tanh(0.5 * a) + 0.5


CAST_BLOCK_BYTES = 8 * 1024 * 1024


def _cast_kernel(x_ref, o_ref):
    o_ref[...] = x_ref[...].astype(o_ref.dtype)


def to_bf16(w):
    cols = w.shape[-1]
    w2 = w.reshape(-1, cols)
    rows = w2.shape[0]
    tr = min(rows, CAST_BLOCK_BYTES // (4 * cols)) // 16 * 16
    while rows % tr:
        tr -= 16
    out = pl.pallas_call(
        _cast_kernel,
        out_shape=jax.ShapeDtypeStruct((rows, cols), BF16),
        grid=(rows // tr,),
        in_specs=[pl.BlockSpec((tr, cols), lambda i: (i, 0))],
        out_specs=pl.BlockSpec((tr, cols), lambda i: (i, 0)),
        compiler_params=_params(("parallel",)),
        name="to_bf16",
    )(w2)
    return out.reshape(w.shape)


def _rms_norm_rows(x_ref, g_ref, xn_ref):
    rows = x_ref.shape[0]
    g = g_ref[...]

    def chunk(c, carry):
        r0 = pl.multiple_of(c * NORM_CHUNK, NORM_CHUNK)
        x = x_ref[pl.ds(r0, NORM_CHUNK), :]
        ms = jnp.mean(x * x, axis=-1, keepdims=True)
        xn_ref[pl.ds(r0, NORM_CHUNK), :] = (x * lax.rsqrt(ms + NORM_EPS) * g).astype(xn_ref.dtype)
        return carry

    lax.fori_loop(0, rows // NORM_CHUNK, chunk, 0, unroll=4)


def _norm_mm_kernel(x_ref, g_ref, w_ref, o_ref, xn_ref):
    @pl.when(pl.program_id(1) == 0)
    def _():
        _rms_norm_rows(x_ref, g_ref, xn_ref)

    o_ref[...] = jnp.dot(xn_ref[...], w_ref[...], preferred_element_type=F32).astype(o_ref.dtype)


def norm_matmul(x, g, w, *, tn, out_dtype=BF16, tm=ROW_TILE):
    t, k = x.shape
    n = w.shape[1]
    return pl.pallas_call(
        _norm_mm_kernel,
        out_shape=jax.ShapeDtypeStruct((t, n), out_dtype),
        grid=(t // tm, n // tn),
        in_specs=[
            pl.BlockSpec((tm, k), lambda i, j: (i, 0)),
            pl.BlockSpec((1, k), lambda i, j: (0, 0)),
            pl.BlockSpec((k, tn), lambda i, j: (0, j)),
        ],
        out_specs=pl.BlockSpec((tm, tn), lambda i, j: (i, j)),
        scratch_shapes=[pltpu.VMEM((tm, k), BF16)],
        compiler_params=_params(("parallel", "arbitrary")),
        name="norm_matmul",
    )(x, g.reshape(1, k), w)


def _norm_swiglu_kernel(x_ref, g_ref, w1_ref, w3_ref, o_ref, xn_ref):
    @pl.when(pl.program_id(1) == 0)
    def _():
        _rms_norm_rows(x_ref, g_ref, xn_ref)

    xn = xn_ref[...]
    a = jnp.dot(xn, w1_ref[...], preferred_element_type=F32)
    b = jnp.dot(xn, w3_ref[...], preferred_element_type=F32)
    o_ref[...] = (a * _sigmoid(a) * b).astype(o_ref.dtype)


def norm_swiglu_up(x, g, w1, w3, *, tn, tm=ROW_TILE):
    t, k = x.shape
    n = w1.shape[1]
    return pl.pallas_call(
        _norm_swiglu_kernel,
        out_shape=jax.ShapeDtypeStruct((t, n), BF16),
        grid=(t // tm, n // tn),
        in_specs=[
            pl.BlockSpec((tm, k), lambda i, j: (i, 0)),
            pl.BlockSpec((1, k), lambda i, j: (0, 0)),
            pl.BlockSpec((k, tn), lambda i, j: (0, j)),
            pl.BlockSpec((k, tn), lambda i, j: (0, j)),
        ],
        out_specs=pl.BlockSpec((tm, tn), lambda i, j: (i, j)),
        scratch_shapes=[pltpu.VMEM((tm, k), BF16)],
        compiler_params=_params(("parallel", "arbitrary")),
        name="norm_swiglu_up",
    )(x, g.reshape(1, k), w1, w3)


def _norm_router_kernel(x_ref, g_ref, w_ref, o_ref, xb_ref, xn_ref):
    _rms_norm_rows(x_ref, g_ref, xn_ref)
    xn = xn_ref[...]
    xb_ref[...] = xn.astype(xb_ref.dtype)
    o_ref[...] = jnp.dot(xn, w_ref[...], preferred_element_type=F32, precision=lax.Precision.HIGHEST)


def norm_router(x, g, w, *, tm=ROW_TILE):
    t, k = x.shape
    n = w.shape[1]
    return pl.pallas_call(
        _norm_router_kernel,
        out_shape=(jax.ShapeDtypeStruct((t, n), F32), jax.ShapeDtypeStruct((t, k), BF16)),
        grid=(t // tm,),
        in_specs=[
            pl.BlockSpec((tm, k), lambda i: (i, 0)),
            pl.BlockSpec((1, k), lambda i: (0, 0)),
            pl.BlockSpec((k, n), lambda i: (0, 0)),
        ],
        out_specs=(pl.BlockSpec((tm, n), lambda i: (i, 0)), pl.BlockSpec((tm, k), lambda i: (i, 0))),
        scratch_shapes=[pltpu.VMEM((tm, k), F32)],
        compiler_params=_params(("parallel",)),
        name="norm_router",
    )(x, g.reshape(1, k), w)


def _norm_kernel(x_ref, g_ref, o_ref):
    _rms_norm_rows(x_ref, g_ref, o_ref)


def rms_norm_rows(x, g, *, out_dtype, tm=ROW_TILE):
    t, k = x.shape
    return pl.pallas_call(
        _norm_kernel,
        out_shape=jax.ShapeDtypeStruct((t, k), out_dtype),
        grid=(t // tm,),
        in_specs=[pl.BlockSpec((tm, k), lambda i: (i, 0)), pl.BlockSpec((1, k), lambda i: (0, 0))],
        out_specs=pl.BlockSpec((tm, k), lambda i: (i, 0)),
        compiler_params=_params(("parallel",)),
        name="rms_norm",
    )(x, g.reshape(1, k))


def _dense_ffn_kernel(x_ref, g_ref, w1_ref, w3_ref, w2_ref, o_ref, xn_ref):
    @pl.when(pl.program_id(1) == 0)
    def _():
        _rms_norm_rows(x_ref, g_ref, xn_ref)
        o_ref[...] = x_ref[...]

    xn = xn_ref[...]
    a = jnp.dot(xn, w1_ref[...], preferred_element_type=F32)
    b = jnp.dot(xn, w3_ref[...], preferred_element_type=F32)
    hid = (a * _sigmoid(a) * b).astype(BF16)
    o_ref[...] += jnp.dot(hid, w2_ref[...], preferred_element_type=F32)


def dense_ffn(x, g, w1, w3, w2, *, fc=512, tm=ROW_TILE):
    t, d = x.shape
    f = w1.shape[1]
    return pl.pallas_call(
        _dense_ffn_kernel,
        out_shape=jax.ShapeDtypeStruct((t, d), F32),
        grid=(t // tm, f // fc),
        in_specs=[
            pl.BlockSpec((tm, d), lambda i, j: (i, 0)),
            pl.BlockSpec((1, d), lambda i, j: (0, 0)),
            pl.BlockSpec((d, fc), lambda i, j: (0, j)),
            pl.BlockSpec((d, fc), lambda i, j: (0, j)),
            pl.BlockSpec((fc, d), lambda i, j: (j, 0)),
        ],
        out_specs=pl.BlockSpec((tm, d), lambda i, j: (i, 0)),
        scratch_shapes=[pltpu.VMEM((tm, d), BF16)],
        compiler_params=_params(("parallel", "arbitrary")),
        name="dense_ffn",
    )(x, g.reshape(1, d), w1, w3, w2)


def _mixer_out_kernel(res_ref, yna_ref, ysc_ref, yhy_ref, gna_ref, gsc_ref, ghy_ref, wna_ref, wsc_ref, why_ref,
                      wo_ref, o_ref):
    @pl.when(pl.program_id(1) == 0)
    def _():
        o_ref[...] = res_ref[...]

    m = _sigmoid(gna_ref[...].astype(F32)) * jnp.dot(yna_ref[...], wna_ref[...], preferred_element_type=F32)
    m += _sigmoid(gsc_ref[...].astype(F32)) * jnp.dot(ysc_ref[...], wsc_ref[...], preferred_element_type=F32)
    m += _sigmoid(ghy_ref[...].astype(F32)) * jnp.dot(yhy_ref[...], why_ref[...], preferred_element_type=F32)
    o_ref[...] += jnp.dot(m.astype(BF16), wo_ref[...], preferred_element_type=F32)


def mixer_out(res, y_na, y_sc, y_hy, proj, w_na, w_sc, w_hy, w_o, *, cw=512, tm=ROW_TILE):
    t, d = res.shape
    return pl.pallas_call(
        _mixer_out_kernel,
        out_shape=jax.ShapeDtypeStruct((t, d), F32),
        grid=(t // tm, d // cw),
        in_specs=[
            pl.BlockSpec((tm, d), lambda i, c: (i, 0)),
            pl.BlockSpec((tm, NA_WIDTH), lambda i, c: (i, 0)),
            pl.BlockSpec((tm, SC_WIDTH), lambda i, c: (i, 0)),
            pl.BlockSpec((tm, HY_WIDTH), lambda i, c: (i, 0)),
            pl.BlockSpec((tm, cw), lambda i, c: (i, COL_GNA // cw + c)),
            pl.BlockSpec((tm, cw), lambda i, c: (i, COL_GSC // cw + c)),
            pl.BlockSpec((tm, cw), lambda i, c: (i, COL_GHY // cw + c)),
            pl.BlockSpec((NA_WIDTH, cw), lambda i, c: (0, c)),
            pl.BlockSpec((SC_WIDTH, cw), lambda i, c: (0, c)),
            pl.BlockSpec((HY_WIDTH, cw), lambda i, c: (0, c)),
            pl.BlockSpec((cw, d), lambda i, c: (c, 0)),
        ],
        out_specs=pl.BlockSpec((tm, d), lambda i, c: (i, 0)),
        compiler_params=_params(("parallel", "arbitrary")),
        name="mixer_out",
    )(res, y_na, y_sc, y_hy, proj, proj, proj, w_na, w_sc, w_hy, w_o)


def _res_mm_kernel(r_ref, x_ref, w_ref, o_ref):
    o_ref[...] = r_ref[...] + jnp.dot(x_ref[...], w_ref[...], preferred_element_type=F32)


def residual_matmul(res, x, w, *, tn, tm=ROW_TILE):
    t, k = x.shape
    n = w.shape[1]
    return pl.pallas_call(
        _res_mm_kernel,
        out_shape=jax.ShapeDtypeStruct((t, n), F32),
        grid=(t // tm, n // tn),
        in_specs=[
            pl.BlockSpec((tm, tn), lambda i, j: (i, j)),
            pl.BlockSpec((tm, k), lambda i, j: (i, 0)),
            pl.BlockSpec((k, tn), lambda i, j: (0, j)),
        ],
        out_specs=pl.BlockSpec((tm, tn), lambda i, j: (i, j)),
        compiler_params=_params(("parallel", "parallel")),
        name="residual_matmul",
    )(res, x, w)


def _merge_kernel(yna_ref, ysc_ref, yhy_ref, gna_ref, gsc_ref, ghy_ref, wna_ref, wsc_ref, why_ref, o_ref):
    acc = _sigmoid(gna_ref[...].astype(F32)) * jnp.dot(yna_ref[...], wna_ref[...], preferred_element_type=F32)
    acc += _sigmoid(gsc_ref[...].astype(F32)) * jnp.dot(ysc_ref[...], wsc_ref[...], preferred_element_type=F32)
    acc += _sigmoid(ghy_ref[...].astype(F32)) * jnp.dot(yhy_ref[...], why_ref[...], preferred_element_type=F32)
    o_ref[...] = acc.astype(o_ref.dtype)


def merge_branches(y_na, y_sc, y_hy, proj, w_na, w_sc, w_hy, *, tn=1024, tm=ROW_TILE):
    t = y_na.shape[0]
    n = D_MODEL
    return pl.pallas_call(
        _merge_kernel,
        out_shape=jax.ShapeDtypeStruct((t, n), BF16),
        grid=(t // tm, n // tn),
        in_specs=[
            pl.BlockSpec((tm, NA_WIDTH), lambda i, j: (i, 0)),
            pl.BlockSpec((tm, SC_WIDTH), lambda i, j: (i, 0)),
            pl.BlockSpec((tm, HY_WIDTH), lambda i, j: (i, 0)),
            pl.BlockSpec((tm, tn), lambda i, j: (i, COL_GNA // tn + j)),
            pl.BlockSpec((tm, tn), lambda i, j: (i, COL_GSC // tn + j)),
            pl.BlockSpec((tm, tn), lambda i, j: (i, COL_GHY // tn + j)),
            pl.BlockSpec((NA_WIDTH, tn), lambda i, j: (0, j)),
            pl.BlockSpec((SC_WIDTH, tn), lambda i, j: (0, j)),
            pl.BlockSpec((HY_WIDTH, tn), lambda i, j: (0, j)),
        ],
        out_specs=pl.BlockSpec((tm, tn), lambda i, j: (i, j)),
        compiler_params=_params(("parallel", "parallel")),
        name="merge_branches",
    )(y_na, y_sc, y_hy, proj, proj, proj, w_na, w_sc, w_hy)


def _na_bias_table(rpb):
    qc = np.arange(GRID_W)[:, None]
    kc = np.arange(GRID_W)[None, :]
    win_start = np.clip(qc - WIN_COLS // 2, 0, GRID_W - WIN_COLS)
    inside = (kc >= win_start) & (kc < win_start + WIN_COLS)
    dc = np.clip(kc - qc, -(WIN_COLS - 1), WIN_COLS - 1) + WIN_COLS - 1
    dr = np.arange(WIN_ROWS)[None, :] - np.arange(WIN_ROWS)[:, None] + WIN_ROWS - 1
    onehot = np.zeros((2 * WIN_COLS - 1, GRID_W * GRID_W), np.float32)
    onehot[dc.reshape(-1), np.arange(GRID_W * GRID_W)] = 1.0
    tbl = jnp.einsum('htid,dj->htij', rpb.astype(F32)[:, dr], jnp.asarray(onehot),
                     precision=lax.Precision.HIGHEST)
    tbl = tbl.reshape(NA_HEADS, WIN_ROWS, WIN_ROWS, GRID_W, GRID_W)
    tbl = jnp.where(jnp.asarray(inside)[None, None, None], tbl, NEG_INF)
    tbl = tbl.transpose(0, 1, 3, 2, 4)
    return tbl.reshape(NA_HEADS, WIN_ROWS, GRID_W, WIN_ROWS * GRID_W)


NA_ROW_GROUP = 4


def _na_kernel(q_ref, k_ref, v_ref, tbl_ref, mb_ref, o_ref):
    scale = NA_HEAD_DIM ** -0.5
    lane = lax.broadcasted_iota(jnp.int32, (1, 2 * NA_HEAD_DIM), 1)
    first = lane < NA_HEAD_DIM
    nt = (((1,), (1,)), ((), ()))
    k_meta = k_ref[0:N_META, :]
    v_meta = v_ref[0:N_META, :]
    mb_grid = mb_ref[0, 0]
    band = WIN_ROWS * GRID_W

    def stack_heads(q):
        q = q * scale
        zero = jnp.zeros_like(q)
        return jnp.concatenate([jnp.where(first, q, zero), jnp.where(first, zero, q)], axis=0)

    def unstack_heads(out, den, n):
        out = out / den
        return jnp.where(first, out[:n], out[n:])

    def softmax_parts(s_grid, s_meta):
        m = jnp.max(s_meta, axis=-1, keepdims=True)
        if s_grid is not None:
            m = jnp.maximum(m, jnp.max(s_grid, axis=-1, keepdims=True))
        p_meta = jnp.exp(s_meta - m)
        den = jnp.sum(p_meta, axis=-1, keepdims=True)
        p_grid = None
        if s_grid is not None:
            p_grid = jnp.exp(s_grid - m)
            den = den + jnp.sum(p_grid, axis=-1, keepdims=True)
            p_grid = p_grid.astype(BF16)
        return p_grid, p_meta.astype(BF16), den

    def attend(rows, with_meta_queries):
        geo = []
        for r in rows:
            rs = min(max(r - WIN_ROWS // 2, 0), GRID_ROWS - WIN_ROWS)
            geo.append((N_META + r * GRID_W, N_META + rs * GRID_W, r - rs))
        scores = []
        if with_meta_queries:
            q2 = stack_heads(q_ref[0:N_META, :])
            s_meta = lax.dot_general(q2, k_meta, nt, preferred_element_type=F32) + mb_ref[0, 1, 0:2 * N_META, :]
            scores.append((None, s_meta))
        for q0, k0, typ in geo:
            q2 = stack_heads(q_ref[q0:q0 + GRID_W, :])
            s_grid = lax.dot_general(q2, k_ref[k0:k0 + band, :], nt, preferred_element_type=F32) + tbl_ref[0, typ]
            s_meta = lax.dot_general(q2, k_meta, nt, preferred_element_type=F32) + mb_grid
            scores.append((s_grid, s_meta))
        probs = [softmax_parts(*sc) for sc in scores]
        if with_meta_queries:
            _, p_meta, den = probs.pop(0)
            out = jnp.dot(p_meta, v_meta, preferred_element_type=F32)
            o_ref[0:N_META, :] = unstack_heads(out, den, N_META).astype(o_ref.dtype)
        for (q0, k0, typ), (p_grid, p_meta, den) in zip(geo, probs):
            out = jnp.dot(p_grid, v_ref[k0:k0 + band, :], preferred_element_type=F32)
            out += jnp.dot(p_meta, v_meta, preferred_element_type=F32)
            o_ref[q0:q0 + GRID_W, :] = unstack_heads(out, den, GRID_W).astype(o_ref.dtype)

    for g0 in range(0, GRID_ROWS, NA_ROW_GROUP):
        attend(range(g0, g0 + NA_ROW_GROUP), with_meta_queries=(g0 == 0))


def neighbourhood_attention(proj, rpb, meta_bias, *, bsz, length):
    hp = 2 * NA_HEAD_DIM
    n_pairs = NA_HEADS // 2
    band = WIN_ROWS * GRID_W
    tbl = _na_bias_table(rpb).reshape(n_pairs, 2, WIN_ROWS, GRID_W, band).transpose(0, 2, 1, 3, 4)
    tbl = tbl.reshape(n_pairs, WIN_ROWS, 2 * GRID_W, band)
    mb = meta_bias.astype(F32).reshape(n_pairs, 2, 1, N_META)
    mb_grid = jnp.broadcast_to(mb, (n_pairs, 2, GRID_W, N_META)).reshape(n_pairs, 2 * GRID_W, N_META)
    mb_meta = jnp.broadcast_to(mb, (n_pairs, 2, N_META, N_META)).reshape(n_pairs, 2 * N_META, N_META)
    mb_meta = jnp.pad(mb_meta, ((0, 0), (0, 2 * GRID_W - 2 * N_META), (0, 0)))
    mb2 = jnp.stack([mb_grid, mb_meta], axis=1)
    return pl.pallas_call(
        _na_kernel,
        out_shape=jax.ShapeDtypeStruct((bsz * length, NA_WIDTH), BF16),
        grid=(n_pairs, bsz),
        in_specs=[
            pl.BlockSpec((length, hp), lambda h, b: (b, COL_Q // hp + h)),
            pl.BlockSpec((length, hp), lambda h, b: (b, COL_K // hp + h)),
            pl.BlockSpec((length, hp), lambda h, b: (b, COL_V // hp + h)),
            pl.BlockSpec((1, WIN_ROWS, 2 * GRID_W, band), lambda h, b: (h, 0, 0, 0)),
            pl.BlockSpec((1, 2, 2 * GRID_W, N_META), lambda h, b: (h, 0, 0, 0)),
        ],
        out_specs=pl.BlockSpec((length, hp), lambda h, b: (b, h)),
        compiler_params=_params(("parallel", "parallel")),
        name="neighbourhood_attention",
    )(proj, proj, proj, tbl, mb2)


def _dwconv3(u, w):
    length = u.shape[0]
    row = lax.broadcasted_iota(jnp.int32, u.shape, 0)
    prev = jnp.where(row == 0, 0.0, pltpu.roll(u, 1, 0))
    nxt = jnp.where(row == length - 1, 0.0, pltpu.roll(u, length - 1, 0))
    return prev * w[0:1] + u * w[1:2] + nxt * w[2:3]


def _gated_conv_kernel(b_ref, c_ref, u_ref, w_ref, o_ref):
    u = c_ref[...].astype(F32) * u_ref[...].astype(F32)
    o_ref[...] = (b_ref[...].astype(F32) * _dwconv3(u, w_ref[...])).astype(o_ref.dtype)


def _plain_conv_kernel(u_ref, w_ref, o_ref):
    o_ref[...] = _dwconv3(u_ref[...].astype(F32), w_ref[...]).astype(o_ref.dtype)


CONV_CB = 256


def short_conv_mixer(proj, w_conv, *, bsz, length):
    cb = CONV_CB
    return pl.pallas_call(
        _gated_conv_kernel,
        out_shape=jax.ShapeDtypeStruct((bsz * length, SC_WIDTH), BF16),
        grid=(bsz, SC_WIDTH // cb),
        in_specs=[
            pl.BlockSpec((length, cb), lambda b, c: (b, COL_SCB // cb + c)),
            pl.BlockSpec((length, cb), lambda b, c: (b, COL_SCC // cb + c)),
            pl.BlockSpec((length, cb), lambda b, c: (b, COL_SCU // cb + c)),
            pl.BlockSpec((3, cb), lambda b, c: (0, c)),
        ],
        out_specs=pl.BlockSpec((length, cb), lambda b, c: (b, c)),
        compiler_params=_params(("parallel", "parallel")),
        name="short_conv_mixer",
    )(proj, proj, proj, w_conv.astype(F32))


def hyena_in_conv(proj, w_conv, *, bsz, length):
    cb = CONV_CB
    return pl.pallas_call(
        _plain_conv_kernel,
        out_shape=jax.ShapeDtypeStruct((bsz * length, 3 * HY_WIDTH), BF16),
        grid=(bsz, 3 * HY_WIDTH // cb),
        in_specs=[
            pl.BlockSpec((length, cb), lambda b, c: (b, COL_HYU // cb + c)),
            pl.BlockSpec((3, cb), lambda b, c: (0, c)),
        ],
        out_specs=pl.BlockSpec((length, cb), lambda b, c: (b, c)),
        compiler_params=_params(("parallel", "parallel")),
        name="hyena_in_conv",
    )(proj, w_conv.astype(F32))


def _filter_kernel(z_ref, w1_ref, b1_ref, w2_ref, b2_ref, w3_ref, fr_ref, dec_ref, f_ref, s_ref):
    hi = lax.Precision.HIGHEST
    z = z_ref[...]
    fr = fr_ref[...]
    h = jnp.sin(fr * (jnp.dot(z, w1_ref[...], preferred_element_type=F32, precision=hi) + b1_ref[...]))
    h = jnp.sin(fr * (jnp.dot(h, w2_ref[...], preferred_element_type=F32, precision=hi) + b2_ref[...]))
    h = jnp.dot(h, w3_ref[...], preferred_element_type=F32, precision=hi)
    t = z[:, 0:1]
    f = h * (jnp.exp(-t * jnp.abs(dec_ref[0])) + HY_MOD_SHIFT)
    row = lax.broadcasted_iota(jnp.int32, f.shape, 0)
    backward = pl.program_id(0) % 2 == 1
    f = jnp.where(jnp.logical_and(row == 0, backward), 0.0, f)
    f_ref[0] = f.astype(f_ref.dtype)
    s_ref[0] = jnp.sum(jnp.abs(f), axis=0, keepdims=True)


def hyena_filters(length, w1, b1, w2, b2, w3, freq, decay):
    bands = (HY_EMB - 1) // 2
    t = jnp.linspace(0.0, 1.0, length, dtype=F32)[:, None]
    ang = (2.0 * math.pi / length) * jnp.arange(length, dtype=F32)[:, None]
    fb = jnp.linspace(1e-4, bands - 1, bands, dtype=F32)[None, :]
    z = jnp.concatenate([t, jnp.cos(fb * ang), -jnp.sin(fb * ang)], axis=-1)
    emb = 128
    z = jnp.pad(z, ((0, 0), (0, emb - HY_EMB)))
    w1p = jnp.pad(w1.astype(F32), ((0, emb - HY_EMB), (0, 0)))
    n_slab = HY_ORDER * 2
    row = lambda a: a.astype(F32).reshape(1, -1)
    return pl.pallas_call(
        _filter_kernel,
        out_shape=(jax.ShapeDtypeStruct((n_slab, length, HY_WIDTH), BF16),
                   jax.ShapeDtypeStruct((n_slab, 1, HY_WIDTH), F32)),
        grid=(n_slab,),
        in_specs=[
            pl.BlockSpec((length, emb), lambda j: (0, 0)),
            pl.BlockSpec((emb, HY_HIDDEN), lambda j: (0, 0)),
            pl.BlockSpec((1, HY_HIDDEN), lambda j: (0, 0)),
            pl.BlockSpec((HY_HIDDEN, HY_HIDDEN), lambda j: (0, 0)),
            pl.BlockSpec((1, HY_HIDDEN), lambda j: (0, 0)),
            pl.BlockSpec((HY_HIDDEN, HY_WIDTH), lambda j: (0, j)),
            pl.BlockSpec((1, HY_HIDDEN), lambda j: (0, 0)),
            pl.BlockSpec((1, 1, HY_WIDTH), lambda j: (j, 0, 0)),
        ],
        out_specs=(pl.BlockSpec((1, length, HY_WIDTH), lambda j: (j, 0, 0)),
                   pl.BlockSpec((1, 1, HY_WIDTH), lambda j: (j, 0, 0))),
        compiler_params=_params(("parallel",)),
        name="hyena_filters",
    )(z, w1p, row(b1), w2.astype(F32), row(b2), w3.astype(F32), row(freq),
      decay.astype(F32).reshape(n_slab, 1, HY_WIDTH))


def _dft_matrices(length):
    k = jnp.arange(FFT_KP, dtype=jnp.int32)[:, None]
    n = jnp.arange(length, dtype=jnp.int32)[None, :]
    theta = (2.0 * math.pi / FFT_N) * ((k * n) % FFT_N).astype(F32)
    valid = k <= FFT_N // 2
    fc = jnp.where(valid, jnp.cos(theta), 0.0)
    fs = jnp.where(valid, -jnp.sin(theta), 0.0)
    return fc.astype(BF16), fs.astype(BF16), fc.T.astype(BF16), fs.T.astype(BF16)


def _dft_raw_kernel(fc_ref, fs_ref, u_ref, re_ref, im_ref):
    u = u_ref[0]
    re_ref[0] = jnp.dot(fc_ref[...], u, preferred_element_type=F32)
    im_ref[0] = jnp.dot(fs_ref[...], u, preferred_element_type=F32)


def dft_filters(fc, fs, filt):
    n_slab, length, c = filt.shape
    out = jax.ShapeDtypeStruct((n_slab, FFT_KP, c), F32)
    return pl.pallas_call(
        _dft_raw_kernel,
        out_shape=(out, out),
        grid=(FFT_KP // FFT_KB, n_slab),
        in_specs=[
            pl.BlockSpec((FFT_KB, length), lambda kb, s: (kb, 0)),
            pl.BlockSpec((FFT_KB, length), lambda kb, s: (kb, 0)),
            pl.BlockSpec((1, length, c), lambda kb, s: (s, 0, 0)),
        ],
        out_specs=(pl.BlockSpec((1, FFT_KB, c), lambda kb, s: (s, kb, 0)),
                   pl.BlockSpec((1, FFT_KB, c), lambda kb, s: (s, kb, 0))),
        compiler_params=_params(("parallel", "parallel")),
        name="dft_filters",
    )(fc, fs, filt)


def _dft_mul_kernel(fc_ref, fs_ref, u_ref, are_ref, aim_ref, s_ref, yre_ref, yim_ref, g_ref):
    @pl.when(pl.program_id(1) == 0)
    def _():
        k = pl.program_id(0) * FFT_KB + lax.broadcasted_iota(jnp.int32, (FFT_KB, 1), 0)
        wk = jnp.where(jnp.logical_or(k == 0, k == FFT_N // 2), 1.0, 2.0)
        wk = jnp.where(k > FFT_N // 2, 0.0, wk) / FFT_N
        inv = wk / (s_ref[0] + s_ref[1])
        g_ref[0] = (are_ref[0] + are_ref[1]) * inv
        g_ref[1] = (aim_ref[0] - aim_ref[1]) * inv

    u = u_ref[...]
    re = jnp.dot(fc_ref[...], u, preferred_element_type=F32)
    im = jnp.dot(fs_ref[...], u, preferred_element_type=F32)
    gre = g_ref[0]
    gim = g_ref[1]
    yre_ref[0] = (re * gre - im * gim).astype(yre_ref.dtype)
    yim_ref[0] = (re * gim + im * gre).astype(yim_ref.dtype)


def dft_forward_mul(fc, fs, u, col, a_re, a_im, abs_sum, order, *, bsz, length):
    c = HY_WIDTH
    out = jax.ShapeDtypeStruct((bsz, FFT_KP, c), BF16)
    return pl.pallas_call(
        _dft_mul_kernel,
        out_shape=(out, out),
        grid=(FFT_KP // FFT_KB, bsz),
        in_specs=[
            pl.BlockSpec((FFT_KB, length), lambda kb, b: (kb, 0)),
            pl.BlockSpec((FFT_KB, length), lambda kb, b: (kb, 0)),
            pl.BlockSpec((length, c), lambda kb, b: (b, col)),
            pl.BlockSpec((2, FFT_KB, c), lambda kb, b: (order, kb, 0)),
            pl.BlockSpec((2, FFT_KB, c), lambda kb, b: (order, kb, 0)),
            pl.BlockSpec((2, 1, c), lambda kb, b: (order, 0, 0)),
        ],
        out_specs=(pl.BlockSpec((1, FFT_KB, c), lambda kb, b: (b, kb, 0)),
                   pl.BlockSpec((1, FFT_KB, c), lambda kb, b: (b, kb, 0))),
        scratch_shapes=[pltpu.VMEM((2, FFT_KB, c), F32)],
        compiler_params=_params(("parallel", "arbitrary")),
        name="dft_forward_mul",
    )(fc, fs, u, a_re, a_im, abs_sum)


def _dft_inv_kernel(ct_ref, st_ref, yre_ref, yim_ref, u_ref, gate_ref, skip_ref, o_ref):
    y = jnp.dot(ct_ref[...], yre_ref[0], preferred_element_type=F32)
    y += jnp.dot(st_ref[...], yim_ref[0], preferred_element_type=F32)
    y += u_ref[...].astype(F32) * skip_ref[0]
    o_ref[...] = (gate_ref[...].astype(F32) * y).astype(o_ref.dtype)


def dft_inverse_gate(fct, fst, yre, yim, u, u_col, gate, gate_col, skip, order, *, bsz, length):
    c = HY_WIDTH
    nblk = length // INV_NB
    return pl.pallas_call(
        _dft_inv_kernel,
        out_shape=jax.ShapeDtypeStruct((bsz * length, c), BF16),
        grid=(nblk, bsz),
        in_specs=[
            pl.BlockSpec((INV_NB, FFT_KP), lambda nb, b: (nb, 0)),
            pl.BlockSpec((INV_NB, FFT_KP), lambda nb, b: (nb, 0)),
            pl.BlockSpec((1, FFT_KP, c), lambda nb, b: (b, 0, 0)),
            pl.BlockSpec((1, FFT_KP, c), lambda nb, b: (b, 0, 0)),
            pl.BlockSpec((INV_NB, c), lambda nb, b: (b * nblk + nb, u_col)),
            pl.BlockSpec((INV_NB, c), lambda nb, b: (b * nblk + nb, gate_col)),
            pl.BlockSpec((1, 1, c), lambda nb, b: (order, 0, 0)),
        ],
        out_specs=pl.BlockSpec((INV_NB, c), lambda nb, b: (b * nblk + nb, 0)),
        compiler_params=_params(("parallel", "parallel")),
        name="dft_inverse_gate",
    )(fct, fst, yre, yim, u, gate, skip)


def hyena_mixer(proj, w_conv, w1, b1, w2, b2, w3, freq, decay, skip, dft, *, bsz, length):
    fc, fs, fct, fst = dft
    vxx = hyena_in_conv(proj, w_conv, bsz=bsz, length=length)
    filt, abs_sum = hyena_filters(length, w1, b1, w2, b2, w3, freq, decay)
    a_re, a_im = dft_filters(fc, fs, filt)
    skip = skip.astype(F32).reshape(HY_ORDER, 1, HY_WIDTH)
    kw = dict(bsz=bsz, length=length)
    yre, yim = dft_forward_mul(fc, fs, vxx, 0, a_re, a_im, abs_sum, 0, **kw)
    z = dft_inverse_gate(fct, fst, yre, yim, vxx, 0, vxx, 1, skip, 0, **kw)
    yre, yim = dft_forward_mul(fc, fs, z, 0, a_re, a_im, abs_sum, 1, **kw)
    return dft_inverse_gate(fct, fst, yre, yim, z, 0, vxx, 2, skip, 1, **kw)


MOE_SUPER = 2 * MOE_BLOCK
MOE_FC = 512
MOE_VMEM_LIMIT = V7X_VMEM_BYTES - 3 * 1024 * 1024


def _moe_kernel(be_ref, nh_ref, x_ref, w1_ref, w3_ref, w2_ref, o_ref, acc_ref):
    i = pl.program_id(0)
    j = pl.program_id(1)
    halves = nh_ref[i]

    @pl.when(j == 0)
    def _():
        acc_ref[...] = jnp.zeros_like(acc_ref)

    def ffn(rows):
        x = x_ref[0:rows, :]
        a = jnp.dot(x, w1_ref[0].astype(BF16), preferred_element_type=F32)
        b = jnp.dot(x, w3_ref[0].astype(BF16), preferred_element_type=F32)
        hid = (a * _sigmoid(a) * b).astype(BF16)
        acc_ref[0:rows, :] += jnp.dot(hid, w2_ref[0].astype(BF16), preferred_element_type=F32)

    @pl.when(halves == 2)
    def _():
        ffn(MOE_SUPER)

    @pl.when(halves == 1)
    def _():
        ffn(MOE_BLOCK)

    @pl.when(j == pl.num_programs(1) - 1)
    def _():
        o_ref[...] = acc_ref[...].astype(o_ref.dtype)


def moe_experts(xb, block_expert, n_halves, w1, w3, w2):
    n_slots, d = xb.shape
    n_super = n_slots // MOE_SUPER
    fc = MOE_FC
    n_fc = D_FF_EXPERT // fc

    def chunk(i, j, nh):
        return jnp.where(nh[i] > 0, j, n_fc - 1)

    grid_spec = pltpu.PrefetchScalarGridSpec(
        num_scalar_prefetch=2,
        grid=(n_super, n_fc),
        in_specs=[
            pl.BlockSpec((MOE_SUPER, d), lambda i, j, be, nh: (i, 0)),
            pl.BlockSpec((1, d, fc), lambda i, j, be, nh: (be[i], 0, chunk(i, j, nh))),
            pl.BlockSpec((1, d, fc), lambda i, j, be, nh: (be[i], 0, chunk(i, j, nh))),
            pl.BlockSpec((1, fc, d), lambda i, j, be, nh: (be[i], chunk(i, j, nh), 0)),
        ],
        out_specs=pl.BlockSpec((MOE_SUPER, d), lambda i, j, be, nh: (i, 0)),
        scratch_shapes=[pltpu.VMEM((MOE_SUPER, d), F32)],
    )
    return pl.pallas_call(
        _moe_kernel,
        out_shape=jax.ShapeDtypeStruct((n_slots, d), BF16),
        grid_spec=grid_spec,
        compiler_params=_params(("parallel", "arbitrary"), MOE_VMEM_LIMIT),
        name="moe_experts",
    )(block_expert, n_halves, xb, w1, w3, w2)


COMBINE_ROWS = 256


def _moe_combine_kernel(h_ref, y0_ref, y1_ref, g0_ref, g1_ref, gn_ref, o_ref, *, final):
    rows = o_ref.shape[0]
    gn = gn_ref[...]

    def chunk(c, carry):
        r0 = pl.multiple_of(c * NORM_CHUNK, NORM_CHUNK)
        sl = pl.ds(r0, NORM_CHUNK)
        h = h_ref[0, sl, :] if final else h_ref[sl, :]
        y = h + g0_ref[sl, :] * y0_ref[sl, :].astype(F32) + g1_ref[sl, :] * y1_ref[sl, :].astype(F32)
        if final:
            ms = jnp.mean(y * y, axis=-1, keepdims=True)
            y = y * lax.rsqrt(ms + NORM_EPS) * gn
        o_ref[sl, :] = y.astype(o_ref.dtype)
        return carry

    lax.fori_loop(0, rows // NORM_CHUNK, chunk, 0, unroll=4)


def moe_combine(h, y0, y1, g0, g1, *, tm=ROW_TILE):
    t, d = h.shape
    row = lambda w: pl.BlockSpec((tm, w), lambda i: (i, 0))
    return pl.pallas_call(
        functools.partial(_moe_combine_kernel, final=False),
        out_shape=jax.ShapeDtypeStruct((t, d), F32),
        grid=(t // tm,),
        in_specs=[row(d), row(d), row(d), row(1), row(1), pl.BlockSpec((1, d), lambda i: (0, 0))],
        out_specs=row(d),
        compiler_params=_params(("parallel",)),
        name="moe_combine",
    )(h, y0, y1, g0, g1, jnp.ones((1, d), F32))


def moe_combine_final(h, y0, y1, g0, g1, g_final, *, bsz, length):
    d = h.shape[1]
    seq = length - N_META
    r = COMBINE_ROWS
    nr = seq // r
    row = lambda w: pl.BlockSpec((r, w), lambda b, i: (b * nr + i, 0))
    h_spec = pl.BlockSpec((pl.Element(1), pl.Element(r), pl.Element(d)),
                          lambda b, i: (b, pl.multiple_of(N_META + i * r, N_META), 0))
    return pl.pallas_call(
        functools.partial(_moe_combine_kernel, final=True),
        out_shape=jax.ShapeDtypeStruct((bsz, seq, d), F32),
        grid=(bsz, nr),
        in_specs=[h_spec, row(d), row(d), row(1), row(1), pl.BlockSpec((1, d), lambda b, i: (0, 0))],
        out_specs=pl.BlockSpec((None, r, d), lambda b, i: (b, i, 0)),
        compiler_params=_params(("parallel", "parallel")),
        name="moe_combine_final",
    )(h.reshape(bsz, length, d), y0, y1, g0, g1, g_final.astype(F32).reshape(1, d))


SCAN_CHUNK = 256


def _expert_ranks(expert):
    n = expert.shape[0]
    onehot = (expert[:, None] == jnp.arange(N_EXPERTS, dtype=jnp.int32)[None, :]).astype(F32)
    chunks = onehot.reshape(n // SCAN_CHUNK, SCAN_CHUNK, N_EXPERTS)
    tri = jnp.tril(jnp.ones((SCAN_CHUNK, SCAN_CHUNK), F32), k=-1)
    within = jnp.einsum('ij,cjk->cik', tri, chunks)
    totals = jnp.sum(chunks, axis=1)
    before = jnp.cumsum(totals, axis=0) - totals
    rank = jnp.sum((within + before[:, None, :]) * chunks, axis=-1).reshape(n)
    return rank.astype(jnp.int32), jnp.sum(totals, axis=0).astype(jnp.int32)


def moe_swiglu(h, g_norm, w_router, w1, w3, w2, g_final, *, bsz, length):
    n_tok, d = h.shape
    n_assign = n_tok * TOP_K
    router_pad = 128
    wr = jnp.pad(w_router.astype(F32), ((0, 0), (0, router_pad - N_EXPERTS)))
    logits, hn = norm_router(h, g_norm, wr)
    top_val, top_idx = lax.top_k(logits[:, :N_EXPERTS], TOP_K)
    gate = jax.nn.softmax(top_val, axis=-1)
    expert = top_idx.astype(jnp.int32).reshape(n_assign)
    rank, counts = _expert_ranks(expert)
    padded = (counts + MOE_SUPER - 1) // MOE_SUPER * MOE_SUPER
    pad_end = jnp.cumsum(padded)
    pad_start = pad_end - padded
    slot = (pad_start[expert] + rank).reshape(n_tok, TOP_K)
    n_super = -(-n_assign // MOE_SUPER) + N_EXPERTS
    n_slots = n_super * MOE_SUPER
    token = jnp.repeat(jnp.arange(n_tok, dtype=jnp.int32), TOP_K)
    slot_token = jnp.zeros((n_slots,), jnp.int32).at[slot.reshape(n_assign)].set(token)
    n_used = pad_end[-1] // MOE_SUPER
    sb = jnp.arange(n_super, dtype=jnp.int32)
    block_expert = jnp.minimum(jnp.searchsorted(pad_end, jnp.minimum(sb, n_used - 1) * MOE_SUPER, side='right'),
                               N_EXPERTS - 1).astype(jnp.int32)
    rows_here = jnp.clip(counts[block_expert] - (sb * MOE_SUPER - pad_start[block_expert]), 0, MOE_SUPER)
    n_halves = jnp.where(sb < n_used, (rows_here + MOE_BLOCK - 1) // MOE_BLOCK, 0).astype(jnp.int32)
    xb = hn[slot_token]
    yb = moe_experts(xb, block_expert, n_halves, w1, w3, w2)
    if g_final is None:
        return moe_combine(h, yb[slot[:, 0]], yb[slot[:, 1]], gate[:, 0:1], gate[:, 1:2])
    keep = lambda a: a.reshape(bsz, length, -1)[:, N_META:].reshape(bsz * (length - N_META), -1)
    slot, gate = keep(slot), keep(gate)
    return moe_combine_final(h, yb[slot[:, 0]], yb[slot[:, 1]], gate[:, 0:1], gate[:, 1:2], g_final,
                             bsz=bsz, length=length)


def kernel(x, meta_tokens, norm_mix, w_in, na_rpb, na_meta_bias, sc_conv, hy_conv, hy_w1, hy_b1, hy_w2, hy_b2,
           hy_w3, hy_freq, hy_decay, hy_skip, w_na_out, w_sc_out, w_hy_out, w_o, norm_ffn, ffn_w1, ffn_w3,
           ffn_w2, router_w, moe_w1, moe_w3, moe_w2, norm_final):
    bsz, seq, d = x.shape
    length = N_META + seq
    depth = w_in.shape[0]
    meta = jnp.broadcast_to(meta_tokens.astype(x.dtype)[None], (bsz, N_META, d))
    h = jnp.concatenate([meta, x], axis=1).reshape(bsz * length, d)
    dft = _dft_matrices(length)
    kw = dict(bsz=bsz, length=length)
    w_in, w_na_out, w_sc_out, w_hy_out, w_o, ffn_w1, ffn_w3, ffn_w2 = map(
        to_bf16, (w_in, w_na_out, w_sc_out, w_hy_out, w_o, ffn_w1, ffn_w3, ffn_w2))
    for layer in range(depth):
        proj = norm_matmul(h, norm_mix[layer], w_in[layer], tn=2048)
        y_na = neighbourhood_attention(proj, na_rpb[layer], na_meta_bias[layer], **kw)
        y_sc = short_conv_mixer(proj, sc_conv[layer], **kw)
        y_hy = hyena_mixer(proj, hy_conv[layer], hy_w1[layer], hy_b1[layer], hy_w2[layer], hy_b2[layer],
                           hy_w3[layer], hy_freq[layer], hy_decay[layer], hy_skip[layer], dft, **kw)
        h = mixer_out(h, y_na, y_sc, y_hy, proj, w_na_out[layer], w_sc_out[layer], w_hy_out[layer], w_o[layer])
        j = layer // 2
        if layer % 2 == 0:
            h = dense_ffn(h, norm_ffn[layer], ffn_w1[j], ffn_w3[j], ffn_w2[j])
        else:
            g_final = norm_final if layer == depth - 1 else None
            h = moe_swiglu(h, norm_ffn[layer], router_w[j], moe_w1[j], moe_w3[j], moe_w2[j], g_final, **kw)
    if depth % 2 == 0:
        return h
    out = rms_norm_rows(h, norm_final, out_dtype=x.dtype)
    return out.reshape(bsz, length, d)[:, N_META:]
```

```python
import functools
import math

import numpy as np
import jax
import jax.numpy as jnp
from jax import lax
from jax.experimental import pallas as pl
from jax.experimental.pallas import tpu as pltpu

D_MODEL = 2048
N_META = 16
GRID_W = 64
GRID_ROWS = 32
NA_HEADS = 16
NA_HEAD_DIM = 64
NA_WIDTH = NA_HEADS * NA_HEAD_DIM
WIN_ROWS = 8
WIN_COLS = 16
SC_WIDTH = 512
HY_WIDTH = 512
HY_ORDER = 2
HY_EMB = 33
HY_HIDDEN = 64
HY_MOD_SHIFT = 0.05
PROJ_WIDTH = 3 * NA_WIDTH + 3 * SC_WIDTH + 3 * HY_WIDTH + 3 * D_MODEL
D_FF = 5632
N_EXPERTS = 8
TOP_K = 2
D_FF_EXPERT = 7168
MOE_BLOCK = 512
NORM_EPS = 1e-6
NEG_INF = -1e30

COL_Q, COL_K, COL_V = 0, NA_WIDTH, 2 * NA_WIDTH
COL_SCB = 3 * NA_WIDTH
COL_SCC = COL_SCB + SC_WIDTH
COL_SCU = COL_SCC + SC_WIDTH
COL_HYU = COL_SCU + SC_WIDTH
COL_GNA = COL_HYU + 3 * HY_WIDTH
COL_GSC = COL_GNA + D_MODEL
COL_GHY = COL_GSC + D_MODEL

V7X_VMEM_BYTES = 64 * 1024 * 1024
VMEM_LIMIT = V7X_VMEM_BYTES - 8 * 1024 * 1024

ROW_TILE = 768
NORM_CHUNK = 16
FFT_N = 2 * (N_META + GRID_ROWS * GRID_W)
FFT_KP = 2080
FFT_KB = 1040
INV_NB = 688

F32 = jnp.float32
BF16 = jnp.bfloat16


def _params(sem, vmem_limit=VMEM_LIMIT):
    return pltpu.CompilerParams(dimension_semantics=sem, vmem_limit_bytes=vmem_limit)


def _sigmoid(a):
    return 0.5 * jnp.tanh(0.5 * a) + 0.5


CAST_BLOCK_BYTES = 8 * 1024 * 1024


def _cast_kernel(x_ref, o_ref):
    o_ref[...] = x_ref[...].astype(o_ref.dtype)


def to_bf16(w):
    cols = w.shape[-1]
    w2 = w.reshape(-1, cols)
    rows = w2.shape[0]
    tr = min(rows, CAST_BLOCK_BYTES // (4 * cols)) // 16 * 16
    while rows % tr:
        tr -= 16
    out = pl.pallas_call(
        _cast_kernel,
        out_shape=jax.ShapeDtypeStruct((rows, cols), BF16),
        grid=(rows // tr,),
        in_specs=[pl.BlockSpec((tr, cols), lambda i: (i, 0))],
        out_specs=pl.BlockSpec((tr, cols), lambda i: (i, 0)),
        compiler_params=_params(("parallel",)),
        name="to_bf16",
    )(w2)
    return out.reshape(w.shape)


def _rms_norm_rows(x_ref, g_ref, xn_ref):
    rows = x_ref.shape[0]
    g = g_ref[...]

    def chunk(c, carry):
        r0 = pl.multiple_of(c * NORM_CHUNK, NORM_CHUNK)
        x = x_ref[pl.ds(r0, NORM_CHUNK), :]
        ms = jnp.mean(x * x, axis=-1, keepdims=True)
        xn_ref[pl.ds(r0, NORM_CHUNK), :] = (x * lax.rsqrt(ms + NORM_EPS) * g).astype(xn_ref.dtype)
        return carry

    lax.fori_loop(0, rows // NORM_CHUNK, chunk, 0, unroll=4)


def _norm_mm_kernel(x_ref, g_ref, w_ref, o_ref, xn_ref):
    @pl.when(pl.program_id(1) == 0)
    def _():
        _rms_norm_rows(x_ref, g_ref, xn_ref)

    o_ref[...] = jnp.dot(xn_ref[...], w_ref[...], preferred_element_type=F32).astype(o_ref.dtype)


def norm_matmul(x, g, w, *, tn, out_dtype=BF16, tm=ROW_TILE):
    t, k = x.shape
    n = w.shape[1]
    return pl.pallas_call(
        _norm_mm_kernel,
        out_shape=jax.ShapeDtypeStruct((t, n), out_dtype),
        grid=(t // tm, n // tn),
        in_specs=[
            pl.BlockSpec((tm, k), lambda i, j: (i, 0)),
            pl.BlockSpec((1, k), lambda i, j: (0, 0)),
            pl.BlockSpec((k, tn), lambda i, j: (0, j)),
        ],
        out_specs=pl.BlockSpec((tm, tn), lambda i, j: (i, j)),
        scratch_shapes=[pltpu.VMEM((tm, k), BF16)],
        compiler_params=_params(("parallel", "arbitrary")),
        name="norm_matmul",
    )(x, g.reshape(1, k), w)


def _norm_router_kernel(x_ref, g_ref, w_ref, o_ref, xb_ref, xn_ref):
    _rms_norm_rows(x_ref, g_ref, xn_ref)
    xn = xn_ref[...]
    xb_ref[...] = xn.astype(xb_ref.dtype)
    o_ref[...] = jnp.dot(xn, w_ref[...], preferred_element_type=F32, precision=lax.Precision.HIGHEST)


def norm_router(x, g, w, *, tm=ROW_TILE):
    t, k = x.shape
    n = w.shape[1]
    return pl.pallas_call(
        _norm_router_kernel,
        out_shape=(jax.ShapeDtypeStruct((t, n), F32), jax.ShapeDtypeStruct((t, k), BF16)),
        grid=(t // tm,),
        in_specs=[
            pl.BlockSpec((tm, k), lambda i: (i, 0)),
            pl.BlockSpec((1, k), lambda i: (0, 0)),
            pl.BlockSpec((k, n), lambda i: (0, 0)),
        ],
        out_specs=(pl.BlockSpec((tm, n), lambda i: (i, 0)), pl.BlockSpec((tm, k), lambda i: (i, 0))),
        scratch_shapes=[pltpu.VMEM((tm, k), F32)],
        compiler_params=_params(("parallel",)),
        name="norm_router",
    )(x, g.reshape(1, k), w)


def _norm_kernel(x_ref, g_ref, o_ref):
    _rms_norm_rows(x_ref, g_ref, o_ref)


def rms_norm_rows(x, g, *, out_dtype, tm=ROW_TILE):
    t, k = x.shape
    return pl.pallas_call(
        _norm_kernel,
        out_shape=jax.ShapeDtypeStruct((t, k), out_dtype),
        grid=(t // tm,),
        in_specs=[pl.BlockSpec((tm, k), lambda i: (i, 0)), pl.BlockSpec((1, k), lambda i: (0, 0))],
        out_specs=pl.BlockSpec((tm, k), lambda i: (i, 0)),
        compiler_params=_params(("parallel",)),
        name="rms_norm",
    )(x, g.reshape(1, k))


def _dense_ffn_kernel(x_ref, g_ref, w1_ref, w3_ref, w2_ref, o_ref, xn_ref):
    @pl.when(pl.program_id(1) == 0)
    def _():
        _rms_norm_rows(x_ref, g_ref, xn_ref)
        o_ref[...] = x_ref[...]

    xn = xn_ref[...]
    a = jnp.dot(xn, w1_ref[...], preferred_element_type=F32)
    b = jnp.dot(xn, w3_ref[...], preferred_element_type=F32)
    hid = (a * _sigmoid(a) * b).astype(BF16)
    o_ref[...] += jnp.dot(hid, w2_ref[...], preferred_element_type=F32)


def dense_ffn(x, g, w1, w3, w2, *, fc=512, tm=ROW_TILE):
    t, d = x.shape
    f = w1.shape[1]
    return pl.pallas_call(
        _dense_ffn_kernel,
        out_shape=jax.ShapeDtypeStruct((t, d), F32),
        grid=(t // tm, f // fc),
        in_specs=[
            pl.BlockSpec((tm, d), lambda i, j: (i, 0)),
            pl.BlockSpec((1, d), lambda i, j: (0, 0)),
            pl.BlockSpec((d, fc), lambda i, j: (0, j)),
            pl.BlockSpec((d, fc), lambda i, j: (0, j)),
            pl.BlockSpec((fc, d), lambda i, j: (j, 0)),
        ],
        out_specs=pl.BlockSpec((tm, d), lambda i, j: (i, 0)),
        scratch_shapes=[pltpu.VMEM((tm, d), BF16)],
        compiler_params=_params(("parallel", "arbitrary")),
        name="dense_ffn",
    )(x, g.reshape(1, d), w1, w3, w2)


def _mixer_out_kernel(res_ref, yna_ref, ysc_ref, yhy_ref, gna_ref, gsc_ref, ghy_ref, wna_ref, wsc_ref, why_ref,
                      wo_ref, o_ref):
    @pl.when(pl.program_id(1) == 0)
    def _():
        o_ref[...] = res_ref[...]

    m = _sigmoid(gna_ref[...].astype(F32)) * jnp.dot(yna_ref[...], wna_ref[...], preferred_element_type=F32)
    m += _sigmoid(gsc_ref[...].astype(F32)) * jnp.dot(ysc_ref[...], wsc_ref[...], preferred_element_type=F32)
    m += _sigmoid(ghy_ref[...].astype(F32)) * jnp.dot(yhy_ref[...], why_ref[...], preferred_element_type=F32)
    o_ref[...] += jnp.dot(m.astype(BF16), wo_ref[...], preferred_element_type=F32)


def mixer_out(res, y_na, y_sc, y_hy, proj, w_na, w_sc, w_hy, w_o, *, cw=512, tm=ROW_TILE):
    t, d = res.shape
    return pl.pallas_call(
        _mixer_out_kernel,
        out_shape=jax.ShapeDtypeStruct((t, d), F32),
        grid=(t // tm, d // cw),
        in_specs=[
            pl.BlockSpec((tm, d), lambda i, c: (i, 0)),
            pl.BlockSpec((tm, NA_WIDTH), lambda i, c: (i, 0)),
            pl.BlockSpec((tm, SC_WIDTH), lambda i, c: (i, 0)),
            pl.BlockSpec((tm, HY_WIDTH), lambda i, c: (i, 0)),
            pl.BlockSpec((tm, cw), lambda i, c: (i, COL_GNA // cw + c)),
            pl.BlockSpec((tm, cw), lambda i, c: (i, COL_GSC // cw + c)),
            pl.BlockSpec((tm, cw), lambda i, c: (i, COL_GHY // cw + c)),
            pl.BlockSpec((NA_WIDTH, cw), lambda i, c: (0, c)),
            pl.BlockSpec((SC_WIDTH, cw), lambda i, c: (0, c)),
            pl.BlockSpec((HY_WIDTH, cw), lambda i, c: (0, c)),
            pl.BlockSpec((cw, d), lambda i, c: (c, 0)),
        ],
        out_specs=pl.BlockSpec((tm, d), lambda i, c: (i, 0)),
        compiler_params=_params(("parallel", "arbitrary")),
        name="mixer_out",
    )(res, y_na, y_sc, y_hy, proj, proj, proj, w_na, w_sc, w_hy, w_o)


def _na_bias_table(rpb):
    qc = np.arange(GRID_W)[:, None]
    kc = np.arange(GRID_W)[None, :]
    win_start = np.clip(qc - WIN_COLS // 2, 0, GRID_W - WIN_COLS)
    inside = (kc >= win_start) & (kc < win_start + WIN_COLS)
    dc = np.clip(kc - qc, -(WIN_COLS - 1), WIN_COLS - 1) + WIN_COLS - 1
    dr = np.arange(WIN_ROWS)[None, :] - np.arange(WIN_ROWS)[:, None] + WIN_ROWS - 1
    onehot = np.zeros((2 * WIN_COLS - 1, GRID_W * GRID_W), np.float32)
    onehot[dc.reshape(-1), np.arange(GRID_W * GRID_W)] = 1.0
    tbl = jnp.einsum('htid,dj->htij', rpb.astype(F32)[:, dr], jnp.asarray(onehot),
                     precision=lax.Precision.HIGHEST)
    tbl = tbl.reshape(NA_HEADS, WIN_ROWS, WIN_ROWS, GRID_W, GRID_W)
    tbl = jnp.where(jnp.asarray(inside)[None, None, None], tbl, NEG_INF)
    tbl = tbl.transpose(0, 1, 3, 2, 4)
    return tbl.reshape(NA_HEADS, WIN_ROWS, GRID_W, WIN_ROWS * GRID_W)


NA_ROW_GROUP = 4


def _na_kernel(q_ref, k_ref, v_ref, tbl_ref, mb_ref, o_ref):
    scale = NA_HEAD_DIM ** -0.5
    lane = lax.broadcasted_iota(jnp.int32, (1, 2 * NA_HEAD_DIM), 1)
    first = lane < NA_HEAD_DIM
    nt = (((1,), (1,)), ((), ()))
    k_meta = k_ref[0:N_META, :]
    v_meta = v_ref[0:N_META, :]
    mb_grid = mb_ref[0, 0]
    band = WIN_ROWS * GRID_W

    def stack_heads(q):
        q = q * scale
        zero = jnp.zeros_like(q)
        return jnp.concatenate([jnp.where(first, q, zero), jnp.where(first, zero, q)], axis=0)

    def unstack_heads(out, den, n):
        out = out / den
        return jnp.where(first, out[:n], out[n:])

    def softmax_parts(s_grid, s_meta):
        m = jnp.max(s_meta, axis=-1, keepdims=True)
        if s_grid is not None:
            m = jnp.maximum(m, jnp.max(s_grid, axis=-1, keepdims=True))
        p_meta = jnp.exp(s_meta - m)
        den = jnp.sum(p_meta, axis=-1, keepdims=True)
        p_grid = None
        if s_grid is not None:
            p_grid = jnp.exp(s_grid - m)
            den = den + jnp.sum(p_grid, axis=-1, keepdims=True)
            p_grid = p_grid.astype(BF16)
        return p_grid, p_meta.astype(BF16), den

    def attend(rows, with_meta_queries):
        geo = []
        for r in rows:
            rs = min(max(r - WIN_ROWS // 2, 0), GRID_ROWS - WIN_ROWS)
            geo.append((N_META + r * GRID_W, N_META + rs * GRID_W, r - rs))
        scores = []
        if with_meta_queries:
            q2 = stack_heads(q_ref[0:N_META, :])
            s_meta = lax.dot_general(q2, k_meta, nt, preferred_element_type=F32) + mb_ref[0, 1, 0:2 * N_META, :]
            scores.append((None, s_meta))
        for q0, k0, typ in geo:
            q2 = stack_heads(q_ref[q0:q0 + GRID_W, :])
            s_grid = lax.dot_general(q2, k_ref[k0:k0 + band, :], nt, preferred_element_type=F32) + tbl_ref[0, typ]
            s_meta = lax.dot_general(q2, k_meta, nt, preferred_element_type=F32) + mb_grid
            scores.append((s_grid, s_meta))
        probs = [softmax_parts(*sc) for sc in scores]
        if with_meta_queries:
            _, p_meta, den = probs.pop(0)
            out = jnp.dot(p_meta, v_meta, preferred_element_type=F32)
            o_ref[0:N_META, :] = unstack_heads(out, den, N_META).astype(o_ref.dtype)
        for (q0, k0, typ), (p_grid, p_meta, den) in zip(geo, probs):
            out = jnp.dot(p_grid, v_ref[k0:k0 + band, :], preferred_element_type=F32)
            out += jnp.dot(p_meta, v_meta, preferred_element_type=F32)
            o_ref[q0:q0 + GRID_W, :] = unstack_heads(out, den, GRID_W).astype(o_ref.dtype)

    for g0 in range(0, GRID_ROWS, NA_ROW_GROUP):
        attend(range(g0, g0 + NA_ROW_GROUP), with_meta_queries=(g0 == 0))


def neighbourhood_attention(proj, rpb, meta_bias, *, bsz, length):
    hp = 2 * NA_HEAD_DIM
    n_pairs = NA_HEADS // 2
    band = WIN_ROWS * GRID_W
    tbl = _na_bias_table(rpb).reshape(n_pairs, 2, WIN_ROWS, GRID_W, band).transpose(0, 2, 1, 3, 4)
    tbl = tbl.reshape(n_pairs, WIN_ROWS, 2 * GRID_W, band)
    mb = meta_bias.astype(F32).reshape(n_pairs, 2, 1, N_META)
    mb_grid = jnp.broadcast_to(mb, (n_pairs, 2, GRID_W, N_META)).reshape(n_pairs, 2 * GRID_W, N_META)
    mb_meta = jnp.broadcast_to(mb, (n_pairs, 2, N_META, N_META)).reshape(n_pairs, 2 * N_META, N_META)
    mb_meta = jnp.pad(mb_meta, ((0, 0), (0, 2 * GRID_W - 2 * N_META), (0, 0)))
    mb2 = jnp.stack([mb_grid, mb_meta], axis=1)
    return pl.pallas_call(
        _na_kernel,
        out_shape=jax.ShapeDtypeStruct((bsz * length, NA_WIDTH), BF16),
        grid=(n_pairs, bsz),
        in_specs=[
            pl.BlockSpec((length, hp), lambda h, b: (b, COL_Q // hp + h)),
            pl.BlockSpec((length, hp), lambda h, b: (b, COL_K // hp + h)),
            pl.BlockSpec((length, hp), lambda h, b: (b, COL_V // hp + h)),
            pl.BlockSpec((1, WIN_ROWS, 2 * GRID_W, band), lambda h, b: (h, 0, 0, 0)),
            pl.BlockSpec((1, 2, 2 * GRID_W, N_META), lambda h, b: (h, 0, 0, 0)),
        ],
        out_specs=pl.BlockSpec((length, hp), lambda h, b: (b, h)),
        compiler_params=_params(("parallel", "parallel")),
        name="neighbourhood_attention",
    )(proj, proj, proj, tbl, mb2)


def _dwconv3(u, w):
    length = u.shape[0]
    row = lax.broadcasted_iota(jnp.int32, u.shape, 0)
    prev = jnp.where(row == 0, 0.0, pltpu.roll(u, 1, 0))
    nxt = jnp.where(row == length - 1, 0.0, pltpu.roll(u, length - 1, 0))
    return prev * w[0:1] + u * w[1:2] + nxt * w[2:3]


def _gated_conv_kernel(b_ref, c_ref, u_ref, w_ref, o_ref):
    u = c_ref[...].astype(F32) * u_ref[...].astype(F32)
    o_ref[...] = (b_ref[...].astype(F32) * _dwconv3(u, w_ref[...])).astype(o_ref.dtype)


def _plain_conv_kernel(u_ref, w_ref, o_ref):
    o_ref[...] = _dwconv3(u_ref[...].astype(F32), w_ref[...]).astype(o_ref.dtype)


CONV_CB = 256


def short_conv_mixer(proj, w_conv, *, bsz, length):
    cb = CONV_CB
    return pl.pallas_call(
        _gated_conv_kernel,
        out_shape=jax.ShapeDtypeStruct((bsz * length, SC_WIDTH), BF16),
        grid=(bsz, SC_WIDTH // cb),
        in_specs=[
            pl.BlockSpec((length, cb), lambda b, c: (b, COL_SCB // cb + c)),
            pl.BlockSpec((length, cb), lambda b, c: (b, COL_SCC // cb + c)),
            pl.BlockSpec((length, cb), lambda b, c: (b, COL_SCU // cb + c)),
            pl.BlockSpec((3, cb), lambda b, c: (0, c)),
        ],
        out_specs=pl.BlockSpec((length, cb), lambda b, c: (b, c)),
        compiler_params=_params(("parallel", "parallel")),
        name="short_conv_mixer",
    )(proj, proj, proj, w_conv.astype(F32))


def hyena_in_conv(proj, w_conv, *, bsz, length):
    cb = CONV_CB
    return pl.pallas_call(
        _plain_conv_kernel,
        out_shape=jax.ShapeDtypeStruct((bsz * length, 3 * HY_WIDTH), BF16),
        grid=(bsz, 3 * HY_WIDTH // cb),
        in_specs=[
            pl.BlockSpec((length, cb), lambda b, c: (b, COL_HYU // cb + c)),
            pl.BlockSpec((3, cb), lambda b, c: (0, c)),
        ],
        out_specs=pl.BlockSpec((length, cb), lambda b, c: (b, c)),
        compiler_params=_params(("parallel", "parallel")),
        name="hyena_in_conv",
    )(proj, w_conv.astype(F32))


def _filter_kernel(z_ref, w1_ref, b1_ref, w2_ref, b2_ref, w3_ref, fr_ref, dec_ref, f_ref, s_ref):
    hi = lax.Precision.HIGHEST
    z = z_ref[...]
    fr = fr_ref[...]
    h = jnp.sin(fr * (jnp.dot(z, w1_ref[...], preferred_element_type=F32, precision=hi) + b1_ref[...]))
    h = jnp.sin(fr * (jnp.dot(h, w2_ref[...], preferred_element_type=F32, precision=hi) + b2_ref[...]))
    h = jnp.dot(h, w3_ref[...], preferred_element_type=F32, precision=hi)
    t = z[:, 0:1]
    f = h * (jnp.exp(-t * jnp.abs(dec_ref[0])) + HY_MOD_SHIFT)
    row = lax.broadcasted_iota(jnp.int32, f.shape, 0)
    backward = pl.program_id(0) % 2 == 1
    f = jnp.where(jnp.logical_and(row == 0, backward), 0.0, f)
    f_ref[0] = f.astype(f_ref.dtype)
    s_ref[0] = jnp.sum(jnp.abs(f), axis=0, keepdims=True)


def hyena_filters(length, w1, b1, w2, b2, w3, freq, decay):
    bands = (HY_EMB - 1) // 2
    t = jnp.linspace(0.0, 1.0, length, dtype=F32)[:, None]
    ang = (2.0 * math.pi / length) * jnp.arange(length, dtype=F32)[:, None]
    fb = jnp.linspace(1e-4, bands - 1, bands, dtype=F32)[None, :]
    z = jnp.concatenate([t, jnp.cos(fb * ang), -jnp.sin(fb * ang)], axis=-1)
    emb = 128
    z = jnp.pad(z, ((0, 0), (0, emb - HY_EMB)))
    w1p = jnp.pad(w1.astype(F32), ((0, emb - HY_EMB), (0, 0)))
    n_slab = HY_ORDER * 2
    row = lambda a: a.astype(F32).reshape(1, -1)
    return pl.pallas_call(
        _filter_kernel,
        out_shape=(jax.ShapeDtypeStruct((n_slab, length, HY_WIDTH), BF16),
                   jax.ShapeDtypeStruct((n_slab, 1, HY_WIDTH), F32)),
        grid=(n_slab,),
        in_specs=[
            pl.BlockSpec((length, emb), lambda j: (0, 0)),
            pl.BlockSpec((emb, HY_HIDDEN), lambda j: (0, 0)),
            pl.BlockSpec((1, HY_HIDDEN), lambda j: (0, 0)),
            pl.BlockSpec((HY_HIDDEN, HY_HIDDEN), lambda j: (0, 0)),
            pl.BlockSpec((1, HY_HIDDEN), lambda j: (0, 0)),
            pl.BlockSpec((HY_HIDDEN, HY_WIDTH), lambda j: (0, j)),
            pl.BlockSpec((1, HY_HIDDEN), lambda j: (0, 0)),
            pl.BlockSpec((1, 1, HY_WIDTH), lambda j: (j, 0, 0)),
        ],
        out_specs=(pl.BlockSpec((1, length, HY_WIDTH), lambda j: (j, 0, 0)),
                   pl.BlockSpec((1, 1, HY_WIDTH), lambda j: (j, 0, 0))),
        compiler_params=_params(("parallel",)),
        name="hyena_filters",
    )(z, w1p, row(b1), w2.astype(F32), row(b2), w3.astype(F32), row(freq),
      decay.astype(F32).reshape(n_slab, 1, HY_WIDTH))


def _dft_matrices(length):
    k = jnp.arange(FFT_KP, dtype=jnp.int32)[:, None]
    n = jnp.arange(length, dtype=jnp.int32)[None, :]
    theta = (2.0 * math.pi / FFT_N) * ((k * n) % FFT_N).astype(F32)
    valid = k <= FFT_N // 2
    fc = jnp.where(valid, jnp.cos(theta), 0.0)
    fs = jnp.where(valid, -jnp.sin(theta), 0.0)
    return fc.astype(BF16), fs.astype(BF16), fc.T.astype(BF16), fs.T.astype(BF16)


def _dft_raw_kernel(fc_ref, fs_ref, u_ref, re_ref, im_ref):
    u = u_ref[0]
    re_ref[0] = jnp.dot(fc_ref[...], u, preferred_element_type=F32)
    im_ref[0] = jnp.dot(fs_ref[...], u, preferred_element_type=F32)


def dft_filters(fc, fs, filt):
    n_slab, length, c = filt.shape
    out = jax.ShapeDtypeStruct((n_slab, FFT_KP, c), F32)
    return pl.pallas_call(
        _dft_raw_kernel,
        out_shape=(out, out),
        grid=(FFT_KP // FFT_KB, n_slab),
        in_specs=[
            pl.BlockSpec((FFT_KB, length), lambda kb, s: (kb, 0)),
            pl.BlockSpec((FFT_KB, length), lambda kb, s: (kb, 0)),
            pl.BlockSpec((1, length, c), lambda kb, s: (s, 0, 0)),
        ],
        out_specs=(pl.BlockSpec((1, FFT_KB, c), lambda kb, s: (s, kb, 0)),
                   pl.BlockSpec((1, FFT_KB, c), lambda kb, s: (s, kb, 0))),
        compiler_params=_params(("parallel", "parallel")),
        name="dft_filters",
    )(fc, fs, filt)


def _dft_mul_kernel(fc_ref, fs_ref, u_ref, are_ref, aim_ref, s_ref, yre_ref, yim_ref, g_ref):
    @pl.when(pl.program_id(1) == 0)
    def _():
        k = pl.program_id(0) * FFT_KB + lax.broadcasted_iota(jnp.int32, (FFT_KB, 1), 0)
        wk = jnp.where(jnp.logical_or(k == 0, k == FFT_N // 2), 1.0, 2.0)
        wk = jnp.where(k > FFT_N // 2, 0.0, wk) / FFT_N
        inv = wk / (s_ref[0] + s_ref[1])
        g_ref[0] = (are_ref[0] + are_ref[1]) * inv
        g_ref[1] = (aim_ref[0] - aim_ref[1]) * inv

    u = u_ref[...]
    re = jnp.dot(fc_ref[...], u, preferred_element_type=F32)
    im = jnp.dot(fs_ref[...], u, preferred_element_type=F32)
    gre = g_ref[0]
    gim = g_ref[1]
    yre_ref[0] = (re * gre - im * gim).astype(yre_ref.dtype)
    yim_ref[0] = (re * gim + im * gre).astype(yim_ref.dtype)


def dft_forward_mul(fc, fs, u, col, a_re, a_im, abs_sum, order, *, bsz, length):
    c = HY_WIDTH
    out = jax.ShapeDtypeStruct((bsz, FFT_KP, c), BF16)
    return pl.pallas_call(
        _dft_mul_kernel,
        out_shape=(out, out),
        grid=(FFT_KP // FFT_KB, bsz),
        in_specs=[
            pl.BlockSpec((FFT_KB, length), lambda kb, b: (kb, 0)),
            pl.BlockSpec((FFT_KB, length), lambda kb, b: (kb, 0)),
            pl.BlockSpec((length, c), lambda kb, b: (b, col)),
            pl.BlockSpec((2, FFT_KB, c), lambda kb, b: (order, kb, 0)),
            pl.BlockSpec((2, FFT_KB, c), lambda kb, b: (order, kb, 0)),
            pl.BlockSpec((2, 1, c), lambda kb, b: (order, 0, 0)),
        ],
        out_specs=(pl.BlockSpec((1, FFT_KB, c), lambda kb, b: (b, kb, 0)),
                   pl.BlockSpec((1, FFT_KB, c), lambda kb, b: (b, kb, 0))),
        scratch_shapes=[pltpu.VMEM((2, FFT_KB, c), F32)],
        compiler_params=_params(("parallel", "arbitrary")),
        name="dft_forward_mul",
    )(fc, fs, u, a_re, a_im, abs_sum)


def _dft_inv_kernel(ct_ref, st_ref, yre_ref, yim_ref, u_ref, gate_ref, skip_ref, o_ref):
    y = jnp.dot(ct_ref[...], yre_ref[0], preferred_element_type=F32)
    y += jnp.dot(st_ref[...], yim_ref[0], preferred_element_type=F32)
    y += u_ref[...].astype(F32) * skip_ref[0]
    o_ref[...] = (gate_ref[...].astype(F32) * y).astype(o_ref.dtype)


def dft_inverse_gate(fct, fst, yre, yim, u, u_col, gate, gate_col, skip, order, *, bsz, length):
    c = HY_WIDTH
    nblk = length // INV_NB
    return pl.pallas_call(
        _dft_inv_kernel,
        out_shape=jax.ShapeDtypeStruct((bsz * length, c), BF16),
        grid=(nblk, bsz),
        in_specs=[
            pl.BlockSpec((INV_NB, FFT_KP), lambda nb, b: (nb, 0)),
            pl.BlockSpec((INV_NB, FFT_KP), lambda nb, b: (nb, 0)),
            pl.BlockSpec((1, FFT_KP, c), lambda nb, b: (b, 0, 0)),
            pl.BlockSpec((1, FFT_KP, c), lambda nb, b: (b, 0, 0)),
            pl.BlockSpec((INV_NB, c), lambda nb, b: (b * nblk + nb, u_col)),
            pl.BlockSpec((INV_NB, c), lambda nb, b: (b * nblk + nb, gate_col)),
            pl.BlockSpec((1, 1, c), lambda nb, b: (order, 0, 0)),
        ],
        out_specs=pl.BlockSpec((INV_NB, c), lambda nb, b: (b * nblk + nb, 0)),
        compiler_params=_params(("parallel", "parallel")),
        name="dft_inverse_gate",
    )(fct, fst, yre, yim, u, gate, skip)


def hyena_mixer(proj, w_conv, w1, b1, w2, b2, w3, freq, decay, skip, dft, *, bsz, length):
    fc, fs, fct, fst = dft
    vxx = hyena_in_conv(proj, w_conv, bsz=bsz, length=length)
    filt, abs_sum = hyena_filters(length, w1, b1, w2, b2, w3, freq, decay)
    a_re, a_im = dft_filters(fc, fs, filt)
    skip = skip.astype(F32).reshape(HY_ORDER, 1, HY_WIDTH)
    kw = dict(bsz=bsz, length=length)
    yre, yim = dft_forward_mul(fc, fs, vxx, 0, a_re, a_im, abs_sum, 0, **kw)
    z = dft_inverse_gate(fct, fst, yre, yim, vxx, 0, vxx, 1, skip, 0, **kw)
    yre, yim = dft_forward_mul(fc, fs, z, 0, a_re, a_im, abs_sum, 1, **kw)
    return dft_inverse_gate(fct, fst, yre, yim, z, 0, vxx, 2, skip, 1, **kw)


MOE_SUPER = 2 * MOE_BLOCK
MOE_FC = 512
MOE_VMEM_LIMIT = V7X_VMEM_BYTES - 3 * 1024 * 1024


def _moe_kernel(be_ref, nh_ref, x_ref, w1_ref, w3_ref, w2_ref, o_ref, acc_ref):
    i = pl.program_id(0)
    j = pl.program_id(1)
    halves = nh_ref[i]

    @pl.when(j == 0)
    def _():
        acc_ref[...] = jnp.zeros_like(acc_ref)

    def ffn(rows):
        x = x_ref[0:rows, :]
        a = jnp.dot(x, w1_ref[0].astype(BF16), preferred_element_type=F32)
        b = jnp.dot(x, w3_ref[0].astype(BF16), preferred_element_type=F32)
        hid = (a * _sigmoid(a) * b).astype(BF16)
        acc_ref[0:rows, :] += jnp.dot(hid, w2_ref[0].astype(BF16), preferred_element_type=F32)

    @pl.when(halves == 2)
    def _():
        ffn(MOE_SUPER)

    @pl.when(halves == 1)
    def _():
        ffn(MOE_BLOCK)

    @pl.when(j == pl.num_programs(1) - 1)
    def _():
        o_ref[...] = acc_ref[...].astype(o_ref.dtype)


def _moe_kernel_inplace(be_ref, nh_ref, x_ref, w1_ref, w3_ref, w2_ref, prev_ref, o_ref, acc_ref):
    del prev_ref
    _moe_kernel(be_ref, nh_ref, x_ref, w1_ref, w3_ref, w2_ref, o_ref, acc_ref)


def moe_experts(xb, sb_start, n_slots, block_expert, n_halves, w1, w3, w2, yb=None):
    d = xb.shape[1]
    n_super = xb.shape[0] // MOE_SUPER
    fc = MOE_FC
    n_fc = D_FF_EXPERT // fc

    def chunk(i, j, nh):
        return jnp.where(nh[i] > 0, j, n_fc - 1)

    in_specs = [
        pl.BlockSpec((MOE_SUPER, d), lambda i, j, be, nh: (i, 0)),
        pl.BlockSpec((1, d, fc), lambda i, j, be, nh: (be[i], 0, chunk(i, j, nh))),
        pl.BlockSpec((1, d, fc), lambda i, j, be, nh: (be[i], 0, chunk(i, j, nh))),
        pl.BlockSpec((1, fc, d), lambda i, j, be, nh: (be[i], chunk(i, j, nh), 0)),
    ]
    args = [block_expert, n_halves, xb, w1, w3, w2]
    aliases = {}
    if yb is not None:
        in_specs.append(pl.BlockSpec(memory_space=pl.ANY))
        args.append(yb)
        aliases = {len(args) - 1: 0}
    grid_spec = pltpu.PrefetchScalarGridSpec(
        num_scalar_prefetch=2,
        grid=(n_super, n_fc),
        in_specs=in_specs,
        out_specs=pl.BlockSpec((MOE_SUPER, d), lambda i, j, be, nh: (sb_start + i, 0)),
        scratch_shapes=[pltpu.VMEM((MOE_SUPER, d), F32)],
    )
    return pl.pallas_call(
        _moe_kernel if yb is None else _moe_kernel_inplace,
        out_shape=jax.ShapeDtypeStruct((n_slots, d), BF16),
        grid_spec=grid_spec,
        input_output_aliases=aliases,
        compiler_params=_params(("parallel", "arbitrary"), MOE_VMEM_LIMIT),
        name="moe_experts",
    )(*args)


COMBINE_ROWS = 256


def _moe_combine_kernel(h_ref, y0_ref, y1_ref, g0_ref, g1_ref, gn_ref, o_ref, *, final):
    rows = o_ref.shape[0]
    gn = gn_ref[...]

    def chunk(c, carry):
        r0 = pl.multiple_of(c * NORM_CHUNK, NORM_CHUNK)
        sl = pl.ds(r0, NORM_CHUNK)
        h = h_ref[0, sl, :] if final else h_ref[sl, :]
        y = h + g0_ref[sl, :] * y0_ref[sl, :].astype(F32) + g1_ref[sl, :] * y1_ref[sl, :].astype(F32)
        if final:
            ms = jnp.mean(y * y, axis=-1, keepdims=True)
            y = y * lax.rsqrt(ms + NORM_EPS) * gn
        o_ref[sl, :] = y.astype(o_ref.dtype)
        return carry

    lax.fori_loop(0, rows // NORM_CHUNK, chunk, 0, unroll=4)


def moe_combine(h, y0, y1, g0, g1, *, tm=ROW_TILE):
    t, d = h.shape
    row = lambda w: pl.BlockSpec((tm, w), lambda i: (i, 0))
    return pl.pallas_call(
        functools.partial(_moe_combine_kernel, final=False),
        out_shape=jax.ShapeDtypeStruct((t, d), F32),
        grid=(t // tm,),
        in_specs=[row(d), row(d), row(d), row(1), row(1), pl.BlockSpec((1, d), lambda i: (0, 0))],
        out_specs=row(d),
        compiler_params=_params(("parallel",)),
        name="moe_combine",
    )(h, y0, y1, g0, g1, jnp.ones((1, d), F32))


def moe_combine_final(h, y0, y1, g0, g1, g_final, *, bsz, length):
    d = h.shape[1]
    seq = length - N_META
    r = COMBINE_ROWS
    nr = seq // r
    row = lambda w: pl.BlockSpec((r, w), lambda b, i: (b * nr + i, 0))
    h_spec = pl.BlockSpec((pl.Element(1), pl.Element(r), pl.Element(d)),
                          lambda b, i: (b, pl.multiple_of(N_META + i * r, N_META), 0))
    return pl.pallas_call(
        functools.partial(_moe_combine_kernel, final=True),
        out_shape=jax.ShapeDtypeStruct((bsz, seq, d), F32),
        grid=(bsz, nr),
        in_specs=[h_spec, row(d), row(d), row(1), row(1), pl.BlockSpec((1, d), lambda b, i: (0, 0))],
        out_specs=pl.BlockSpec((None, r, d), lambda b, i: (b, i, 0)),
        compiler_params=_params(("parallel", "parallel")),
        name="moe_combine_final",
    )(h.reshape(bsz, length, d), y0, y1, g0, g1, g_final.astype(F32).reshape(1, d))


SCAN_CHUNK = 256
MOE_CALL_SPLIT = (4, 16, None)


def _expert_ranks(expert):
    n = expert.shape[0]
    onehot = (expert[:, None] == jnp.arange(N_EXPERTS, dtype=jnp.int32)[None, :]).astype(F32)
    chunks = onehot.reshape(n // SCAN_CHUNK, SCAN_CHUNK, N_EXPERTS)
    tri = jnp.tril(jnp.ones((SCAN_CHUNK, SCAN_CHUNK), F32), k=-1)
    within = jnp.einsum('ij,cjk->cik', tri, chunks)
    totals = jnp.sum(chunks, axis=1)
    before = jnp.cumsum(totals, axis=0) - totals
    rank = jnp.sum((within + before[:, None, :]) * chunks, axis=-1).reshape(n)
    return rank.astype(jnp.int32), jnp.sum(totals, axis=0).astype(jnp.int32)


def moe_swiglu(h, g_norm, w_router, w1, w3, w2, g_final, *, bsz, length):
    n_tok, d = h.shape
    n_assign = n_tok * TOP_K
    router_pad = 128
    wr = jnp.pad(w_router.astype(F32), ((0, 0), (0, router_pad - N_EXPERTS)))
    logits, hn = norm_router(h, g_norm, wr)
    top_val, top_idx = lax.top_k(logits[:, :N_EXPERTS], TOP_K)
    gate = jax.nn.softmax(top_val, axis=-1)
    expert = top_idx.astype(jnp.int32).reshape(n_assign)
    rank, counts = _expert_ranks(expert)
    padded = (counts + MOE_SUPER - 1) // MOE_SUPER * MOE_SUPER
    pad_end = jnp.cumsum(padded)
    pad_start = pad_end - padded
    slot = (pad_start[expert] + rank).reshape(n_tok, TOP_K)
    n_super = -(-n_assign // MOE_SUPER) + N_EXPERTS
    n_slots = n_super * MOE_SUPER
    token = jnp.repeat(jnp.arange(n_tok, dtype=jnp.int32), TOP_K)
    slot_token = jnp.zeros((n_slots,), jnp.int32).at[slot.reshape(n_assign)].set(token)
    n_used = pad_end[-1] // MOE_SUPER
    sb = jnp.arange(n_super, dtype=jnp.int32)
    block_expert = jnp.minimum(jnp.searchsorted(pad_end, jnp.minimum(sb, n_used - 1) * MOE_SUPER, side='right'),
                               N_EXPERTS - 1).astype(jnp.int32)
    rows_here = jnp.clip(counts[block_expert] - (sb * MOE_SUPER - pad_start[block_expert]), 0, MOE_SUPER)
    n_halves = jnp.where(sb < n_used, (rows_here + MOE_BLOCK - 1) // MOE_BLOCK, 0).astype(jnp.int32)
    yb = None
    sb0 = 0
    for part in MOE_CALL_SPLIT:
        sb1 = n_super if part is None else sb0 + part
        rows = slice(sb0 * MOE_SUPER, sb1 * MOE_SUPER)
        yb = moe_experts(hn[slot_token[rows]], sb0, n_slots, block_expert[sb0:sb1], n_halves[sb0:sb1],
                         w1, w3, w2, yb)
        sb0 = sb1
    if g_final is None:
        return moe_combine(h, yb[slot[:, 0]], yb[slot[:, 1]], gate[:, 0:1], gate[:, 1:2])
    keep = lambda a: a.reshape(bsz, length, -1)[:, N_META:].reshape(bsz * (length - N_META), -1)
    slot, gate = keep(slot), keep(gate)
    return moe_combine_final(h, yb[slot[:, 0]], yb[slot[:, 1]], gate[:, 0:1], gate[:, 1:2], g_final,
                             bsz=bsz, length=length)


def kernel(x, meta_tokens, norm_mix, w_in, na_rpb, na_meta_bias, sc_conv, hy_conv, hy_w1, hy_b1, hy_w2, hy_b2,
           hy_w3, hy_freq, hy_decay, hy_skip, w_na_out, w_sc_out, w_hy_out, w_o, norm_ffn, ffn_w1, ffn_w3,
           ffn_w2, router_w, moe_w1, moe_w3, moe_w2, norm_final):
    bsz, seq, d = x.shape
    length = N_META + seq
    depth = w_in.shape[0]
    meta = jnp.broadcast_to(meta_tokens.astype(x.dtype)[None], (bsz, N_META, d))
    h = jnp.concatenate([meta, x], axis=1).reshape(bsz * length, d)
    dft = _dft_matrices(length)
    kw = dict(bsz=bsz, length=length)
    w_in, w_na_out, w_sc_out, w_hy_out, w_o, ffn_w1, ffn_w3, ffn_w2 = map(
        to_bf16, (w_in, w_na_out, w_sc_out, w_hy_out, w_o, ffn_w1, ffn_w3, ffn_w2))
    for layer in range(depth):
        proj = norm_matmul(h, norm_mix[layer], w_in[layer], tn=2048)
        y_na = neighbourhood_attention(proj, na_rpb[layer], na_meta_bias[layer], **kw)
        y_sc = short_conv_mixer(proj, sc_conv[layer], **kw)
        y_hy = hyena_mixer(proj, hy_conv[layer], hy_w1[layer], hy_b1[layer], hy_w2[layer], hy_b2[layer],
                           hy_w3[layer], hy_freq[layer], hy_decay[layer], hy_skip[layer], dft, **kw)
        h = mixer_out(h, y_na, y_sc, y_hy, proj, w_na_out[layer], w_sc_out[layer], w_hy_out[layer], w_o[layer])
        j = layer // 2
        if layer % 2 == 0:
            h = dense_ffn(h, norm_ffn[layer], ffn_w1[j], ffn_w3[j], ffn_w2[j])
        else:
            g_final = norm_final if layer == depth - 1 else None
            h = moe_swiglu(h, norm_ffn[layer], router_w[j], moe_w1[j], moe_w3[j], moe_w2[j], g_final, **kw)
    if depth % 2 == 0:
        return h
    out = rms_norm_rows(h, norm_final, out_dtype=x.dtype)
    return out.reshape(bsz, length, d)[:, N_META:]
```

```python
import functools
import math

import numpy as np
import jax
import jax.numpy as jnp
from jax import lax
from jax.experimental import pallas as pl
from jax.experimental.pallas import tpu as pltpu

D_MODEL = 2048
N_META = 16
GRID_W = 64
GRID_ROWS = 32
NA_HEADS = 16
NA_HEAD_DIM = 64
NA_WIDTH = NA_HEADS * NA_HEAD_DIM
WIN_ROWS = 8
WIN_COLS = 16
SC_WIDTH = 512
HY_WIDTH = 512
HY_ORDER = 2
HY_EMB = 33
HY_HIDDEN = 64
HY_MOD_SHIFT = 0.05
PROJ_WIDTH = 3 * NA_WIDTH + 3 * SC_WIDTH + 3 * HY_WIDTH + 3 * D_MODEL
D_FF = 5632
N_EXPERTS = 8
TOP_K = 2
D_FF_EXPERT = 7168
MOE_BLOCK = 512
NORM_EPS = 1e-6
NEG_INF = -1e30

COL_Q, COL_K, COL_V = 0, NA_WIDTH, 2 * NA_WIDTH
COL_SCB = 3 * NA_WIDTH
COL_SCC = COL_SCB + SC_WIDTH
COL_SCU = COL_SCC + SC_WIDTH
COL_HYU = COL_SCU + SC_WIDTH
COL_GNA = COL_HYU + 3 * HY_WIDTH
COL_GSC = COL_GNA + D_MODEL
COL_GHY = COL_GSC + D_MODEL

V7X_VMEM_BYTES = 64 * 1024 * 1024
VMEM_LIMIT = V7X_VMEM_BYTES - 8 * 1024 * 1024

ROW_TILE = 768
NORM_CHUNK = 16
FFT_N = 2 * (N_META + GRID_ROWS * GRID_W)
FFT_KP = 2080
FFT_KB = 1040
INV_NB = 688

F32 = jnp.float32
BF16 = jnp.bfloat16


def _params(sem, vmem_limit=VMEM_LIMIT):
    return pltpu.CompilerParams(dimension_semantics=sem, vmem_limit_bytes=vmem_limit)


def _sigmoid(a):
    return 0.5 * jnp.tanh(0.5 * a) + 0.5


CAST_BLOCK_BYTES = 8 * 1024 * 1024


def _cast_kernel(x_ref, o_ref):
    o_ref[...] = x_ref[...].astype(o_ref.dtype)


def to_bf16(w):
    cols = w.shape[-1]
    w2 = w.reshape(-1, cols)
    rows = w2.shape[0]
    tr = min(rows, CAST_BLOCK_BYTES // (4 * cols)) // 16 * 16
    while rows % tr:
        tr -= 16
    out = pl.pallas_call(
        _cast_kernel,
        out_shape=jax.ShapeDtypeStruct((rows, cols), BF16),
        grid=(rows // tr,),
        in_specs=[pl.BlockSpec((tr, cols), lambda i: (i, 0))],
        out_specs=pl.BlockSpec((tr, cols), lambda i: (i, 0)),
        compiler_params=_params(("parallel",)),
        name="to_bf16",
    )(w2)
    return out.reshape(w.shape)


def _rms_norm_rows(x_ref, g_ref, xn_ref):
    rows = x_ref.shape[0]
    g = g_ref[...]

    def chunk(c, carry):
        r0 = pl.multiple_of(c * NORM_CHUNK, NORM_CHUNK)
        x = x_ref[pl.ds(r0, NORM_CHUNK), :]
        ms = jnp.mean(x * x, axis=-1, keepdims=True)
        xn_ref[pl.ds(r0, NORM_CHUNK), :] = (x * lax.rsqrt(ms + NORM_EPS) * g).astype(xn_ref.dtype)
        return carry

    lax.fori_loop(0, rows // NORM_CHUNK, chunk, 0, unroll=4)


def _norm_mm_kernel(x_ref, g_ref, w_ref, o_ref, xn_ref):
    @pl.when(pl.program_id(1) == 0)
    def _():
        _rms_norm_rows(x_ref, g_ref, xn_ref)

    o_ref[...] = jnp.dot(xn_ref[...], w_ref[...], preferred_element_type=F32).astype(o_ref.dtype)


def norm_matmul(x, g, w, *, tn, out_dtype=BF16, tm=ROW_TILE):
    t, k = x.shape
    n = w.shape[1]
    return pl.pallas_call(
        _norm_mm_kernel,
        out_shape=jax.ShapeDtypeStruct((t, n), out_dtype),
        grid=(t // tm, n // tn),
        in_specs=[
            pl.BlockSpec((tm, k), lambda i, j: (i, 0)),
            pl.BlockSpec((1, k), lambda i, j: (0, 0)),
            pl.BlockSpec((k, tn), lambda i, j: (0, j)),
        ],
        out_specs=pl.BlockSpec((tm, tn), lambda i, j: (i, j)),
        scratch_shapes=[pltpu.VMEM((tm, k), BF16)],
        compiler_params=_params(("parallel", "arbitrary")),
        name="norm_matmul",
    )(x, g.reshape(1, k), w)


def _norm_router_kernel(x_ref, g_ref, w_ref, o_ref, xb_ref, xn_ref):
    _rms_norm_rows(x_ref, g_ref, xn_ref)
    xn = xn_ref[...]
    xb_ref[...] = xn.astype(xb_ref.dtype)
    o_ref[...] = jnp.dot(xn, w_ref[...], preferred_element_type=F32, precision=lax.Precision.HIGHEST)


def norm_router(x, g, w, *, tm=ROW_TILE):
    t, k = x.shape
    n = w.shape[1]
    return pl.pallas_call(
        _norm_router_kernel,
        out_shape=(jax.ShapeDtypeStruct((t, n), F32), jax.ShapeDtypeStruct((t, k), BF16)),
        grid=(t // tm,),
        in_specs=[
            pl.BlockSpec((tm, k), lambda i: (i, 0)),
            pl.BlockSpec((1, k), lambda i: (0, 0)),
            pl.BlockSpec((k, n), lambda i: (0, 0)),
        ],
        out_specs=(pl.BlockSpec((tm, n), lambda i: (i, 0)), pl.BlockSpec((tm, k), lambda i: (i, 0))),
        scratch_shapes=[pltpu.VMEM((tm, k), F32)],
        compiler_params=_params(("parallel",)),
        name="norm_router",
    )(x, g.reshape(1, k), w)


def _norm_kernel(x_ref, g_ref, o_ref):
    _rms_norm_rows(x_ref, g_ref, o_ref)


def rms_norm_rows(x, g, *, out_dtype, tm=ROW_TILE):
    t, k = x.shape
    return pl.pallas_call(
        _norm_kernel,
        out_shape=jax.ShapeDtypeStruct((t, k), out_dtype),
        grid=(t // tm,),
        in_specs=[pl.BlockSpec((tm, k), lambda i: (i, 0)), pl.BlockSpec((1, k), lambda i: (0, 0))],
        out_specs=pl.BlockSpec((tm, k), lambda i: (i, 0)),
        compiler_params=_params(("parallel",)),
        name="rms_norm",
    )(x, g.reshape(1, k))


def _dense_ffn_kernel(x_ref, g_ref, w1_ref, w3_ref, w2_ref, o_ref, xn_ref):
    @pl.when(pl.program_id(1) == 0)
    def _():
        _rms_norm_rows(x_ref, g_ref, xn_ref)
        o_ref[...] = x_ref[...]

    xn = xn_ref[...]
    a = jnp.dot(xn, w1_ref[...], preferred_element_type=F32)
    b = jnp.dot(xn, w3_ref[...], preferred_element_type=F32)
    hid = (a * _sigmoid(a) * b).astype(BF16)
    o_ref[...] += jnp.dot(hid, w2_ref[...], preferred_element_type=F32)


def dense_ffn(x, g, w1, w3, w2, *, fc=512, tm=ROW_TILE):
    t, d = x.shape
    f = w1.shape[1]
    return pl.pallas_call(
        _dense_ffn_kernel,
        out_shape=jax.ShapeDtypeStruct((t, d), F32),
        grid=(t // tm, f // fc),
        in_specs=[
            pl.BlockSpec((tm, d), lambda i, j: (i, 0)),
            pl.BlockSpec((1, d), lambda i, j: (0, 0)),
            pl.BlockSpec((d, fc), lambda i, j: (0, j)),
            pl.BlockSpec((d, fc), lambda i, j: (0, j)),
            pl.BlockSpec((fc, d), lambda i, j: (j, 0)),
        ],
        out_specs=pl.BlockSpec((tm, d), lambda i, j: (i, 0)),
        scratch_shapes=[pltpu.VMEM((tm, d), BF16)],
        compiler_params=_params(("parallel", "arbitrary")),
        name="dense_ffn",
    )(x, g.reshape(1, d), w1, w3, w2)


def _mixer_out_kernel(res_ref, yna_ref, ysc_ref, yhy_ref, gna_ref, gsc_ref, ghy_ref, wna_ref, wsc_ref, why_ref,
                      wo_ref, o_ref):
    @pl.when(pl.program_id(1) == 0)
    def _():
        o_ref[...] = res_ref[...]

    m = _sigmoid(gna_ref[...].astype(F32)) * jnp.dot(yna_ref[...], wna_ref[...], preferred_element_type=F32)
    m += _sigmoid(gsc_ref[...].astype(F32)) * jnp.dot(ysc_ref[...], wsc_ref[...], preferred_element_type=F32)
    m += _sigmoid(ghy_ref[...].astype(F32)) * jnp.dot(yhy_ref[...], why_ref[...], preferred_element_type=F32)
    o_ref[...] += jnp.dot(m.astype(BF16), wo_ref[...], preferred_element_type=F32)


MIXER_ROWS = 256


def mixer_out(res, y_na, y_sc, y_hy, proj, w_na, w_sc, w_hy, w_o, *, cw=D_MODEL, tm=MIXER_ROWS):
    t, d = res.shape
    return pl.pallas_call(
        _mixer_out_kernel,
        out_shape=jax.ShapeDtypeStruct((t, d), F32),
        grid=(t // tm, d // cw),
        in_specs=[
            pl.BlockSpec((tm, d), lambda i, c: (i, 0)),
            pl.BlockSpec((tm, NA_WIDTH), lambda i, c: (i, 0)),
            pl.BlockSpec((tm, SC_WIDTH), lambda i, c: (i, 0)),
            pl.BlockSpec((tm, HY_WIDTH), lambda i, c: (i, 0)),
            pl.BlockSpec((tm, cw), lambda i, c: (i, COL_GNA // cw + c)),
            pl.BlockSpec((tm, cw), lambda i, c: (i, COL_GSC // cw + c)),
            pl.BlockSpec((tm, cw), lambda i, c: (i, COL_GHY // cw + c)),
            pl.BlockSpec((NA_WIDTH, cw), lambda i, c: (0, c)),
            pl.BlockSpec((SC_WIDTH, cw), lambda i, c: (0, c)),
            pl.BlockSpec((HY_WIDTH, cw), lambda i, c: (0, c)),
            pl.BlockSpec((cw, d), lambda i, c: (c, 0)),
        ],
        out_specs=pl.BlockSpec((tm, d), lambda i, c: (i, 0)),
        compiler_params=_params(("parallel", "arbitrary")),
        name="mixer_out",
    )(res, y_na, y_sc, y_hy, proj, proj, proj, w_na, w_sc, w_hy, w_o)


def _na_bias_table(rpb):
    qc = np.arange(GRID_W)[:, None]
    kc = np.arange(GRID_W)[None, :]
    win_start = np.clip(qc - WIN_COLS // 2, 0, GRID_W - WIN_COLS)
    inside = (kc >= win_start) & (kc < win_start + WIN_COLS)
    dc = np.clip(kc - qc, -(WIN_COLS - 1), WIN_COLS - 1) + WIN_COLS - 1
    dr = np.arange(WIN_ROWS)[None, :] - np.arange(WIN_ROWS)[:, None] + WIN_ROWS - 1
    onehot = np.zeros((2 * WIN_COLS - 1, GRID_W * GRID_W), np.float32)
    onehot[dc.reshape(-1), np.arange(GRID_W * GRID_W)] = 1.0
    tbl = jnp.einsum('htid,dj->htij', rpb.astype(F32)[:, dr], jnp.asarray(onehot),
                     precision=lax.Precision.HIGHEST)
    tbl = tbl.reshape(NA_HEADS, WIN_ROWS, WIN_ROWS, GRID_W, GRID_W)
    tbl = jnp.where(jnp.asarray(inside)[None, None, None], tbl, NEG_INF)
    tbl = tbl.transpose(0, 1, 3, 2, 4)
    return tbl.reshape(NA_HEADS, WIN_ROWS, GRID_W, WIN_ROWS * GRID_W)


NA_ROW_GROUP = 4


def _na_kernel(q_ref, k_ref, v_ref, tbl_ref, mb_ref, o_ref):
    scale = NA_HEAD_DIM ** -0.5
    lane = lax.broadcasted_iota(jnp.int32, (1, 2 * NA_HEAD_DIM), 1)
    first = lane < NA_HEAD_DIM
    nt = (((1,), (1,)), ((), ()))
    k_meta = k_ref[0:N_META, :]
    v_meta = v_ref[0:N_META, :]
    mb_grid = mb_ref[0, 0]
    band = WIN_ROWS * GRID_W

    def stack_heads(q):
        q = q * scale
        zero = jnp.zeros_like(q)
        return jnp.concatenate([jnp.where(first, q, zero), jnp.where(first, zero, q)], axis=0)

    def unstack_heads(out, den, n):
        out = out / den
        return jnp.where(first, out[:n], out[n:])

    def softmax_parts(s_grid, s_meta):
        m = jnp.max(s_meta, axis=-1, keepdims=True)
        if s_grid is not None:
            m = jnp.maximum(m, jnp.max(s_grid, axis=-1, keepdims=True))
        p_meta = jnp.exp(s_meta - m)
        den = jnp.sum(p_meta, axis=-1, keepdims=True)
        p_grid = None
        if s_grid is not None:
            p_grid = jnp.exp(s_grid - m)
            den = den + jnp.sum(p_grid, axis=-1, keepdims=True)
            p_grid = p_grid.astype(BF16)
        return p_grid, p_meta.astype(BF16), den

    def attend(rows, with_meta_queries):
        geo = []
        for r in rows:
            rs = min(max(r - WIN_ROWS // 2, 0), GRID_ROWS - WIN_ROWS)
            geo.append((N_META + r * GRID_W, N_META + rs * GRID_W, r - rs))
        scores = []
        if with_meta_queries:
            q2 = stack_heads(q_ref[0:N_META, :])
            s_meta = lax.dot_general(q2, k_meta, nt, preferred_element_type=F32) + mb_ref[0, 1, 0:2 * N_META, :]
            scores.append((None, s_meta))
        for q0, k0, typ in geo:
            q2 = stack_heads(q_ref[q0:q0 + GRID_W, :])
            s_grid = lax.dot_general(q2, k_ref[k0:k0 + band, :], nt, preferred_element_type=F32) + tbl_ref[0, typ]
            s_meta = lax.dot_general(q2, k_meta, nt, preferred_element_type=F32) + mb_grid
            scores.append((s_grid, s_meta))
        probs = [softmax_parts(*sc) for sc in scores]
        if with_meta_queries:
            _, p_meta, den = probs.pop(0)
            out = jnp.dot(p_meta, v_meta, preferred_element_type=F32)
            o_ref[0:N_META, :] = unstack_heads(out, den, N_META).astype(o_ref.dtype)
        for (q0, k0, typ), (p_grid, p_meta, den) in zip(geo, probs):
            out = jnp.dot(p_grid, v_ref[k0:k0 + band, :], preferred_element_type=F32)
            out += jnp.dot(p_meta, v_meta, preferred_element_type=F32)
            o_ref[q0:q0 + GRID_W, :] = unstack_heads(out, den, GRID_W).astype(o_ref.dtype)

    for g0 in range(0, GRID_ROWS, NA_ROW_GROUP):
        attend(range(g0, g0 + NA_ROW_GROUP), with_meta_queries=(g0 == 0))


def neighbourhood_attention(proj, rpb, meta_bias, *, bsz, length):
    hp = 2 * NA_HEAD_DIM
    n_pairs = NA_HEADS // 2
    band = WIN_ROWS * GRID_W
    tbl = _na_bias_table(rpb).reshape(n_pairs, 2, WIN_ROWS, GRID_W, band).transpose(0, 2, 1, 3, 4)
    tbl = tbl.reshape(n_pairs, WIN_ROWS, 2 * GRID_W, band)
    mb = meta_bias.astype(F32).reshape(n_pairs, 2, 1, N_META)
    mb_grid = jnp.broadcast_to(mb, (n_pairs, 2, GRID_W, N_META)).reshape(n_pairs, 2 * GRID_W, N_META)
    mb_meta = jnp.broadcast_to(mb, (n_pairs, 2, N_META, N_META)).reshape(n_pairs, 2 * N_META, N_META)
    mb_meta = jnp.pad(mb_meta, ((0, 0), (0, 2 * GRID_W - 2 * N_META), (0, 0)))
    mb2 = jnp.stack([mb_grid, mb_meta], axis=1)
    return pl.pallas_call(
        _na_kernel,
        out_shape=jax.ShapeDtypeStruct((bsz * length, NA_WIDTH), BF16),
        grid=(n_pairs, bsz),
        in_specs=[
            pl.BlockSpec((length, hp), lambda h, b: (b, COL_Q // hp + h)),
            pl.BlockSpec((length, hp), lambda h, b: (b, COL_K // hp + h)),
            pl.BlockSpec((length, hp), lambda h, b: (b, COL_V // hp + h)),
            pl.BlockSpec((1, WIN_ROWS, 2 * GRID_W, band), lambda h, b: (h, 0, 0, 0)),
            pl.BlockSpec((1, 2, 2 * GRID_W, N_META), lambda h, b: (h, 0, 0, 0)),
        ],
        out_specs=pl.BlockSpec((length, hp), lambda h, b: (b, h)),
        compiler_params=_params(("parallel", "parallel")),
        name="neighbourhood_attention",
    )(proj, proj, proj, tbl, mb2)


def _dwconv3(u, w):
    length = u.shape[0]
    row = lax.broadcasted_iota(jnp.int32, u.shape, 0)
    prev = jnp.where(row == 0, 0.0, pltpu.roll(u, 1, 0))
    nxt = jnp.where(row == length - 1, 0.0, pltpu.roll(u, length - 1, 0))
    return prev * w[0:1] + u * w[1:2] + nxt * w[2:3]


def _gated_conv_kernel(b_ref, c_ref, u_ref, w_ref, o_ref):
    u = c_ref[...].astype(F32) * u_ref[...].astype(F32)
    o_ref[...] = (b_ref[...].astype(F32) * _dwconv3(u, w_ref[...])).astype(o_ref.dtype)


def _plain_conv_kernel(u_ref, w_ref, o_ref):
    o_ref[...] = _dwconv3(u_ref[...].astype(F32), w_ref[...]).astype(o_ref.dtype)


CONV_CB = 256


def short_conv_mixer(proj, w_conv, *, bsz, length):
    cb = CONV_CB
    return pl.pallas_call(
        _gated_conv_kernel,
        out_shape=jax.ShapeDtypeStruct((bsz * length, SC_WIDTH), BF16),
        grid=(bsz, SC_WIDTH // cb),
        in_specs=[
            pl.BlockSpec((length, cb), lambda b, c: (b, COL_SCB // cb + c)),
            pl.BlockSpec((length, cb), lambda b, c: (b, COL_SCC // cb + c)),
            pl.BlockSpec((length, cb), lambda b, c: (b, COL_SCU // cb + c)),
            pl.BlockSpec((3, cb), lambda b, c: (0, c)),
        ],
        out_specs=pl.BlockSpec((length, cb), lambda b, c: (b, c)),
        compiler_params=_params(("parallel", "parallel")),
        name="short_conv_mixer",
    )(proj, proj, proj, w_conv.astype(F32))


def hyena_in_conv(proj, w_conv, *, bsz, length):
    cb = CONV_CB
    return pl.pallas_call(
        _plain_conv_kernel,
        out_shape=jax.ShapeDtypeStruct((bsz * length, 3 * HY_WIDTH), BF16),
        grid=(bsz, 3 * HY_WIDTH // cb),
        in_specs=[
            pl.BlockSpec((length, cb), lambda b, c: (b, COL_HYU // cb + c)),
            pl.BlockSpec((3, cb), lambda b, c: (0, c)),
        ],
        out_specs=pl.BlockSpec((length, cb), lambda b, c: (b, c)),
        compiler_params=_params(("parallel", "parallel")),
        name="hyena_in_conv",
    )(proj, w_conv.astype(F32))


def _filter_kernel(z_ref, w1_ref, b1_ref, w2_ref, b2_ref, w3_ref, fr_ref, dec_ref, f_ref, s_ref):
    hi = lax.Precision.HIGHEST
    z = z_ref[...]
    fr = fr_ref[...]
    h = jnp.sin(fr * (jnp.dot(z, w1_ref[...], preferred_element_type=F32, precision=hi) + b1_ref[...]))
    h = jnp.sin(fr * (jnp.dot(h, w2_ref[...], preferred_element_type=F32, precision=hi) + b2_ref[...]))
    h = jnp.dot(h, w3_ref[...], preferred_element_type=F32, precision=hi)
    t = z[:, 0:1]
    f = h * (jnp.exp(-t * jnp.abs(dec_ref[0])) + HY_MOD_SHIFT)
    row = lax.broadcasted_iota(jnp.int32, f.shape, 0)
    backward = pl.program_id(0) % 2 == 1
    f = jnp.where(jnp.logical_and(row == 0, backward), 0.0, f)
    f_ref[0] = f.astype(f_ref.dtype)
    s_ref[0] = jnp.sum(jnp.abs(f), axis=0, keepdims=True)


def hyena_filters(length, w1, b1, w2, b2, w3, freq, decay):
    bands = (HY_EMB - 1) // 2
    t = jnp.linspace(0.0, 1.0, length, dtype=F32)[:, None]
    ang = (2.0 * math.pi / length) * jnp.arange(length, dtype=F32)[:, None]
    fb = jnp.linspace(1e-4, bands - 1, bands, dtype=F32)[None, :]
    z = jnp.concatenate([t, jnp.cos(fb * ang), -jnp.sin(fb * ang)], axis=-1)
    emb = 128
    z = jnp.pad(z, ((0, 0), (0, emb - HY_EMB)))
    w1p = jnp.pad(w1.astype(F32), ((0, emb - HY_EMB), (0, 0)))
    n_slab = HY_ORDER * 2
    row = lambda a: a.astype(F32).reshape(1, -1)
    return pl.pallas_call(
        _filter_kernel,
        out_shape=(jax.ShapeDtypeStruct((n_slab, length, HY_WIDTH), BF16),
                   jax.ShapeDtypeStruct((n_slab, 1, HY_WIDTH), F32)),
        grid=(n_slab,),
        in_specs=[
            pl.BlockSpec((length, emb), lambda j: (0, 0)),
            pl.BlockSpec((emb, HY_HIDDEN), lambda j: (0, 0)),
            pl.BlockSpec((1, HY_HIDDEN), lambda j: (0, 0)),
            pl.BlockSpec((HY_HIDDEN, HY_HIDDEN), lambda j: (0, 0)),
            pl.BlockSpec((1, HY_HIDDEN), lambda j: (0, 0)),
            pl.BlockSpec((HY_HIDDEN, HY_WIDTH), lambda j: (0, j)),
            pl.BlockSpec((1, HY_HIDDEN), lambda j: (0, 0)),
            pl.BlockSpec((1, 1, HY_WIDTH), lambda j: (j, 0, 0)),
        ],
        out_specs=(pl.BlockSpec((1, length, HY_WIDTH), lambda j: (j, 0, 0)),
                   pl.BlockSpec((1, 1, HY_WIDTH), lambda j: (j, 0, 0))),
        compiler_params=_params(("parallel",)),
        name="hyena_filters",
    )(z, w1p, row(b1), w2.astype(F32), row(b2), w3.astype(F32), row(freq),
      decay.astype(F32).reshape(n_slab, 1, HY_WIDTH))


def _dft_matrices(length):
    k = jnp.arange(FFT_KP, dtype=jnp.int32)[:, None]
    n = jnp.arange(length, dtype=jnp.int32)[None, :]
    theta = (2.0 * math.pi / FFT_N) * ((k * n) % FFT_N).astype(F32)
    valid = k <= FFT_N // 2
    fc = jnp.where(valid, jnp.cos(theta), 0.0).astype(BF16)
    fs = jnp.where(valid, -jnp.sin(theta), 0.0).astype(BF16)
    fc, fs = lax.optimization_barrier((fc, fs))
    return fc, fs, fc.T, fs.T


def _dft_raw_kernel(fc_ref, fs_ref, u_ref, re_ref, im_ref):
    u = u_ref[0]
    re_ref[0] = jnp.dot(fc_ref[...], u, preferred_element_type=F32)
    im_ref[0] = jnp.dot(fs_ref[...], u, preferred_element_type=F32)


def dft_filters(fc, fs, filt):
    n_slab, length, c = filt.shape
    out = jax.ShapeDtypeStruct((n_slab, FFT_KP, c), F32)
    return pl.pallas_call(
        _dft_raw_kernel,
        out_shape=(out, out),
        grid=(FFT_KP // FFT_KB, n_slab),
        in_specs=[
            pl.BlockSpec((FFT_KB, length), lambda kb, s: (kb, 0)),
            pl.BlockSpec((FFT_KB, length), lambda kb, s: (kb, 0)),
            pl.BlockSpec((1, length, c), lambda kb, s: (s, 0, 0)),
        ],
        out_specs=(pl.BlockSpec((1, FFT_KB, c), lambda kb, s: (s, kb, 0)),
                   pl.BlockSpec((1, FFT_KB, c), lambda kb, s: (s, kb, 0))),
        compiler_params=_params(("parallel", "parallel")),
        name="dft_filters",
    )(fc, fs, filt)


def _dft_mul_kernel(fc_ref, fs_ref, u_ref, are_ref, aim_ref, s_ref, yre_ref, yim_ref, g_ref):
    @pl.when(pl.program_id(1) == 0)
    def _():
        k = pl.program_id(0) * FFT_KB + lax.broadcasted_iota(jnp.int32, (FFT_KB, 1), 0)
        wk = jnp.where(jnp.logical_or(k == 0, k == FFT_N // 2), 1.0, 2.0)
        wk = jnp.where(k > FFT_N // 2, 0.0, wk) / FFT_N
        inv = wk / (s_ref[0] + s_ref[1])
        g_ref[0] = (are_ref[0] + are_ref[1]) * inv
        g_ref[1] = (aim_ref[0] - aim_ref[1]) * inv

    u = u_ref[...]
    re = jnp.dot(fc_ref[...], u, preferred_element_type=F32)
    im = jnp.dot(fs_ref[...], u, preferred_element_type=F32)
    gre = g_ref[0]
    gim = g_ref[1]
    yre_ref[0] = (re * gre - im * gim).astype(yre_ref.dtype)
    yim_ref[0] = (re * gim + im * gre).astype(yim_ref.dtype)


def dft_forward_mul(fc, fs, u, col, a_re, a_im, abs_sum, order, *, bsz, length):
    c = HY_WIDTH
    out = jax.ShapeDtypeStruct((bsz, FFT_KP, c), BF16)
    return pl.pallas_call(
        _dft_mul_kernel,
        out_shape=(out, out),
        grid=(FFT_KP // FFT_KB, bsz),
        in_specs=[
            pl.BlockSpec((FFT_KB, length), lambda kb, b: (kb, 0)),
            pl.BlockSpec((FFT_KB, length), lambda kb, b: (kb, 0)),
            pl.BlockSpec((length, c), lambda kb, b: (b, col)),
            pl.BlockSpec((2, FFT_KB, c), lambda kb, b: (order, kb, 0)),
            pl.BlockSpec((2, FFT_KB, c), lambda kb, b: (order, kb, 0)),
            pl.BlockSpec((2, 1, c), lambda kb, b: (order, 0, 0)),
        ],
        out_specs=(pl.BlockSpec((1, FFT_KB, c), lambda kb, b: (b, kb, 0)),
                   pl.BlockSpec((1, FFT_KB, c), lambda kb, b: (b, kb, 0))),
        scratch_shapes=[pltpu.VMEM((2, FFT_KB, c), F32)],
        compiler_params=_params(("parallel", "arbitrary")),
        name="dft_forward_mul",
    )(fc, fs, u, a_re, a_im, abs_sum)


def _dft_inv_kernel(ct_ref, st_ref, yre_ref, yim_ref, u_ref, gate_ref, skip_ref, o_ref):
    y = jnp.dot(ct_ref[...], yre_ref[0], preferred_element_type=F32)
    y += jnp.dot(st_ref[...], yim_ref[0], preferred_element_type=F32)
    y += u_ref[...].astype(F32) * skip_ref[0]
    o_ref[...] = (gate_ref[...].astype(F32) * y).astype(o_ref.dtype)


def dft_inverse_gate(fct, fst, yre, yim, u, u_col, gate, gate_col, skip, order, *, bsz, length):
    c = HY_WIDTH
    nblk = length // INV_NB
    return pl.pallas_call(
        _dft_inv_kernel,
        out_shape=jax.ShapeDtypeStruct((bsz * length, c), BF16),
        grid=(nblk, bsz),
        in_specs=[
            pl.BlockSpec((INV_NB, FFT_KP), lambda nb, b: (nb, 0)),
            pl.BlockSpec((INV_NB, FFT_KP), lambda nb, b: (nb, 0)),
            pl.BlockSpec((1, FFT_KP, c), lambda nb, b: (b, 0, 0)),
            pl.BlockSpec((1, FFT_KP, c), lambda nb, b: (b, 0, 0)),
            pl.BlockSpec((INV_NB, c), lambda nb, b: (b * nblk + nb, u_col)),
            pl.BlockSpec((INV_NB, c), lambda nb, b: (b * nblk + nb, gate_col)),
            pl.BlockSpec((1, 1, c), lambda nb, b: (order, 0, 0)),
        ],
        out_specs=pl.BlockSpec((INV_NB, c), lambda nb, b: (b * nblk + nb, 0)),
        compiler_params=_params(("parallel", "parallel")),
        name="dft_inverse_gate",
    )(fct, fst, yre, yim, u, gate, skip)


def hyena_mixer(proj, w_conv, w1, b1, w2, b2, w3, freq, decay, skip, dft, *, bsz, length):
    fc, fs, fct, fst = dft
    vxx = hyena_in_conv(proj, w_conv, bsz=bsz, length=length)
    filt, abs_sum = hyena_filters(length, w1, b1, w2, b2, w3, freq, decay)
    a_re, a_im = dft_filters(fc, fs, filt)
    skip = skip.astype(F32).reshape(HY_ORDER, 1, HY_WIDTH)
    kw = dict(bsz=bsz, length=length)
    yre, yim = dft_forward_mul(fc, fs, vxx, 0, a_re, a_im, abs_sum, 0, **kw)
    z = dft_inverse_gate(fct, fst, yre, yim, vxx, 0, vxx, 1, skip, 0, **kw)
    yre, yim = dft_forward_mul(fc, fs, z, 0, a_re, a_im, abs_sum, 1, **kw)
    return dft_inverse_gate(fct, fst, yre, yim, z, 0, vxx, 2, skip, 1, **kw)


MOE_SUPER = 2 * MOE_BLOCK
MOE_FC = 512
MOE_VMEM_LIMIT = V7X_VMEM_BYTES - 3 * 1024 * 1024


def _moe_kernel(be_ref, nh_ref, x_ref, w1_ref, w3_ref, w2_ref, o_ref, acc_ref):
    i = pl.program_id(0)
    j = pl.program_id(1)
    halves = nh_ref[i]

    @pl.when(j == 0)
    def _():
        acc_ref[...] = jnp.zeros_like(acc_ref)

    def ffn(rows):
        x = x_ref[0:rows, :]
        a = jnp.dot(x, w1_ref[0].astype(BF16), preferred_element_type=F32)
        b = jnp.dot(x, w3_ref[0].astype(BF16), preferred_element_type=F32)
        hid = (a * _sigmoid(a) * b).astype(BF16)
        acc_ref[0:rows, :] += jnp.dot(hid, w2_ref[0].astype(BF16), preferred_element_type=F32)

    @pl.when(halves == 2)
    def _():
        ffn(MOE_SUPER)

    @pl.when(halves == 1)
    def _():
        ffn(MOE_BLOCK)

    @pl.when(j == pl.num_programs(1) - 1)
    def _():
        o_ref[...] = acc_ref[...].astype(o_ref.dtype)


def _moe_kernel_inplace(be_ref, nh_ref, x_ref, w1_ref, w3_ref, w2_ref, prev_ref, o_ref, acc_ref):
    del prev_ref
    _moe_kernel(be_ref, nh_ref, x_ref, w1_ref, w3_ref, w2_ref, o_ref, acc_ref)


def moe_experts(xb, sb_start, n_slots, block_expert, n_halves, w1, w3, w2, yb=None):
    d = xb.shape[1]
    n_super = xb.shape[0] // MOE_SUPER
    fc = MOE_FC
    n_fc = D_FF_EXPERT // fc

    def chunk(i, j, nh):
        return jnp.where(nh[i] > 0, j, n_fc - 1)

    in_specs = [
        pl.BlockSpec((MOE_SUPER, d), lambda i, j, be, nh: (i, 0)),
        pl.BlockSpec((1, d, fc), lambda i, j, be, nh: (be[i], 0, chunk(i, j, nh))),
        pl.BlockSpec((1, d, fc), lambda i, j, be, nh: (be[i], 0, chunk(i, j, nh))),
        pl.BlockSpec((1, fc, d), lambda i, j, be, nh: (be[i], chunk(i, j, nh), 0)),
    ]
    args = [block_expert, n_halves, xb, w1, w3, w2]
    aliases = {}
    if yb is not None:
        in_specs.append(pl.BlockSpec(memory_space=pl.ANY))
        args.append(yb)
        aliases = {len(args) - 1: 0}
    grid_spec = pltpu.PrefetchScalarGridSpec(
        num_scalar_prefetch=2,
        grid=(n_super, n_fc),
        in_specs=in_specs,
        out_specs=pl.BlockSpec((MOE_SUPER, d), lambda i, j, be, nh: (sb_start + i, 0)),
        scratch_shapes=[pltpu.VMEM((MOE_SUPER, d), F32)],
    )
    return pl.pallas_call(
        _moe_kernel if yb is None else _moe_kernel_inplace,
        out_shape=jax.ShapeDtypeStruct((n_slots, d), BF16),
        grid_spec=grid_spec,
        input_output_aliases=aliases,
        compiler_params=_params(("parallel", "arbitrary"), MOE_VMEM_LIMIT),
        name="moe_experts",
    )(*args)


COMBINE_ROWS = 256


def _moe_combine_kernel(h_ref, y0_ref, y1_ref, g0_ref, g1_ref, gn_ref, o_ref, *, final):
    rows = o_ref.shape[0]
    gn = gn_ref[...]

    def chunk(c, carry):
        r0 = pl.multiple_of(c * NORM_CHUNK, NORM_CHUNK)
        sl = pl.ds(r0, NORM_CHUNK)
        h = h_ref[0, sl, :] if final else h_ref[sl, :]
        y = h + g0_ref[sl, :] * y0_ref[sl, :].astype(F32) + g1_ref[sl, :] * y1_ref[sl, :].astype(F32)
        if final:
            ms = jnp.mean(y * y, axis=-1, keepdims=True)
            y = y * lax.rsqrt(ms + NORM_EPS) * gn
        o_ref[sl, :] = y.astype(o_ref.dtype)
        return carry

    lax.fori_loop(0, rows // NORM_CHUNK, chunk, 0, unroll=4)


def moe_combine(h, y0, y1, g0, g1, *, tm=ROW_TILE):
    t, d = h.shape
    row = lambda w: pl.BlockSpec((tm, w), lambda i: (i, 0))
    return pl.pallas_call(
        functools.partial(_moe_combine_kernel, final=False),
        out_shape=jax.ShapeDtypeStruct((t, d), F32),
        grid=(t // tm,),
        in_specs=[row(d), row(d), row(d), row(1), row(1), pl.BlockSpec((1, d), lambda i: (0, 0))],
        out_specs=row(d),
        compiler_params=_params(("parallel",)),
        name="moe_combine",
    )(h, y0, y1, g0, g1, jnp.ones((1, d), F32))


def moe_combine_final(h, y0, y1, g0, g1, g_final, *, bsz, length):
    d = h.shape[1]
    seq = length - N_META
    r = COMBINE_ROWS
    nr = seq // r
    row = lambda w: pl.BlockSpec((r, w), lambda b, i: (b * nr + i, 0))
    h_spec = pl.BlockSpec((pl.Element(1), pl.Element(r), pl.Element(d)),
                          lambda b, i: (b, pl.multiple_of(N_META + i * r, N_META), 0))
    return pl.pallas_call(
        functools.partial(_moe_combine_kernel, final=True),
        out_shape=jax.ShapeDtypeStruct((bsz, seq, d), F32),
        grid=(bsz, nr),
        in_specs=[h_spec, row(d), row(d), row(1), row(1), pl.BlockSpec((1, d), lambda b, i: (0, 0))],
        out_specs=pl.BlockSpec((None, r, d), lambda b, i: (b, i, 0)),
        compiler_params=_params(("parallel", "parallel")),
        name="moe_combine_final",
    )(h.reshape(bsz, length, d), y0, y1, g0, g1, g_final.astype(F32).reshape(1, d))


SCAN_CHUNK = 256
MOE_CALL_SPLIT = (4, 16, None)


def _expert_ranks(expert):
    n = expert.shape[0]
    onehot = (expert[:, None] == jnp.arange(N_EXPERTS, dtype=jnp.int32)[None, :]).astype(F32)
    chunks = onehot.reshape(n // SCAN_CHUNK, SCAN_CHUNK, N_EXPERTS)
    tri = jnp.tril(jnp.ones((SCAN_CHUNK, SCAN_CHUNK), F32), k=-1)
    within = jnp.einsum('ij,cjk->cik', tri, chunks)
    totals = jnp.sum(chunks, axis=1)
    before = jnp.cumsum(totals, axis=0) - totals
    rank = jnp.sum((within + before[:, None, :]) * chunks, axis=-1).reshape(n)
    return rank.astype(jnp.int32), jnp.sum(totals, axis=0).astype(jnp.int32)


def moe_swiglu(h, g_norm, w_router, w1, w3, w2, g_final, *, bsz, length):
    n_tok, d = h.shape
    n_assign = n_tok * TOP_K
    router_pad = 128
    wr = jnp.pad(w_router.astype(F32), ((0, 0), (0, router_pad - N_EXPERTS)))
    logits, hn = norm_router(h, g_norm, wr)
    top_val, top_idx = lax.top_k(logits[:, :N_EXPERTS], TOP_K)
    gate = jax.nn.softmax(top_val, axis=-1)
    expert = top_idx.astype(jnp.int32).reshape(n_assign)
    rank, counts = _expert_ranks(expert)
    padded = (counts + MOE_SUPER - 1) // MOE_SUPER * MOE_SUPER
    pad_end = jnp.cumsum(padded)
    pad_start = pad_end - padded
    slot = (pad_start[expert] + rank).reshape(n_tok, TOP_K)
    n_super = -(-n_assign // MOE_SUPER) + N_EXPERTS
    n_slots = n_super * MOE_SUPER
    token = jnp.repeat(jnp.arange(n_tok, dtype=jnp.int32), TOP_K)
    slot_token = jnp.zeros((n_slots,), jnp.int32).at[slot.reshape(n_assign)].set(token)
    n_used = pad_end[-1] // MOE_SUPER
    sb = jnp.arange(n_super, dtype=jnp.int32)
    block_expert = jnp.minimum(jnp.searchsorted(pad_end, jnp.minimum(sb, n_used - 1) * MOE_SUPER, side='right'),
                               N_EXPERTS - 1).astype(jnp.int32)
    rows_here = jnp.clip(counts[block_expert] - (sb * MOE_SUPER - pad_start[block_expert]), 0, MOE_SUPER)
    n_halves = jnp.where(sb < n_used, (rows_here + MOE_BLOCK - 1) // MOE_BLOCK, 0).astype(jnp.int32)
    yb = None
    sb0 = 0
    for part in MOE_CALL_SPLIT:
        sb1 = n_super if part is None else sb0 + part
        rows = slice(sb0 * MOE_SUPER, sb1 * MOE_SUPER)
        yb = moe_experts(hn[slot_token[rows]], sb0, n_slots, block_expert[sb0:sb1], n_halves[sb0:sb1],
                         w1, w3, w2, yb)
        sb0 = sb1
    if g_final is None:
        return moe_combine(h, yb[slot[:, 0]], yb[slot[:, 1]], gate[:, 0:1], gate[:, 1:2])
    keep = lambda a: a.reshape(bsz, length, -1)[:, N_META:].reshape(bsz * (length - N_META), -1)
    slot, gate = keep(slot), keep(gate)
    return moe_combine_final(h, yb[slot[:, 0]], yb[slot[:, 1]], gate[:, 0:1], gate[:, 1:2], g_final,
                             bsz=bsz, length=length)


def kernel(x, meta_tokens, norm_mix, w_in, na_rpb, na_meta_bias, sc_conv, hy_conv, hy_w1, hy_b1, hy_w2, hy_b2,
           hy_w3, hy_freq, hy_decay, hy_skip, w_na_out, w_sc_out, w_hy_out, w_o, norm_ffn, ffn_w1, ffn_w3,
           ffn_w2, router_w, moe_w1, moe_w3, moe_w2, norm_final):
    bsz, seq, d = x.shape
    length = N_META + seq
    depth = w_in.shape[0]
    meta = jnp.broadcast_to(meta_tokens.astype(x.dtype)[None], (bsz, N_META, d))
    h = jnp.concatenate([meta, x], axis=1).reshape(bsz * length, d)
    dft = _dft_matrices(length)
    kw = dict(bsz=bsz, length=length)
    w_in, w_na_out, w_sc_out, w_hy_out, w_o, ffn_w1, ffn_w3, ffn_w2 = map(
        to_bf16, (w_in, w_na_out, w_sc_out, w_hy_out, w_o, ffn_w1, ffn_w3, ffn_w2))
    for layer in range(depth):
        proj = norm_matmul(h, norm_mix[layer], w_in[layer], tn=2048)
        y_na = neighbourhood_attention(proj, na_rpb[layer], na_meta_bias[layer], **kw)
        y_sc = short_conv_mixer(proj, sc_conv[layer], **kw)
        y_hy = hyena_mixer(proj, hy_conv[layer], hy_w1[layer], hy_b1[layer], hy_w2[layer], hy_b2[layer],
                           hy_w3[layer], hy_freq[layer], hy_decay[layer], hy_skip[layer], dft, **kw)
        h = mixer_out(h, y_na, y_sc, y_hy, proj, w_na_out[layer], w_sc_out[layer], w_hy_out[layer], w_o[layer])
        j = layer // 2
        if layer % 2 == 0:
            h = dense_ffn(h, norm_ffn[layer], ffn_w1[j], ffn_w3[j], ffn_w2[j])
        else:
            g_final = norm_final if layer == depth - 1 else None
            h = moe_swiglu(h, norm_ffn[layer], router_w[j], moe_w1[j], moe_w3[j], moe_w2[j], g_final, **kw)
    if depth % 2 == 0:
        return h
    out = rms_norm_rows(h, norm_final, out_dtype=x.dtype)
    return out.reshape(bsz, length, d)[:, N_META:]
```

```python
import functools
import math

import numpy as np
import jax
import jax.numpy as jnp
from jax import lax
from jax.experimental import pallas as pl
from jax.experimental.pallas import tpu as pltpu

D_MODEL = 2048
N_META = 16
GRID_W = 64
GRID_ROWS = 32
NA_HEADS = 16
NA_HEAD_DIM = 64
NA_WIDTH = NA_HEADS * NA_HEAD_DIM
WIN_ROWS = 8
WIN_COLS = 16
SC_WIDTH = 512
HY_WIDTH = 512
HY_ORDER = 2
HY_EMB = 33
HY_HIDDEN = 64
HY_MOD_SHIFT = 0.05
PROJ_WIDTH = 3 * NA_WIDTH + 3 * SC_WIDTH + 3 * HY_WIDTH + 3 * D_MODEL
D_FF = 5632
N_EXPERTS = 8
TOP_K = 2
D_FF_EXPERT = 7168
MOE_BLOCK = 512
NORM_EPS = 1e-6
NEG_INF = -1e30

COL_Q, COL_K, COL_V = 0, NA_WIDTH, 2 * NA_WIDTH
COL_SCB = 3 * NA_WIDTH
COL_SCC = COL_SCB + SC_WIDTH
COL_SCU = COL_SCC + SC_WIDTH
COL_HYU = COL_SCU + SC_WIDTH
COL_GNA = COL_HYU + 3 * HY_WIDTH
COL_GSC = COL_GNA + D_MODEL
COL_GHY = COL_GSC + D_MODEL

V7X_VMEM_BYTES = 64 * 1024 * 1024
VMEM_LIMIT = V7X_VMEM_BYTES - 8 * 1024 * 1024

ROW_TILE = 768
NORM_CHUNK = 16
FFT_N = 2 * (N_META + GRID_ROWS * GRID_W)
FFT_KP = 2080
FFT_KB = 1040
INV_NB = 688

F32 = jnp.float32
BF16 = jnp.bfloat16


def _params(sem, vmem_limit=VMEM_LIMIT):
    return pltpu.CompilerParams(dimension_semantics=sem, vmem_limit_bytes=vmem_limit)


def _sigmoid(a):
    return 0.5 * jnp.tanh(0.5 * a) + 0.5


CAST_BLOCK_BYTES = 8 * 1024 * 1024


def _cast_kernel(x_ref, o_ref):
    o_ref[...] = x_ref[...].astype(o_ref.dtype)


def to_bf16(w):
    cols = w.shape[-1]
    w2 = w.reshape(-1, cols)
    rows = w2.shape[0]
    tr = min(rows, CAST_BLOCK_BYTES // (4 * cols)) // 16 * 16
    while rows % tr:
        tr -= 16
    out = pl.pallas_call(
        _cast_kernel,
        out_shape=jax.ShapeDtypeStruct((rows, cols), BF16),
        grid=(rows // tr,),
        in_specs=[pl.BlockSpec((tr, cols), lambda i: (i, 0))],
        out_specs=pl.BlockSpec((tr, cols), lambda i: (i, 0)),
        compiler_params=_params(("parallel",)),
        name="to_bf16",
    )(w2)
    return out.reshape(w.shape)


def _rms_norm_rows(x_ref, g_ref, xn_ref):
    rows = x_ref.shape[0]
    g = g_ref[...]

    def chunk(c, carry):
        r0 = pl.multiple_of(c * NORM_CHUNK, NORM_CHUNK)
        x = x_ref[pl.ds(r0, NORM_CHUNK), :]
        ms = jnp.mean(x * x, axis=-1, keepdims=True)
        xn_ref[pl.ds(r0, NORM_CHUNK), :] = (x * lax.rsqrt(ms + NORM_EPS) * g).astype(xn_ref.dtype)
        return carry

    lax.fori_loop(0, rows // NORM_CHUNK, chunk, 0, unroll=4)


def _norm_mm_kernel(x_ref, g_ref, w_ref, o_ref, xn_ref):
    @pl.when(pl.program_id(1) == 0)
    def _():
        _rms_norm_rows(x_ref, g_ref, xn_ref)

    o_ref[...] = jnp.dot(xn_ref[...], w_ref[...], preferred_element_type=F32).astype(o_ref.dtype)


def norm_matmul(x, g, w, *, tn, out_dtype=BF16, tm=ROW_TILE):
    t, k = x.shape
    n = w.shape[1]
    return pl.pallas_call(
        _norm_mm_kernel,
        out_shape=jax.ShapeDtypeStruct((t, n), out_dtype),
        grid=(t // tm, n // tn),
        in_specs=[
            pl.BlockSpec((tm, k), lambda i, j: (i, 0)),
            pl.BlockSpec((1, k), lambda i, j: (0, 0)),
            pl.BlockSpec((k, tn), lambda i, j: (0, j)),
        ],
        out_specs=pl.BlockSpec((tm, tn), lambda i, j: (i, j)),
        scratch_shapes=[pltpu.VMEM((tm, k), BF16)],
        compiler_params=_params(("parallel", "arbitrary")),
        name="norm_matmul",
    )(x, g.reshape(1, k), w)


def _norm_router_kernel(x_ref, g_ref, w_ref, o_ref, xb_ref, xn_ref):
    _rms_norm_rows(x_ref, g_ref, xn_ref)
    xn = xn_ref[...]
    xb_ref[...] = xn.astype(xb_ref.dtype)
    o_ref[...] = jnp.dot(xn, w_ref[...], preferred_element_type=F32, precision=lax.Precision.HIGHEST)


def norm_router(x, g, w, *, tm=ROW_TILE):
    t, k = x.shape
    n = w.shape[1]
    return pl.pallas_call(
        _norm_router_kernel,
        out_shape=(jax.ShapeDtypeStruct((t, n), F32), jax.ShapeDtypeStruct((t, k), BF16)),
        grid=(t // tm,),
        in_specs=[
            pl.BlockSpec((tm, k), lambda i: (i, 0)),
            pl.BlockSpec((1, k), lambda i: (0, 0)),
            pl.BlockSpec((k, n), lambda i: (0, 0)),
        ],
        out_specs=(pl.BlockSpec((tm, n), lambda i: (i, 0)), pl.BlockSpec((tm, k), lambda i: (i, 0))),
        scratch_shapes=[pltpu.VMEM((tm, k), F32)],
        compiler_params=_params(("parallel",)),
        name="norm_router",
    )(x, g.reshape(1, k), w)


def _norm_kernel(x_ref, g_ref, o_ref):
    _rms_norm_rows(x_ref, g_ref, o_ref)


def rms_norm_rows(x, g, *, out_dtype, tm=ROW_TILE):
    t, k = x.shape
    return pl.pallas_call(
        _norm_kernel,
        out_shape=jax.ShapeDtypeStruct((t, k), out_dtype),
        grid=(t // tm,),
        in_specs=[pl.BlockSpec((tm, k), lambda i: (i, 0)), pl.BlockSpec((1, k), lambda i: (0, 0))],
        out_specs=pl.BlockSpec((tm, k), lambda i: (i, 0)),
        compiler_params=_params(("parallel",)),
        name="rms_norm",
    )(x, g.reshape(1, k))


def _dense_ffn_kernel(x_ref, g_ref, w1_ref, w3_ref, w2_ref, o_ref, xn_ref):
    @pl.when(pl.program_id(1) == 0)
    def _():
        _rms_norm_rows(x_ref, g_ref, xn_ref)
        o_ref[...] = x_ref[...]

    xn = xn_ref[...]
    a = jnp.dot(xn, w1_ref[...], preferred_element_type=F32)
    b = jnp.dot(xn, w3_ref[...], preferred_element_type=F32)
    hid = (a * _sigmoid(a) * b).astype(BF16)
    o_ref[...] += jnp.dot(hid, w2_ref[...], preferred_element_type=F32)


def dense_ffn(x, g, w1, w3, w2, *, fc=512, tm=ROW_TILE):
    t, d = x.shape
    f = w1.shape[1]
    return pl.pallas_call(
        _dense_ffn_kernel,
        out_shape=jax.ShapeDtypeStruct((t, d), F32),
        grid=(t // tm, f // fc),
        in_specs=[
            pl.BlockSpec((tm, d), lambda i, j: (i, 0)),
            pl.BlockSpec((1, d), lambda i, j: (0, 0)),
            pl.BlockSpec((d, fc), lambda i, j: (0, j)),
            pl.BlockSpec((d, fc), lambda i, j: (0, j)),
            pl.BlockSpec((fc, d), lambda i, j: (j, 0)),
        ],
        out_specs=pl.BlockSpec((tm, d), lambda i, j: (i, 0)),
        scratch_shapes=[pltpu.VMEM((tm, d), BF16)],
        compiler_params=_params(("parallel", "arbitrary")),
        name="dense_ffn",
    )(x, g.reshape(1, d), w1, w3, w2)


def _mixer_out_kernel(res_ref, yna_ref, ysc_ref, yhy_ref, gna_ref, gsc_ref, ghy_ref, wna_ref, wsc_ref, why_ref,
                      wo_ref, o_ref):
    @pl.when(pl.program_id(1) == 0)
    def _():
        o_ref[...] = res_ref[...]

    m = _sigmoid(gna_ref[...].astype(F32)) * jnp.dot(yna_ref[...], wna_ref[...], preferred_element_type=F32)
    m += _sigmoid(gsc_ref[...].astype(F32)) * jnp.dot(ysc_ref[...], wsc_ref[...], preferred_element_type=F32)
    m += _sigmoid(ghy_ref[...].astype(F32)) * jnp.dot(yhy_ref[...], why_ref[...], preferred_element_type=F32)
    o_ref[...] += jnp.dot(m.astype(BF16), wo_ref[...], preferred_element_type=F32)


MIXER_ROWS = 256


def mixer_out(res, y_na, y_sc, y_hy, proj, w_na, w_sc, w_hy, w_o, *, cw=D_MODEL, tm=MIXER_ROWS):
    t, d = res.shape
    return pl.pallas_call(
        _mixer_out_kernel,
        out_shape=jax.ShapeDtypeStruct((t, d), F32),
        grid=(t // tm, d // cw),
        in_specs=[
            pl.BlockSpec((tm, d), lambda i, c: (i, 0)),
            pl.BlockSpec((tm, NA_WIDTH), lambda i, c: (i, 0)),
            pl.BlockSpec((tm, SC_WIDTH), lambda i, c: (i, 0)),
            pl.BlockSpec((tm, HY_WIDTH), lambda i, c: (i, 0)),
            pl.BlockSpec((tm, cw), lambda i, c: (i, COL_GNA // cw + c)),
            pl.BlockSpec((tm, cw), lambda i, c: (i, COL_GSC // cw + c)),
            pl.BlockSpec((tm, cw), lambda i, c: (i, COL_GHY // cw + c)),
            pl.BlockSpec((NA_WIDTH, cw), lambda i, c: (0, c)),
            pl.BlockSpec((SC_WIDTH, cw), lambda i, c: (0, c)),
            pl.BlockSpec((HY_WIDTH, cw), lambda i, c: (0, c)),
            pl.BlockSpec((cw, d), lambda i, c: (c, 0)),
        ],
        out_specs=pl.BlockSpec((tm, d), lambda i, c: (i, 0)),
        compiler_params=_params(("parallel", "arbitrary")),
        name="mixer_out",
    )(res, y_na, y_sc, y_hy, proj, proj, proj, w_na, w_sc, w_hy, w_o)


def _na_bias_table(rpb):
    qc = np.arange(GRID_W)[:, None]
    kc = np.arange(GRID_W)[None, :]
    win_start = np.clip(qc - WIN_COLS // 2, 0, GRID_W - WIN_COLS)
    inside = (kc >= win_start) & (kc < win_start + WIN_COLS)
    dc = np.clip(kc - qc, -(WIN_COLS - 1), WIN_COLS - 1) + WIN_COLS - 1
    dr = np.arange(WIN_ROWS)[None, :] - np.arange(WIN_ROWS)[:, None] + WIN_ROWS - 1
    onehot = np.zeros((2 * WIN_COLS - 1, GRID_W * GRID_W), np.float32)
    onehot[dc.reshape(-1), np.arange(GRID_W * GRID_W)] = 1.0
    tbl = jnp.einsum('htid,dj->htij', rpb.astype(F32)[:, dr], jnp.asarray(onehot),
                     precision=lax.Precision.HIGHEST)
    tbl = tbl.reshape(NA_HEADS, WIN_ROWS, WIN_ROWS, GRID_W, GRID_W)
    tbl = jnp.where(jnp.asarray(inside)[None, None, None], tbl, NEG_INF)
    tbl = tbl.transpose(0, 1, 3, 2, 4)
    return tbl.reshape(NA_HEADS, WIN_ROWS, GRID_W, WIN_ROWS * GRID_W)


NA_ROW_GROUP = 4


def _na_kernel(q_ref, k_ref, v_ref, tbl_ref, mb_ref, o_ref):
    scale = NA_HEAD_DIM ** -0.5
    lane = lax.broadcasted_iota(jnp.int32, (1, 2 * NA_HEAD_DIM), 1)
    first = lane < NA_HEAD_DIM
    nt = (((1,), (1,)), ((), ()))
    k_meta = k_ref[0:N_META, :]
    v_meta = v_ref[0:N_META, :]
    mb_grid = mb_ref[0, 0]
    band = WIN_ROWS * GRID_W

    def stack_heads(q):
        q = q * scale
        zero = jnp.zeros_like(q)
        return jnp.concatenate([jnp.where(first, q, zero), jnp.where(first, zero, q)], axis=0)

    def unstack_heads(out, den, n):
        out = out / den
        return jnp.where(first, out[:n], out[n:])

    def softmax_parts(s_grid, s_meta):
        m = jnp.max(s_meta, axis=-1, keepdims=True)
        if s_grid is not None:
            m = jnp.maximum(m, jnp.max(s_grid, axis=-1, keepdims=True))
        p_meta = jnp.exp(s_meta - m)
        den = jnp.sum(p_meta, axis=-1, keepdims=True)
        p_grid = None
        if s_grid is not None:
            p_grid = jnp.exp(s_grid - m)
            den = den + jnp.sum(p_grid, axis=-1, keepdims=True)
            p_grid = p_grid.astype(BF16)
        return p_grid, p_meta.astype(BF16), den

    def attend(rows, with_meta_queries):
        geo = []
        for r in rows:
            rs = min(max(r - WIN_ROWS // 2, 0), GRID_ROWS - WIN_ROWS)
            geo.append((N_META + r * GRID_W, N_META + rs * GRID_W, r - rs))
        scores = []
        if with_meta_queries:
            q2 = stack_heads(q_ref[0:N_META, :])
            s_meta = lax.dot_general(q2, k_meta, nt, preferred_element_type=F32) + mb_ref[0, 1, 0:2 * N_META, :]
            scores.append((None, s_meta))
        for q0, k0, typ in geo:
            q2 = stack_heads(q_ref[q0:q0 + GRID_W, :])
            s_grid = lax.dot_general(q2, k_ref[k0:k0 + band, :], nt, preferred_element_type=F32) + tbl_ref[0, typ]
            s_meta = lax.dot_general(q2, k_meta, nt, preferred_element_type=F32) + mb_grid
            scores.append((s_grid, s_meta))
        probs = [softmax_parts(*sc) for sc in scores]
        if with_meta_queries:
            _, p_meta, den = probs.pop(0)
            out = jnp.dot(p_meta, v_meta, preferred_element_type=F32)
            o_ref[0:N_META, :] = unstack_heads(out, den, N_META).astype(o_ref.dtype)
        for (q0, k0, typ), (p_grid, p_meta, den) in zip(geo, probs):
            out = jnp.dot(p_grid, v_ref[k0:k0 + band, :], preferred_element_type=F32)
            out += jnp.dot(p_meta, v_meta, preferred_element_type=F32)
            o_ref[q0:q0 + GRID_W, :] = unstack_heads(out, den, GRID_W).astype(o_ref.dtype)

    for g0 in range(0, GRID_ROWS, NA_ROW_GROUP):
        attend(range(g0, g0 + NA_ROW_GROUP), with_meta_queries=(g0 == 0))


def neighbourhood_attention(proj, rpb, meta_bias, *, bsz, length):
    hp = 2 * NA_HEAD_DIM
    n_pairs = NA_HEADS // 2
    band = WIN_ROWS * GRID_W
    tbl = _na_bias_table(rpb).reshape(n_pairs, 2, WIN_ROWS, GRID_W, band).transpose(0, 2, 1, 3, 4)
    tbl = tbl.reshape(n_pairs, WIN_ROWS, 2 * GRID_W, band)
    mb = meta_bias.astype(F32).reshape(n_pairs, 2, 1, N_META)
    mb_grid = jnp.broadcast_to(mb, (n_pairs, 2, GRID_W, N_META)).reshape(n_pairs, 2 * GRID_W, N_META)
    mb_meta = jnp.broadcast_to(mb, (n_pairs, 2, N_META, N_META)).reshape(n_pairs, 2 * N_META, N_META)
    mb_meta = jnp.pad(mb_meta, ((0, 0), (0, 2 * GRID_W - 2 * N_META), (0, 0)))
    mb2 = jnp.stack([mb_grid, mb_meta], axis=1)
    return pl.pallas_call(
        _na_kernel,
        out_shape=jax.ShapeDtypeStruct((bsz * length, NA_WIDTH), BF16),
        grid=(n_pairs, bsz),
        in_specs=[
            pl.BlockSpec((length, hp), lambda h, b: (b, COL_Q // hp + h)),
            pl.BlockSpec((length, hp), lambda h, b: (b, COL_K // hp + h)),
            pl.BlockSpec((length, hp), lambda h, b: (b, COL_V // hp + h)),
            pl.BlockSpec((1, WIN_ROWS, 2 * GRID_W, band), lambda h, b: (h, 0, 0, 0)),
            pl.BlockSpec((1, 2, 2 * GRID_W, N_META), lambda h, b: (h, 0, 0, 0)),
        ],
        out_specs=pl.BlockSpec((length, hp), lambda h, b: (b, h)),
        compiler_params=_params(("parallel", "parallel")),
        name="neighbourhood_attention",
    )(proj, proj, proj, tbl, mb2)


def _dwconv3(u, w):
    length = u.shape[0]
    row = lax.broadcasted_iota(jnp.int32, u.shape, 0)
    prev = jnp.where(row == 0, 0.0, pltpu.roll(u, 1, 0))
    nxt = jnp.where(row == length - 1, 0.0, pltpu.roll(u, length - 1, 0))
    return prev * w[0:1] + u * w[1:2] + nxt * w[2:3]


def _gated_conv_kernel(b_ref, c_ref, u_ref, w_ref, o_ref):
    u = c_ref[...].astype(F32) * u_ref[...].astype(F32)
    o_ref[...] = (b_ref[...].astype(F32) * _dwconv3(u, w_ref[...])).astype(o_ref.dtype)


def _plain_conv_kernel(u_ref, w_ref, o_ref):
    o_ref[...] = _dwconv3(u_ref[...].astype(F32), w_ref[...]).astype(o_ref.dtype)


CONV_CB = 512


def short_conv_mixer(proj, w_conv, *, bsz, length):
    cb = CONV_CB
    return pl.pallas_call(
        _gated_conv_kernel,
        out_shape=jax.ShapeDtypeStruct((bsz * length, SC_WIDTH), BF16),
        grid=(bsz, SC_WIDTH // cb),
        in_specs=[
            pl.BlockSpec((length, cb), lambda b, c: (b, COL_SCB // cb + c)),
            pl.BlockSpec((length, cb), lambda b, c: (b, COL_SCC // cb + c)),
            pl.BlockSpec((length, cb), lambda b, c: (b, COL_SCU // cb + c)),
            pl.BlockSpec((3, cb), lambda b, c: (0, c)),
        ],
        out_specs=pl.BlockSpec((length, cb), lambda b, c: (b, c)),
        compiler_params=_params(("parallel", "parallel")),
        name="short_conv_mixer",
    )(proj, proj, proj, w_conv.astype(F32))


def hyena_in_conv(proj, w_conv, *, bsz, length):
    cb = CONV_CB
    return pl.pallas_call(
        _plain_conv_kernel,
        out_shape=jax.ShapeDtypeStruct((bsz * length, 3 * HY_WIDTH), BF16),
        grid=(bsz, 3 * HY_WIDTH // cb),
        in_specs=[
            pl.BlockSpec((length, cb), lambda b, c: (b, COL_HYU // cb + c)),
            pl.BlockSpec((3, cb), lambda b, c: (0, c)),
        ],
        out_specs=pl.BlockSpec((length, cb), lambda b, c: (b, c)),
        compiler_params=_params(("parallel", "parallel")),
        name="hyena_in_conv",
    )(proj, w_conv.astype(F32))


def _filter_kernel(z_ref, w1_ref, b1_ref, w2_ref, b2_ref, w3_ref, fr_ref, dec_ref, f_ref, s_ref):
    hi = lax.Precision.HIGHEST
    z = z_ref[...]
    fr = fr_ref[...]
    h = jnp.sin(fr * (jnp.dot(z, w1_ref[...], preferred_element_type=F32, precision=hi) + b1_ref[...]))
    h = jnp.sin(fr * (jnp.dot(h, w2_ref[...], preferred_element_type=F32, precision=hi) + b2_ref[...]))
    h = jnp.dot(h, w3_ref[...], preferred_element_type=F32, precision=hi)
    t = z[:, 0:1]
    f = h * (jnp.exp(-t * jnp.abs(dec_ref[0])) + HY_MOD_SHIFT)
    row = lax.broadcasted_iota(jnp.int32, f.shape, 0)
    backward = pl.program_id(0) % 2 == 1
    f = jnp.where(jnp.logical_and(row == 0, backward), 0.0, f)
    f_ref[0] = f.astype(f_ref.dtype)
    s_ref[0] = jnp.sum(jnp.abs(f), axis=0, keepdims=True)


def hyena_filters(length, w1, b1, w2, b2, w3, freq, decay):
    bands = (HY_EMB - 1) // 2
    t = jnp.linspace(0.0, 1.0, length, dtype=F32)[:, None]
    ang = (2.0 * math.pi / length) * jnp.arange(length, dtype=F32)[:, None]
    fb = jnp.linspace(1e-4, bands - 1, bands, dtype=F32)[None, :]
    z = jnp.concatenate([t, jnp.cos(fb * ang), -jnp.sin(fb * ang)], axis=-1)
    emb = 128
    z = jnp.pad(z, ((0, 0), (0, emb - HY_EMB)))
    w1p = jnp.pad(w1.astype(F32), ((0, emb - HY_EMB), (0, 0)))
    n_slab = HY_ORDER * 2
    row = lambda a: a.astype(F32).reshape(1, -1)
    return pl.pallas_call(
        _filter_kernel,
        out_shape=(jax.ShapeDtypeStruct((n_slab, length, HY_WIDTH), BF16),
                   jax.ShapeDtypeStruct((n_slab, 1, HY_WIDTH), F32)),
        grid=(n_slab,),
        in_specs=[
            pl.BlockSpec((length, emb), lambda j: (0, 0)),
            pl.BlockSpec((emb, HY_HIDDEN), lambda j: (0, 0)),
            pl.BlockSpec((1, HY_HIDDEN), lambda j: (0, 0)),
            pl.BlockSpec((HY_HIDDEN, HY_HIDDEN), lambda j: (0, 0)),
            pl.BlockSpec((1, HY_HIDDEN), lambda j: (0, 0)),
            pl.BlockSpec((HY_HIDDEN, HY_WIDTH), lambda j: (0, j)),
            pl.BlockSpec((1, HY_HIDDEN), lambda j: (0, 0)),
            pl.BlockSpec((1, 1, HY_WIDTH), lambda j: (j, 0, 0)),
        ],
        out_specs=(pl.BlockSpec((1, length, HY_WIDTH), lambda j: (j, 0, 0)),
                   pl.BlockSpec((1, 1, HY_WIDTH), lambda j: (j, 0, 0))),
        compiler_params=_params(("parallel",)),
        name="hyena_filters",
    )(z, w1p, row(b1), w2.astype(F32), row(b2), w3.astype(F32), row(freq),
      decay.astype(F32).reshape(n_slab, 1, HY_WIDTH))


def _dft_matrices(length):
    k = jnp.arange(FFT_KP, dtype=jnp.int32)[:, None]
    n = jnp.arange(length, dtype=jnp.int32)[None, :]
    theta = (2.0 * math.pi / FFT_N) * ((k * n) % FFT_N).astype(F32)
    valid = k <= FFT_N // 2
    fc = jnp.where(valid, jnp.cos(theta), 0.0).astype(BF16)
    fs = jnp.where(valid, -jnp.sin(theta), 0.0).astype(BF16)
    fc, fs = lax.optimization_barrier((fc, fs))
    return fc, fs, fc.T, fs.T


def _dft_raw_kernel(fc_ref, fs_ref, u_ref, re_ref, im_ref):
    u = u_ref[0]
    re_ref[0] = jnp.dot(fc_ref[...], u, preferred_element_type=F32)
    im_ref[0] = jnp.dot(fs_ref[...], u, preferred_element_type=F32)


def dft_filters(fc, fs, filt):
    n_slab, length, c = filt.shape
    out = jax.ShapeDtypeStruct((n_slab, FFT_KP, c), F32)
    return pl.pallas_call(
        _dft_raw_kernel,
        out_shape=(out, out),
        grid=(FFT_KP // FFT_KB, n_slab),
        in_specs=[
            pl.BlockSpec((FFT_KB, length), lambda kb, s: (kb, 0)),
            pl.BlockSpec((FFT_KB, length), lambda kb, s: (kb, 0)),
            pl.BlockSpec((1, length, c), lambda kb, s: (s, 0, 0)),
        ],
        out_specs=(pl.BlockSpec((1, FFT_KB, c), lambda kb, s: (s, kb, 0)),
                   pl.BlockSpec((1, FFT_KB, c), lambda kb, s: (s, kb, 0))),
        compiler_params=_params(("parallel", "parallel")),
        name="dft_filters",
    )(fc, fs, filt)


def _dft_mul_kernel(fc_ref, fs_ref, u_ref, are_ref, aim_ref, s_ref, yre_ref, yim_ref, g_ref):
    @pl.when(pl.program_id(1) == 0)
    def _():
        k = pl.program_id(0) * FFT_KB + lax.broadcasted_iota(jnp.int32, (FFT_KB, 1), 0)
        wk = jnp.where(jnp.logical_or(k == 0, k == FFT_N // 2), 1.0, 2.0)
        wk = jnp.where(k > FFT_N // 2, 0.0, wk) / FFT_N
        inv = wk / (s_ref[0] + s_ref[1])
        g_ref[0] = (are_ref[0] + are_ref[1]) * inv
        g_ref[1] = (aim_ref[0] - aim_ref[1]) * inv

    u = u_ref[...]
    re = jnp.dot(fc_ref[...], u, preferred_element_type=F32)
    im = jnp.dot(fs_ref[...], u, preferred_element_type=F32)
    gre = g_ref[0]
    gim = g_ref[1]
    yre_ref[0] = (re * gre - im * gim).astype(yre_ref.dtype)
    yim_ref[0] = (re * gim + im * gre).astype(yim_ref.dtype)


def dft_forward_mul(fc, fs, u, col, a_re, a_im, abs_sum, order, *, bsz, length):
    c = HY_WIDTH
    out = jax.ShapeDtypeStruct((bsz, FFT_KP, c), BF16)
    return pl.pallas_call(
        _dft_mul_kernel,
        out_shape=(out, out),
        grid=(FFT_KP // FFT_KB, bsz),
        in_specs=[
            pl.BlockSpec((FFT_KB, length), lambda kb, b: (kb, 0)),
            pl.BlockSpec((FFT_KB, length), lambda kb, b: (kb, 0)),
            pl.BlockSpec((length, c), lambda kb, b: (b, col)),
            pl.BlockSpec((2, FFT_KB, c), lambda kb, b: (order, kb, 0)),
            pl.BlockSpec((2, FFT_KB, c), lambda kb, b: (order, kb, 0)),
            pl.BlockSpec((2, 1, c), lambda kb, b: (order, 0, 0)),
        ],
        out_specs=(pl.BlockSpec((1, FFT_KB, c), lambda kb, b: (b, kb, 0)),
                   pl.BlockSpec((1, FFT_KB, c), lambda kb, b: (b, kb, 0))),
        scratch_shapes=[pltpu.VMEM((2, FFT_KB, c), F32)],
        compiler_params=_params(("parallel", "arbitrary")),
        name="dft_forward_mul",
    )(fc, fs, u, a_re, a_im, abs_sum)


def _dft_inv_kernel(ct_ref, st_ref, yre_ref, yim_ref, u_ref, gate_ref, skip_ref, o_ref):
    y = jnp.dot(ct_ref[...], yre_ref[0], preferred_element_type=F32)
    y += jnp.dot(st_ref[...], yim_ref[0], preferred_element_type=F32)
    y += u_ref[...].astype(F32) * skip_ref[0]
    o_ref[...] = (gate_ref[...].astype(F32) * y).astype(o_ref.dtype)


def dft_inverse_gate(fct, fst, yre, yim, u, u_col, gate, gate_col, skip, order, *, bsz, length):
    c = HY_WIDTH
    nblk = length // INV_NB
    return pl.pallas_call(
        _dft_inv_kernel,
        out_shape=jax.ShapeDtypeStruct((bsz * length, c), BF16),
        grid=(nblk, bsz),
        in_specs=[
            pl.BlockSpec((INV_NB, FFT_KP), lambda nb, b: (nb, 0)),
            pl.BlockSpec((INV_NB, FFT_KP), lambda nb, b: (nb, 0)),
            pl.BlockSpec((1, FFT_KP, c), lambda nb, b: (b, 0, 0)),
            pl.BlockSpec((1, FFT_KP, c), lambda nb, b: (b, 0, 0)),
            pl.BlockSpec((INV_NB, c), lambda nb, b: (b * nblk + nb, u_col)),
            pl.BlockSpec((INV_NB, c), lambda nb, b: (b * nblk + nb, gate_col)),
            pl.BlockSpec((1, 1, c), lambda nb, b: (order, 0, 0)),
        ],
        out_specs=pl.BlockSpec((INV_NB, c), lambda nb, b: (b * nblk + nb, 0)),
        compiler_params=_params(("parallel", "parallel")),
        name="dft_inverse_gate",
    )(fct, fst, yre, yim, u, gate, skip)


def hyena_mixer(proj, w_conv, w1, b1, w2, b2, w3, freq, decay, skip, dft, *, bsz, length):
    fc, fs, fct, fst = dft
    vxx = hyena_in_conv(proj, w_conv, bsz=bsz, length=length)
    filt, abs_sum = hyena_filters(length, w1, b1, w2, b2, w3, freq, decay)
    a_re, a_im = dft_filters(fc, fs, filt)
    skip = skip.astype(F32).reshape(HY_ORDER, 1, HY_WIDTH)
    kw = dict(bsz=bsz, length=length)
    yre, yim = dft_forward_mul(fc, fs, vxx, 0, a_re, a_im, abs_sum, 0, **kw)
    z = dft_inverse_gate(fct, fst, yre, yim, vxx, 0, vxx, 1, skip, 0, **kw)
    yre, yim = dft_forward_mul(fc, fs, z, 0, a_re, a_im, abs_sum, 1, **kw)
    return dft_inverse_gate(fct, fst, yre, yim, z, 0, vxx, 2, skip, 1, **kw)


MOE_SUPER = 2 * MOE_BLOCK
MOE_FC = 512
MOE_VMEM_LIMIT = V7X_VMEM_BYTES - 3 * 1024 * 1024


def _moe_kernel(be_ref, nh_ref, x_ref, w1_ref, w3_ref, w2_ref, yb_ref, o_ref, acc_ref):
    del be_ref, yb_ref
    i = pl.program_id(0)
    j = pl.program_id(1)
    halves = nh_ref[i]

    @pl.when(j == 0)
    def _():
        acc_ref[...] = jnp.zeros_like(acc_ref)

    def ffn(rows):
        x = x_ref[0:rows, :]
        a = jnp.dot(x, w1_ref[0].astype(BF16), preferred_element_type=F32)
        b = jnp.dot(x, w3_ref[0].astype(BF16), preferred_element_type=F32)
        hid = (a * _sigmoid(a) * b).astype(BF16)
        acc_ref[0:rows, :] += jnp.dot(hid, w2_ref[0].astype(BF16), preferred_element_type=F32)

    @pl.when(halves == 2)
    def _():
        ffn(MOE_SUPER)

    @pl.when(halves == 1)
    def _():
        ffn(MOE_BLOCK)

    @pl.when(j == pl.num_programs(1) - 1)
    def _():
        o_ref[...] = acc_ref[...].astype(o_ref.dtype)


def moe_experts(xb, sb_start, block_expert, n_halves, w1, w3, w2, yb):
    d = xb.shape[1]
    n_super = xb.shape[0] // MOE_SUPER
    fc = MOE_FC
    n_fc = D_FF_EXPERT // fc

    def chunk(i, j, nh):
        return jnp.where(nh[i] > 0, j, n_fc - 1)

    grid_spec = pltpu.PrefetchScalarGridSpec(
        num_scalar_prefetch=2,
        grid=(n_super, n_fc),
        in_specs=[
            pl.BlockSpec((MOE_SUPER, d), lambda i, j, be, nh: (i, 0)),
            pl.BlockSpec((1, d, fc), lambda i, j, be, nh: (be[i], 0, chunk(i, j, nh))),
            pl.BlockSpec((1, d, fc), lambda i, j, be, nh: (be[i], 0, chunk(i, j, nh))),
            pl.BlockSpec((1, fc, d), lambda i, j, be, nh: (be[i], chunk(i, j, nh), 0)),
            pl.BlockSpec(memory_space=pl.ANY),
        ],
        out_specs=pl.BlockSpec((MOE_SUPER, d), lambda i, j, be, nh: (sb_start + i, 0)),
        scratch_shapes=[pltpu.VMEM((MOE_SUPER, d), F32)],
    )
    return pl.pallas_call(
        _moe_kernel,
        out_shape=jax.ShapeDtypeStruct(yb.shape, yb.dtype),
        grid_spec=grid_spec,
        input_output_aliases={6: 0},
        compiler_params=_params(("parallel", "arbitrary"), MOE_VMEM_LIMIT),
        name="moe_experts",
    )(block_expert, n_halves, xb, w1, w3, w2, yb)


COMBINE_ROWS = 256


def _moe_combine_kernel(h_ref, y0_ref, y1_ref, g0_ref, g1_ref, gn_ref, o_ref, *, final):
    rows = o_ref.shape[0]
    gn = gn_ref[...]

    def chunk(c, carry):
        r0 = pl.multiple_of(c * NORM_CHUNK, NORM_CHUNK)
        sl = pl.ds(r0, NORM_CHUNK)
        h = h_ref[0, sl, :] if final else h_ref[sl, :]
        y = h + g0_ref[sl, :] * y0_ref[sl, :].astype(F32) + g1_ref[sl, :] * y1_ref[sl, :].astype(F32)
        if final:
            ms = jnp.mean(y * y, axis=-1, keepdims=True)
            y = y * lax.rsqrt(ms + NORM_EPS) * gn
        o_ref[sl, :] = y.astype(o_ref.dtype)
        return carry

    lax.fori_loop(0, rows // NORM_CHUNK, chunk, 0, unroll=4)


def moe_combine(h, y0, y1, g0, g1, *, tm=ROW_TILE):
    t, d = h.shape
    row = lambda w: pl.BlockSpec((tm, w), lambda i: (i, 0))
    return pl.pallas_call(
        functools.partial(_moe_combine_kernel, final=False),
        out_shape=jax.ShapeDtypeStruct((t, d), F32),
        grid=(t // tm,),
        in_specs=[row(d), row(d), row(d), row(1), row(1), pl.BlockSpec((1, d), lambda i: (0, 0))],
        out_specs=row(d),
        compiler_params=_params(("parallel",)),
        name="moe_combine",
    )(h, y0, y1, g0, g1, jnp.ones((1, d), F32))


def moe_combine_final(h, y0, y1, g0, g1, g_final, *, bsz, length):
    d = h.shape[1]
    seq = length - N_META
    r = COMBINE_ROWS
    nr = seq // r
    row = lambda w: pl.BlockSpec((r, w), lambda b, i: (b * nr + i, 0))
    h_spec = pl.BlockSpec((pl.Element(1), pl.Element(r), pl.Element(d)),
                          lambda b, i: (b, pl.multiple_of(N_META + i * r, N_META), 0))
    return pl.pallas_call(
        functools.partial(_moe_combine_kernel, final=True),
        out_shape=jax.ShapeDtypeStruct((bsz, seq, d), F32),
        grid=(bsz, nr),
        in_specs=[h_spec, row(d), row(d), row(1), row(1), pl.BlockSpec((1, d), lambda b, i: (0, 0))],
        out_specs=pl.BlockSpec((None, r, d), lambda b, i: (b, i, 0)),
        compiler_params=_params(("parallel", "parallel")),
        name="moe_combine_final",
    )(h.reshape(bsz, length, d), y0, y1, g0, g1, g_final.astype(F32).reshape(1, d))


SCAN_CHUNK = 256
MOE_CALL_SPLIT = (4, 16, None)


def _expert_ranks(expert):
    n = expert.shape[0]
    onehot = (expert[:, None] == jnp.arange(N_EXPERTS, dtype=jnp.int32)[None, :]).astype(F32)
    chunks = onehot.reshape(n // SCAN_CHUNK, SCAN_CHUNK, N_EXPERTS)
    tri = jnp.tril(jnp.ones((SCAN_CHUNK, SCAN_CHUNK), F32), k=-1)
    within = jnp.einsum('ij,cjk->cik', tri, chunks)
    totals = jnp.sum(chunks, axis=1)
    before = jnp.cumsum(totals, axis=0) - totals
    rank = jnp.sum((within + before[:, None, :]) * chunks, axis=-1).reshape(n)
    return rank.astype(jnp.int32), jnp.sum(totals, axis=0).astype(jnp.int32)


def moe_swiglu(h, g_norm, w_router, w1, w3, w2, g_final, *, bsz, length):
    n_tok, d = h.shape
    n_assign = n_tok * TOP_K
    router_pad = 128
    wr = jnp.pad(w_router.astype(F32), ((0, 0), (0, router_pad - N_EXPERTS)))
    logits, hn = norm_router(h, g_norm, wr)
    top_val, top_idx = lax.top_k(logits[:, :N_EXPERTS], TOP_K)
    gate = jax.nn.softmax(top_val, axis=-1)
    expert = top_idx.astype(jnp.int32).reshape(n_assign)
    rank, counts = _expert_ranks(expert)
    padded = (counts + MOE_SUPER - 1) // MOE_SUPER * MOE_SUPER
    pad_end = jnp.cumsum(padded)
    pad_start = pad_end - padded
    slot = (pad_start[expert] + rank).reshape(n_tok, TOP_K)
    n_super = -(-n_assign // MOE_SUPER) + N_EXPERTS
    n_slots = n_super * MOE_SUPER
    token = jnp.repeat(jnp.arange(n_tok, dtype=jnp.int32), TOP_K)
    slot_token = jnp.zeros((n_slots,), jnp.int32).at[slot.reshape(n_assign)].set(token)
    n_used = pad_end[-1] // MOE_SUPER
    sb = jnp.arange(n_super, dtype=jnp.int32)
    block_expert = jnp.minimum(jnp.searchsorted(pad_end, jnp.minimum(sb, n_used - 1) * MOE_SUPER, side='right'),
                               N_EXPERTS - 1).astype(jnp.int32)
    rows_here = jnp.clip(counts[block_expert] - (sb * MOE_SUPER - pad_start[block_expert]), 0, MOE_SUPER)
    n_halves = jnp.where(sb < n_used, (rows_here + MOE_BLOCK - 1) // MOE_BLOCK, 0).astype(jnp.int32)
    yb = jnp.zeros((n_slots, d), BF16)
    sb0 = 0
    for part in MOE_CALL_SPLIT:
        sb1 = n_super if part is None else sb0 + part
        rows = slice(sb0 * MOE_SUPER, sb1 * MOE_SUPER)
        yb = moe_experts(hn[slot_token[rows]], sb0, block_expert[sb0:sb1], n_halves[sb0:sb1], w1, w3, w2, yb)
        sb0 = sb1
    if g_final is None:
        return moe_combine(h, yb[slot[:, 0]], yb[slot[:, 1]], gate[:, 0:1], gate[:, 1:2])
    keep = lambda a: a.reshape(bsz, length, -1)[:, N_META:].reshape(bsz * (length - N_META), -1)
    slot, gate = keep(slot), keep(gate)
    return moe_combine_final(h, yb[slot[:, 0]], yb[slot[:, 1]], gate[:, 0:1], gate[:, 1:2], g_final,
                             bsz=bsz, length=length)


def kernel(x, meta_tokens, norm_mix, w_in, na_rpb, na_meta_bias, sc_conv, hy_conv, hy_w1, hy_b1, hy_w2, hy_b2,
           hy_w3, hy_freq, hy_decay, hy_skip, w_na_out, w_sc_out, w_hy_out, w_o, norm_ffn, ffn_w1, ffn_w3,
           ffn_w2, router_w, moe_w1, moe_w3, moe_w2, norm_final):
    bsz, seq, d = x.shape
    length = N_META + seq
    depth = w_in.shape[0]
    meta = jnp.broadcast_to(meta_tokens.astype(x.dtype)[None], (bsz, N_META, d))
    h = jnp.concatenate([meta, x], axis=1).reshape(bsz * length, d)
    dft = _dft_matrices(length)
    kw = dict(bsz=bsz, length=length)
    w_in, w_na_out, w_sc_out, w_hy_out, w_o, ffn_w1, ffn_w3, ffn_w2 = map(
        to_bf16, (w_in, w_na_out, w_sc_out, w_hy_out, w_o, ffn_w1, ffn_w3, ffn_w2))
    for layer in range(depth):
        proj = norm_matmul(h, norm_mix[layer], w_in[layer], tn=2048)
        y_na = neighbourhood_attention(proj, na_rpb[layer], na_meta_bias[layer], **kw)
        y_sc = short_conv_mixer(proj, sc_conv[layer], **kw)
        y_hy = hyena_mixer(proj, hy_conv[layer], hy_w1[layer], hy_b1[layer], hy_w2[layer], hy_b2[layer],
                           hy_w3[layer], hy_freq[layer], hy_decay[layer], hy_skip[layer], dft, **kw)
        h = mixer_out(h, y_na, y_sc, y_hy, proj, w_na_out[layer], w_sc_out[layer], w_hy_out[layer], w_o[layer])
        j = layer // 2
        if layer % 2 == 0:
            h = dense_ffn(h, norm_ffn[layer], ffn_w1[j], ffn_w3[j], ffn_w2[j])
        else:
            g_final = norm_final if layer == depth - 1 else None
            h = moe_swiglu(h, norm_ffn[layer], router_w[j], moe_w1[j], moe_w3[j], moe_w2[j], g_final, **kw)
    if depth % 2 == 0:
        return h
    out = rms_norm_rows(h, norm_final, out_dtype=x.dtype)
    return out.reshape(bsz, length, d)[:, N_META:]
```

```python
import functools
import math

import numpy as np
import jax
import jax.numpy as jnp
from jax import lax
from jax.experimental import pallas as pl
from jax.experimental.pallas import tpu as pltpu

D_MODEL = 2048
N_META = 16
GRID_W = 64
GRID_ROWS = 32
NA_HEADS = 16
NA_HEAD_DIM = 64
NA_WIDTH = NA_HEADS * NA_HEAD_DIM
WIN_ROWS = 8
WIN_COLS = 16
SC_WIDTH = 512
HY_WIDTH = 512
HY_ORDER = 2
HY_EMB = 33
HY_HIDDEN = 64
HY_MOD_SHIFT = 0.05
PROJ_WIDTH = 3 * NA_WIDTH + 3 * SC_WIDTH + 3 * HY_WIDTH + 3 * D_MODEL
D_FF = 5632
N_EXPERTS = 8
TOP_K = 2
D_FF_EXPERT = 7168
MOE_BLOCK = 512
NORM_EPS = 1e-6
NEG_INF = -1e30

COL_Q, COL_K, COL_V = 0, NA_WIDTH, 2 * NA_WIDTH
COL_SCB = 3 * NA_WIDTH
COL_SCC = COL_SCB + SC_WIDTH
COL_SCU = COL_SCC + SC_WIDTH
COL_HYU = COL_SCU + SC_WIDTH
COL_GNA = COL_HYU + 3 * HY_WIDTH
COL_GSC = COL_GNA + D_MODEL
COL_GHY = COL_GSC + D_MODEL

V7X_VMEM_BYTES = 64 * 1024 * 1024
VMEM_LIMIT = V7X_VMEM_BYTES - 8 * 1024 * 1024

ROW_TILE = 768
NORM_CHUNK = 16
FFT_N = 2 * (N_META + GRID_ROWS * GRID_W)
FFT_KP = 2080
FFT_KB = 1040
INV_NB = 688

F32 = jnp.float32
BF16 = jnp.bfloat16


def _params(sem, vmem_limit=VMEM_LIMIT):
    return pltpu.CompilerParams(dimension_semantics=sem, vmem_limit_bytes=vmem_limit)


def _sigmoid(a):
    return 0.5 * jnp.tanh(0.5 * a) + 0.5


CAST_BLOCK_BYTES = 8 * 1024 * 1024


def _cast_kernel(x_ref, o_ref):
    o_ref[...] = x_ref[...].astype(o_ref.dtype)


def to_bf16(w):
    cols = w.shape[-1]
    w2 = w.reshape(-1, cols)
    rows = w2.shape[0]
    tr = min(rows, CAST_BLOCK_BYTES // (4 * cols)) // 16 * 16
    while rows % tr:
        tr -= 16
    out = pl.pallas_call(
        _cast_kernel,
        out_shape=jax.ShapeDtypeStruct((rows, cols), BF16),
        grid=(rows // tr,),
        in_specs=[pl.BlockSpec((tr, cols), lambda i: (i, 0))],
        out_specs=pl.BlockSpec((tr, cols), lambda i: (i, 0)),
        compiler_params=_params(("parallel",)),
        name="to_bf16",
    )(w2)
    return out.reshape(w.shape)


def _rms_norm_rows(x_ref, g_ref, xn_ref):
    rows = x_ref.shape[0]
    g = g_ref[...]

    def chunk(c, carry):
        r0 = pl.multiple_of(c * NORM_CHUNK, NORM_CHUNK)
        x = x_ref[pl.ds(r0, NORM_CHUNK), :]
        ms = jnp.mean(x * x, axis=-1, keepdims=True)
        xn_ref[pl.ds(r0, NORM_CHUNK), :] = (x * lax.rsqrt(ms + NORM_EPS) * g).astype(xn_ref.dtype)
        return carry

    lax.fori_loop(0, rows // NORM_CHUNK, chunk, 0, unroll=4)


def _norm_mm_kernel(x_ref, g_ref, w_ref, o_ref, xn_ref):
    @pl.when(pl.program_id(1) == 0)
    def _():
        _rms_norm_rows(x_ref, g_ref, xn_ref)

    o_ref[...] = jnp.dot(xn_ref[...], w_ref[...], preferred_element_type=F32).astype(o_ref.dtype)


def norm_matmul(x, g, w, *, tn, out_dtype=BF16, tm=ROW_TILE):
    t, k = x.shape
    n = w.shape[1]
    return pl.pallas_call(
        _norm_mm_kernel,
        out_shape=jax.ShapeDtypeStruct((t, n), out_dtype),
        grid=(t // tm, n // tn),
        in_specs=[
            pl.BlockSpec((tm, k), lambda i, j: (i, 0)),
            pl.BlockSpec((1, k), lambda i, j: (0, 0)),
            pl.BlockSpec((k, tn), lambda i, j: (0, j)),
        ],
        out_specs=pl.BlockSpec((tm, tn), lambda i, j: (i, j)),
        scratch_shapes=[pltpu.VMEM((tm, k), BF16)],
        compiler_params=_params(("parallel", "arbitrary")),
        name="norm_matmul",
    )(x, g.reshape(1, k), w)


def _norm_router_kernel(x_ref, g_ref, w_ref, o_ref, xb_ref, xn_ref):
    _rms_norm_rows(x_ref, g_ref, xn_ref)
    xn = xn_ref[...]
    x_hi = xn.astype(BF16)
    x_lo = (xn - x_hi.astype(F32)).astype(BF16)
    w = w_ref[...]
    w_hi = w.astype(BF16)
    w_lo = (w - w_hi.astype(F32)).astype(BF16)
    xb_ref[...] = x_hi
    logits = jnp.dot(x_hi, w_hi, preferred_element_type=F32)
    logits += jnp.dot(x_lo, w_hi, preferred_element_type=F32)
    logits += jnp.dot(x_hi, w_lo, preferred_element_type=F32)
    o_ref[...] = logits


def norm_router(x, g, w, *, tm=ROW_TILE):
    t, k = x.shape
    n = w.shape[1]
    return pl.pallas_call(
        _norm_router_kernel,
        out_shape=(jax.ShapeDtypeStruct((t, n), F32), jax.ShapeDtypeStruct((t, k), BF16)),
        grid=(t // tm,),
        in_specs=[
            pl.BlockSpec((tm, k), lambda i: (i, 0)),
            pl.BlockSpec((1, k), lambda i: (0, 0)),
            pl.BlockSpec((k, n), lambda i: (0, 0)),
        ],
        out_specs=(pl.BlockSpec((tm, n), lambda i: (i, 0)), pl.BlockSpec((tm, k), lambda i: (i, 0))),
        scratch_shapes=[pltpu.VMEM((tm, k), F32)],
        compiler_params=_params(("parallel",)),
        name="norm_router",
    )(x, g.reshape(1, k), w)


def _norm_kernel(x_ref, g_ref, o_ref):
    _rms_norm_rows(x_ref, g_ref, o_ref)


def rms_norm_rows(x, g, *, out_dtype, tm=ROW_TILE):
    t, k = x.shape
    return pl.pallas_call(
        _norm_kernel,
        out_shape=jax.ShapeDtypeStruct((t, k), out_dtype),
        grid=(t // tm,),
        in_specs=[pl.BlockSpec((tm, k), lambda i: (i, 0)), pl.BlockSpec((1, k), lambda i: (0, 0))],
        out_specs=pl.BlockSpec((tm, k), lambda i: (i, 0)),
        compiler_params=_params(("parallel",)),
        name="rms_norm",
    )(x, g.reshape(1, k))


def _dense_ffn_kernel(x_ref, g_ref, w1_ref, w3_ref, w2_ref, o_ref, xn_ref):
    @pl.when(pl.program_id(1) == 0)
    def _():
        _rms_norm_rows(x_ref, g_ref, xn_ref)
        o_ref[...] = x_ref[...]

    xn = xn_ref[...]
    a = jnp.dot(xn, w1_ref[...], preferred_element_type=F32)
    b = jnp.dot(xn, w3_ref[...], preferred_element_type=F32)
    hid = (a * _sigmoid(a) * b).astype(BF16)
    o_ref[...] += jnp.dot(hid, w2_ref[...], preferred_element_type=F32)


def dense_ffn(x, g, w1, w3, w2, *, fc=512, tm=ROW_TILE):
    t, d = x.shape
    f = w1.shape[1]
    return pl.pallas_call(
        _dense_ffn_kernel,
        out_shape=jax.ShapeDtypeStruct((t, d), F32),
        grid=(t // tm, f // fc),
        in_specs=[
            pl.BlockSpec((tm, d), lambda i, j: (i, 0)),
            pl.BlockSpec((1, d), lambda i, j: (0, 0)),
            pl.BlockSpec((d, fc), lambda i, j: (0, j)),
            pl.BlockSpec((d, fc), lambda i, j: (0, j)),
            pl.BlockSpec((fc, d), lambda i, j: (j, 0)),
        ],
        out_specs=pl.BlockSpec((tm, d), lambda i, j: (i, 0)),
        scratch_shapes=[pltpu.VMEM((tm, d), BF16)],
        compiler_params=_params(("parallel", "arbitrary")),
        name="dense_ffn",
    )(x, g.reshape(1, d), w1, w3, w2)


def _mixer_out_kernel(res_ref, yna_ref, ysc_ref, yhy_ref, gna_ref, gsc_ref, ghy_ref, wna_ref, wsc_ref, why_ref,
                      wo_ref, o_ref):
    @pl.when(pl.program_id(1) == 0)
    def _():
        o_ref[...] = res_ref[...]

    m = _sigmoid(gna_ref[...].astype(F32)) * jnp.dot(yna_ref[...], wna_ref[...], preferred_element_type=F32)
    m += _sigmoid(gsc_ref[...].astype(F32)) * jnp.dot(ysc_ref[...], wsc_ref[...], preferred_element_type=F32)
    m += _sigmoid(ghy_ref[...].astype(F32)) * jnp.dot(yhy_ref[...], why_ref[...], preferred_element_type=F32)
    o_ref[...] += jnp.dot(m.astype(BF16), wo_ref[...], preferred_element_type=F32)


MIXER_ROWS = 256


def mixer_out(res, y_na, y_sc, y_hy, proj, w_na, w_sc, w_hy, w_o, *, cw=D_MODEL, tm=MIXER_ROWS):
    t, d = res.shape
    return pl.pallas_call(
        _mixer_out_kernel,
        out_shape=jax.ShapeDtypeStruct((t, d), F32),
        grid=(t // tm, d // cw),
        in_specs=[
            pl.BlockSpec((tm, d), lambda i, c: (i, 0)),
            pl.BlockSpec((tm, NA_WIDTH), lambda i, c: (i, 0)),
            pl.BlockSpec((tm, SC_WIDTH), lambda i, c: (i, 0)),
            pl.BlockSpec((tm, HY_WIDTH), lambda i, c: (i, 0)),
            pl.BlockSpec((tm, cw), lambda i, c: (i, COL_GNA // cw + c)),
            pl.BlockSpec((tm, cw), lambda i, c: (i, COL_GSC // cw + c)),
            pl.BlockSpec((tm, cw), lambda i, c: (i, COL_GHY // cw + c)),
            pl.BlockSpec((NA_WIDTH, cw), lambda i, c: (0, c)),
            pl.BlockSpec((SC_WIDTH, cw), lambda i, c: (0, c)),
            pl.BlockSpec((HY_WIDTH, cw), lambda i, c: (0, c)),
            pl.BlockSpec((cw, d), lambda i, c: (c, 0)),
        ],
        out_specs=pl.BlockSpec((tm, d), lambda i, c: (i, 0)),
        compiler_params=_params(("parallel", "arbitrary")),
        name="mixer_out",
    )(res, y_na, y_sc, y_hy, proj, proj, proj, w_na, w_sc, w_hy, w_o)


def _na_bias_table(rpb):
    qc = np.arange(GRID_W)[:, None]
    kc = np.arange(GRID_W)[None, :]
    win_start = np.clip(qc - WIN_COLS // 2, 0, GRID_W - WIN_COLS)
    inside = (kc >= win_start) & (kc < win_start + WIN_COLS)
    dc = np.clip(kc - qc, -(WIN_COLS - 1), WIN_COLS - 1) + WIN_COLS - 1
    dr = np.arange(WIN_ROWS)[None, :] - np.arange(WIN_ROWS)[:, None] + WIN_ROWS - 1
    onehot = np.zeros((2 * WIN_COLS - 1, GRID_W * GRID_W), np.float32)
    onehot[dc.reshape(-1), np.arange(GRID_W * GRID_W)] = 1.0
    tbl = jnp.einsum('htid,dj->htij', rpb.astype(F32)[:, dr], jnp.asarray(onehot),
                     precision=lax.Precision.HIGHEST)
    tbl = tbl.reshape(NA_HEADS, WIN_ROWS, WIN_ROWS, GRID_W, GRID_W)
    tbl = jnp.where(jnp.asarray(inside)[None, None, None], tbl, NEG_INF)
    tbl = tbl.transpose(0, 1, 3, 2, 4)
    return tbl.reshape(NA_HEADS, WIN_ROWS, GRID_W, WIN_ROWS * GRID_W)


NA_SOFTMAX_LAG = 1
NA_PV_LAG = 2


def _na_kernel(q_ref, k_ref, v_ref, tbl_ref, mb_ref, o_ref):
    scale = NA_HEAD_DIM ** -0.5
    lane = lax.broadcasted_iota(jnp.int32, (1, 2 * NA_HEAD_DIM), 1)
    first = lane < NA_HEAD_DIM
    nt = (((1,), (1,)), ((), ()))
    k_meta = k_ref[0:N_META, :]
    v_meta = v_ref[0:N_META, :]
    mb_grid = mb_ref[0, 0]
    band = WIN_ROWS * GRID_W

    def stack_heads(q):
        q = q * scale
        zero = jnp.zeros_like(q)
        return jnp.concatenate([jnp.where(first, q, zero), jnp.where(first, zero, q)], axis=0)

    def unstack_heads(out, den, n):
        out = out / den
        return jnp.where(first, out[:n], out[n:])

    def softmax_parts(s_grid, s_meta):
        m = jnp.max(s_meta, axis=-1, keepdims=True)
        if s_grid is not None:
            m = jnp.maximum(m, jnp.max(s_grid, axis=-1, keepdims=True))
        p_meta = jnp.exp(s_meta - m)
        den = jnp.sum(p_meta, axis=-1, keepdims=True)
        p_grid = None
        if s_grid is not None:
            p_grid = jnp.exp(s_grid - m)
            den = den + jnp.sum(p_grid, axis=-1, keepdims=True)
            p_grid = p_grid.astype(BF16)
        return p_grid, p_meta.astype(BF16), den

    def geometry(r):
        rs = min(max(r - WIN_ROWS // 2, 0), GRID_ROWS - WIN_ROWS)
        return N_META + r * GRID_W, N_META + rs * GRID_W, r - rs

    def scores(r):
        if r is None:
            q2 = stack_heads(q_ref[0:N_META, :])
            return None, lax.dot_general(q2, k_meta, nt, preferred_element_type=F32) + mb_ref[0, 1, 0:2 * N_META, :]
        q0, k0, typ = geometry(r)
        q2 = stack_heads(q_ref[q0:q0 + GRID_W, :])
        s_grid = lax.dot_general(q2, k_ref[k0:k0 + band, :], nt, preferred_element_type=F32) + tbl_ref[0, typ]
        s_meta = lax.dot_general(q2, k_meta, nt, preferred_element_type=F32) + mb_grid
        return s_grid, s_meta

    def weighted_values(r, p_grid, p_meta, den):
        out = jnp.dot(p_meta, v_meta, preferred_element_type=F32)
        if r is None:
            o_ref[0:N_META, :] = unstack_heads(out, den, N_META).astype(o_ref.dtype)
            return
        q0, k0, _ = geometry(r)
        out += jnp.dot(p_grid, v_ref[k0:k0 + band, :], preferred_element_type=F32)
        o_ref[q0:q0 + GRID_W, :] = unstack_heads(out, den, GRID_W).astype(o_ref.dtype)

    units = [None] + list(range(GRID_ROWS))
    pending_scores, pending_probs = {}, {}
    for t in range(len(units) + NA_PV_LAG):
        if t < len(units):
            pending_scores[t] = scores(units[t])
        if 0 <= t - NA_SOFTMAX_LAG < len(units):
            pending_probs[t - NA_SOFTMAX_LAG] = softmax_parts(*pending_scores.pop(t - NA_SOFTMAX_LAG))
        if 0 <= t - NA_PV_LAG < len(units):
            weighted_values(units[t - NA_PV_LAG], *pending_probs.pop(t - NA_PV_LAG))


def neighbourhood_attention(proj, rpb, meta_bias, *, bsz, length):
    hp = 2 * NA_HEAD_DIM
    n_pairs = NA_HEADS // 2
    band = WIN_ROWS * GRID_W
    tbl = _na_bias_table(rpb).reshape(n_pairs, 2, WIN_ROWS, GRID_W, band).transpose(0, 2, 1, 3, 4)
    tbl = tbl.reshape(n_pairs, WIN_ROWS, 2 * GRID_W, band)
    mb = meta_bias.astype(F32).reshape(n_pairs, 2, 1, N_META)
    mb_grid = jnp.broadcast_to(mb, (n_pairs, 2, GRID_W, N_META)).reshape(n_pairs, 2 * GRID_W, N_META)
    mb_meta = jnp.broadcast_to(mb, (n_pairs, 2, N_META, N_META)).reshape(n_pairs, 2 * N_META, N_META)
    mb_meta = jnp.pad(mb_meta, ((0, 0), (0, 2 * GRID_W - 2 * N_META), (0, 0)))
    mb2 = jnp.stack([mb_grid, mb_meta], axis=1)
    return pl.pallas_call(
        _na_kernel,
        out_shape=jax.ShapeDtypeStruct((bsz * length, NA_WIDTH), BF16),
        grid=(n_pairs, bsz),
        in_specs=[
            pl.BlockSpec((length, hp), lambda h, b: (b, COL_Q // hp + h)),
            pl.BlockSpec((length, hp), lambda h, b: (b, COL_K // hp + h)),
            pl.BlockSpec((length, hp), lambda h, b: (b, COL_V // hp + h)),
            pl.BlockSpec((1, WIN_ROWS, 2 * GRID_W, band), lambda h, b: (h, 0, 0, 0)),
            pl.BlockSpec((1, 2, 2 * GRID_W, N_META), lambda h, b: (h, 0, 0, 0)),
        ],
        out_specs=pl.BlockSpec((length, hp), lambda h, b: (b, h)),
        compiler_params=_params(("parallel", "parallel")),
        name="neighbourhood_attention",
    )(proj, proj, proj, tbl, mb2)


def _dwconv3(u, w):
    length = u.shape[0]
    row = lax.broadcasted_iota(jnp.int32, u.shape, 0)
    prev = jnp.where(row == 0, 0.0, pltpu.roll(u, 1, 0))
    nxt = jnp.where(row == length - 1, 0.0, pltpu.roll(u, length - 1, 0))
    return prev * w[0:1] + u * w[1:2] + nxt * w[2:3]


def _gated_conv_kernel(b_ref, c_ref, u_ref, w_ref, o_ref):
    u = c_ref[...].astype(F32) * u_ref[...].astype(F32)
    o_ref[...] = (b_ref[...].astype(F32) * _dwconv3(u, w_ref[...])).astype(o_ref.dtype)


def _plain_conv_kernel(u_ref, w_ref, o_ref):
    o_ref[...] = _dwconv3(u_ref[...].astype(F32), w_ref[...]).astype(o_ref.dtype)


CONV_CB = 512


def short_conv_mixer(proj, w_conv, *, bsz, length):
    cb = CONV_CB
    return pl.pallas_call(
        _gated_conv_kernel,
        out_shape=jax.ShapeDtypeStruct((bsz * length, SC_WIDTH), BF16),
        grid=(bsz, SC_WIDTH // cb),
        in_specs=[
            pl.BlockSpec((length, cb), lambda b, c: (b, COL_SCB // cb + c)),
            pl.BlockSpec((length, cb), lambda b, c: (b, COL_SCC // cb + c)),
            pl.BlockSpec((length, cb), lambda b, c: (b, COL_SCU // cb + c)),
            pl.BlockSpec((3, cb), lambda b, c: (0, c)),
        ],
        out_specs=pl.BlockSpec((length, cb), lambda b, c: (b, c)),
        compiler_params=_params(("parallel", "parallel")),
        name="short_conv_mixer",
    )(proj, proj, proj, w_conv.astype(F32))


def hyena_in_conv(proj, w_conv, *, bsz, length):
    cb = CONV_CB
    return pl.pallas_call(
        _plain_conv_kernel,
        out_shape=jax.ShapeDtypeStruct((bsz * length, 3 * HY_WIDTH), BF16),
        grid=(bsz, 3 * HY_WIDTH // cb),
        in_specs=[
            pl.BlockSpec((length, cb), lambda b, c: (b, COL_HYU // cb + c)),
            pl.BlockSpec((3, cb), lambda b, c: (0, c)),
        ],
        out_specs=pl.BlockSpec((length, cb), lambda b, c: (b, c)),
        compiler_params=_params(("parallel", "parallel")),
        name="hyena_in_conv",
    )(proj, w_conv.astype(F32))


def _filter_kernel(z_ref, w1_ref, b1_ref, w2_ref, b2_ref, w3_ref, fr_ref, dec_ref, f_ref, s_ref):
    hi = lax.Precision.HIGHEST
    z = z_ref[...]
    fr = fr_ref[...]
    h = jnp.sin(fr * (jnp.dot(z, w1_ref[...], preferred_element_type=F32, precision=hi) + b1_ref[...]))
    h = jnp.sin(fr * (jnp.dot(h, w2_ref[...], preferred_element_type=F32, precision=hi) + b2_ref[...]))
    h = jnp.dot(h, w3_ref[...], preferred_element_type=F32, precision=hi)
    t = z[:, 0:1]
    f = h * (jnp.exp(-t * jnp.abs(dec_ref[0])) + HY_MOD_SHIFT)
    row = lax.broadcasted_iota(jnp.int32, f.shape, 0)
    backward = pl.program_id(0) % 2 == 1
    f = jnp.where(jnp.logical_and(row == 0, backward), 0.0, f)
    f_ref[0] = f.astype(f_ref.dtype)
    s_ref[0] = jnp.sum(jnp.abs(f), axis=0, keepdims=True)


def hyena_filters(length, w1, b1, w2, b2, w3, freq, decay):
    bands = (HY_EMB - 1) // 2
    t = jnp.linspace(0.0, 1.0, length, dtype=F32)[:, None]
    ang = (2.0 * math.pi / length) * jnp.arange(length, dtype=F32)[:, None]
    fb = jnp.linspace(1e-4, bands - 1, bands, dtype=F32)[None, :]
    z = jnp.concatenate([t, jnp.cos(fb * ang), -jnp.sin(fb * ang)], axis=-1)
    emb = 128
    z = jnp.pad(z, ((0, 0), (0, emb - HY_EMB)))
    w1p = jnp.pad(w1.astype(F32), ((0, emb - HY_EMB), (0, 0)))
    n_slab = HY_ORDER * 2
    row = lambda a: a.astype(F32).reshape(1, -1)
    return pl.pallas_call(
        _filter_kernel,
        out_shape=(jax.ShapeDtypeStruct((n_slab, length, HY_WIDTH), BF16),
                   jax.ShapeDtypeStruct((n_slab, 1, HY_WIDTH), F32)),
        grid=(n_slab,),
        in_specs=[
            pl.BlockSpec((length, emb), lambda j: (0, 0)),
            pl.BlockSpec((emb, HY_HIDDEN), lambda j: (0, 0)),
            pl.BlockSpec((1, HY_HIDDEN), lambda j: (0, 0)),
            pl.BlockSpec((HY_HIDDEN, HY_HIDDEN), lambda j: (0, 0)),
            pl.BlockSpec((1, HY_HIDDEN), lambda j: (0, 0)),
            pl.BlockSpec((HY_HIDDEN, HY_WIDTH), lambda j: (0, j)),
            pl.BlockSpec((1, HY_HIDDEN), lambda j: (0, 0)),
            pl.BlockSpec((1, 1, HY_WIDTH), lambda j: (j, 0, 0)),
        ],
        out_specs=(pl.BlockSpec((1, length, HY_WIDTH), lambda j: (j, 0, 0)),
                   pl.BlockSpec((1, 1, HY_WIDTH), lambda j: (j, 0, 0))),
        compiler_params=_params(("parallel",)),
        name="hyena_filters",
    )(z, w1p, row(b1), w2.astype(F32), row(b2), w3.astype(F32), row(freq),
      decay.astype(F32).reshape(n_slab, 1, HY_WIDTH))


def _dft_matrices(length):
    k = jnp.arange(FFT_KP, dtype=jnp.int32)[:, None]
    n = jnp.arange(length, dtype=jnp.int32)[None, :]
    theta = (2.0 * math.pi / FFT_N) * ((k * n) % FFT_N).astype(F32)
    valid = k <= FFT_N // 2
    fc = jnp.where(valid, jnp.cos(theta), 0.0).astype(BF16)
    fs = jnp.where(valid, -jnp.sin(theta), 0.0).astype(BF16)
    fc, fs = lax.optimization_barrier((fc, fs))
    return fc, fs, fc.T, fs.T


def _dft_raw_kernel(fc_ref, fs_ref, u_ref, re_ref, im_ref):
    u = u_ref[0]
    re_ref[0] = jnp.dot(fc_ref[...], u, preferred_element_type=F32)
    im_ref[0] = jnp.dot(fs_ref[...], u, preferred_element_type=F32)


def dft_filters(fc, fs, filt):
    n_slab, length, c = filt.shape
    out = jax.ShapeDtypeStruct((n_slab, FFT_KP, c), F32)
    return pl.pallas_call(
        _dft_raw_kernel,
        out_shape=(out, out),
        grid=(FFT_KP // FFT_KB, n_slab),
        in_specs=[
            pl.BlockSpec((FFT_KB, length), lambda kb, s: (kb, 0)),
            pl.BlockSpec((FFT_KB, length), lambda kb, s: (kb, 0)),
            pl.BlockSpec((1, length, c), lambda kb, s: (s, 0, 0)),
        ],
        out_specs=(pl.BlockSpec((1, FFT_KB, c), lambda kb, s: (s, kb, 0)),
                   pl.BlockSpec((1, FFT_KB, c), lambda kb, s: (s, kb, 0))),
        compiler_params=_params(("parallel", "parallel")),
        name="dft_filters",
    )(fc, fs, filt)


def _dft_mul_kernel(fc_ref, fs_ref, u_ref, are_ref, aim_ref, s_ref, yre_ref, yim_ref, g_ref):
    @pl.when(pl.program_id(1) == 0)
    def _():
        k = pl.program_id(0) * FFT_KB + lax.broadcasted_iota(jnp.int32, (FFT_KB, 1), 0)
        wk = jnp.where(jnp.logical_or(k == 0, k == FFT_N // 2), 1.0, 2.0)
        wk = jnp.where(k > FFT_N // 2, 0.0, wk) / FFT_N
        inv = wk / (s_ref[0] + s_ref[1])
        g_ref[0] = (are_ref[0] + are_ref[1]) * inv
        g_ref[1] = (aim_ref[0] - aim_ref[1]) * inv

    u = u_ref[...]
    re = jnp.dot(fc_ref[...], u, preferred_element_type=F32)
    im = jnp.dot(fs_ref[...], u, preferred_element_type=F32)
    gre = g_ref[0]
    gim = g_ref[1]
    yre_ref[0] = (re * gre - im * gim).astype(yre_ref.dtype)
    yim_ref[0] = (re * gim + im * gre).astype(yim_ref.dtype)


def dft_forward_mul(fc, fs, u, col, a_re, a_im, abs_sum, order, *, bsz, length):
    c = HY_WIDTH
    out = jax.ShapeDtypeStruct((bsz, FFT_KP, c), BF16)
    return pl.pallas_call(
        _dft_mul_kernel,
        out_shape=(out, out),
        grid=(FFT_KP // FFT_KB, bsz),
        in_specs=[
            pl.BlockSpec((FFT_KB, length), lambda kb, b: (kb, 0)),
            pl.BlockSpec((FFT_KB, length), lambda kb, b: (kb, 0)),
            pl.BlockSpec((length, c), lambda kb, b: (b, col)),
            pl.BlockSpec((2, FFT_KB, c), lambda kb, b: (order, kb, 0)),
            pl.BlockSpec((2, FFT_KB, c), lambda kb, b: (order, kb, 0)),
            pl.BlockSpec((2, 1, c), lambda kb, b: (order, 0, 0)),
        ],
        out_specs=(pl.BlockSpec((1, FFT_KB, c), lambda kb, b: (b, kb, 0)),
                   pl.BlockSpec((1, FFT_KB, c), lambda kb, b: (b, kb, 0))),
        scratch_shapes=[pltpu.VMEM((2, FFT_KB, c), F32)],
        compiler_params=_params(("parallel", "arbitrary")),
        name="dft_forward_mul",
    )(fc, fs, u, a_re, a_im, abs_sum)


def _dft_inv_kernel(ct_ref, st_ref, yre_ref, yim_ref, u_ref, gate_ref, skip_ref, o_ref):
    y = jnp.dot(ct_ref[...], yre_ref[0], preferred_element_type=F32)
    y += jnp.dot(st_ref[...], yim_ref[0], preferred_element_type=F32)
    y += u_ref[...].astype(F32) * skip_ref[0]
    o_ref[...] = (gate_ref[...].astype(F32) * y).astype(o_ref.dtype)


def dft_inverse_gate(fct, fst, yre, yim, u, u_col, gate, gate_col, skip, order, *, bsz, length):
    c = HY_WIDTH
    nblk = length // INV_NB
    return pl.pallas_call(
        _dft_inv_kernel,
        out_shape=jax.ShapeDtypeStruct((bsz * length, c), BF16),
        grid=(nblk, bsz),
        in_specs=[
            pl.BlockSpec((INV_NB, FFT_KP), lambda nb, b: (nb, 0)),
            pl.BlockSpec((INV_NB, FFT_KP), lambda nb, b: (nb, 0)),
            pl.BlockSpec((1, FFT_KP, c), lambda nb, b: (b, 0, 0)),
            pl.BlockSpec((1, FFT_KP, c), lambda nb, b: (b, 0, 0)),
            pl.BlockSpec((INV_NB, c), lambda nb, b: (b * nblk + nb, u_col)),
            pl.BlockSpec((INV_NB, c), lambda nb, b: (b * nblk + nb, gate_col)),
            pl.BlockSpec((1, 1, c), lambda nb, b: (order, 0, 0)),
        ],
        out_specs=pl.BlockSpec((INV_NB, c), lambda nb, b: (b * nblk + nb, 0)),
        compiler_params=_params(("parallel", "parallel")),
        name="dft_inverse_gate",
    )(fct, fst, yre, yim, u, gate, skip)


def hyena_mixer(proj, w_conv, w1, b1, w2, b2, w3, freq, decay, skip, dft, *, bsz, length):
    fc, fs, fct, fst = dft
    vxx = hyena_in_conv(proj, w_conv, bsz=bsz, length=length)
    filt, abs_sum = hyena_filters(length, w1, b1, w2, b2, w3, freq, decay)
    a_re, a_im = dft_filters(fc, fs, filt)
    skip = skip.astype(F32).reshape(HY_ORDER, 1, HY_WIDTH)
    kw = dict(bsz=bsz, length=length)
    yre, yim = dft_forward_mul(fc, fs, vxx, 0, a_re, a_im, abs_sum, 0, **kw)
    z = dft_inverse_gate(fct, fst, yre, yim, vxx, 0, vxx, 1, skip, 0, **kw)
    yre, yim = dft_forward_mul(fc, fs, z, 0, a_re, a_im, abs_sum, 1, **kw)
    return dft_inverse_gate(fct, fst, yre, yim, z, 0, vxx, 2, skip, 1, **kw)


MOE_SUPER = 2 * MOE_BLOCK
MOE_FC = 512
MOE_VMEM_LIMIT = V7X_VMEM_BYTES - 3 * 1024 * 1024


def _moe_kernel(be_ref, nh_ref, x_ref, w1_ref, w3_ref, w2_ref, yb_ref, o_ref, acc_ref):
    del be_ref, yb_ref
    i = pl.program_id(0)
    j = pl.program_id(1)
    halves = nh_ref[i]

    @pl.when(j == 0)
    def _():
        acc_ref[...] = jnp.zeros_like(acc_ref)

    def ffn(rows):
        x = x_ref[0:rows, :]
        a = jnp.dot(x, w1_ref[0].astype(BF16), preferred_element_type=F32)
        b = jnp.dot(x, w3_ref[0].astype(BF16), preferred_element_type=F32)
        hid = (a * _sigmoid(a) * b).astype(BF16)
        acc_ref[0:rows, :] += jnp.dot(hid, w2_ref[0].astype(BF16), preferred_element_type=F32)

    @pl.when(halves == 2)
    def _():
        ffn(MOE_SUPER)

    @pl.when(halves == 1)
    def _():
        ffn(MOE_BLOCK)

    @pl.when(j == pl.num_programs(1) - 1)
    def _():
        o_ref[...] = acc_ref[...].astype(o_ref.dtype)


def moe_experts(xb, sb_start, block_expert, n_halves, w1, w3, w2, yb):
    d = xb.shape[1]
    n_super = xb.shape[0] // MOE_SUPER
    fc = MOE_FC
    n_fc = D_FF_EXPERT // fc

    def chunk(i, j, nh):
        return jnp.where(nh[i] > 0, j, n_fc - 1)

    grid_spec = pltpu.PrefetchScalarGridSpec(
        num_scalar_prefetch=2,
        grid=(n_super, n_fc),
        in_specs=[
            pl.BlockSpec((MOE_SUPER, d), lambda i, j, be, nh: (i, 0)),
            pl.BlockSpec((1, d, fc), lambda i, j, be, nh: (be[i], 0, chunk(i, j, nh))),
            pl.BlockSpec((1, d, fc), lambda i, j, be, nh: (be[i], 0, chunk(i, j, nh))),
            pl.BlockSpec((1, fc, d), lambda i, j, be, nh: (be[i], chunk(i, j, nh), 0)),
            pl.BlockSpec(memory_space=pl.ANY),
        ],
        out_specs=pl.BlockSpec((MOE_SUPER, d), lambda i, j, be, nh: (sb_start + i, 0)),
        scratch_shapes=[pltpu.VMEM((MOE_SUPER, d), F32)],
    )
    return pl.pallas_call(
        _moe_kernel,
        out_shape=jax.ShapeDtypeStruct(yb.shape, yb.dtype),
        grid_spec=grid_spec,
        input_output_aliases={6: 0},
        compiler_params=_params(("parallel", "arbitrary"), MOE_VMEM_LIMIT),
        name="moe_experts",
    )(block_expert, n_halves, xb, w1, w3, w2, yb)


COMBINE_ROWS = 256


def _moe_combine_kernel(h_ref, y0_ref, y1_ref, g0_ref, g1_ref, gn_ref, o_ref, *, final):
    rows = o_ref.shape[0]
    gn = gn_ref[...]

    def chunk(c, carry):
        r0 = pl.multiple_of(c * NORM_CHUNK, NORM_CHUNK)
        sl = pl.ds(r0, NORM_CHUNK)
        h = h_ref[0, sl, :] if final else h_ref[sl, :]
        y = h + g0_ref[sl, :] * y0_ref[sl, :].astype(F32) + g1_ref[sl, :] * y1_ref[sl, :].astype(F32)
        if final:
            ms = jnp.mean(y * y, axis=-1, keepdims=True)
            y = y * lax.rsqrt(ms + NORM_EPS) * gn
        o_ref[sl, :] = y.astype(o_ref.dtype)
        return carry

    lax.fori_loop(0, rows // NORM_CHUNK, chunk, 0, unroll=4)


def moe_combine(h, y0, y1, g0, g1, *, tm=ROW_TILE):
    t, d = h.shape
    row = lambda w: pl.BlockSpec((tm, w), lambda i: (i, 0))
    return pl.pallas_call(
        functools.partial(_moe_combine_kernel, final=False),
        out_shape=jax.ShapeDtypeStruct((t, d), F32),
        grid=(t // tm,),
        in_specs=[row(d), row(d), row(d), row(1), row(1), pl.BlockSpec((1, d), lambda i: (0, 0))],
        out_specs=row(d),
        compiler_params=_params(("parallel",)),
        name="moe_combine",
    )(h, y0, y1, g0, g1, jnp.ones((1, d), F32))


def moe_combine_final(h, y0, y1, g0, g1, g_final, *, bsz, length):
    d = h.shape[1]
    seq = length - N_META
    r = COMBINE_ROWS
    nr = seq // r
    row = lambda w: pl.BlockSpec((r, w), lambda b, i: (b * nr + i, 0))
    h_spec = pl.BlockSpec((pl.Element(1), pl.Element(r), pl.Element(d)),
                          lambda b, i: (b, pl.multiple_of(N_META + i * r, N_META), 0))
    return pl.pallas_call(
        functools.partial(_moe_combine_kernel, final=True),
        out_shape=jax.ShapeDtypeStruct((bsz, seq, d), F32),
        grid=(bsz, nr),
        in_specs=[h_spec, row(d), row(d), row(1), row(1), pl.BlockSpec((1, d), lambda b, i: (0, 0))],
        out_specs=pl.BlockSpec((None, r, d), lambda b, i: (b, i, 0)),
        compiler_params=_params(("parallel", "parallel")),
        name="moe_combine_final",
    )(h.reshape(bsz, length, d), y0, y1, g0, g1, g_final.astype(F32).reshape(1, d))


SCAN_CHUNK = 256
MOE_CALL_SPLIT = (4, 16, None)


def _expert_ranks(expert):
    n = expert.shape[0]
    onehot = (expert[:, None] == jnp.arange(N_EXPERTS, dtype=jnp.int32)[None, :]).astype(F32)
    chunks = onehot.reshape(n // SCAN_CHUNK, SCAN_CHUNK, N_EXPERTS)
    tri = jnp.tril(jnp.ones((SCAN_CHUNK, SCAN_CHUNK), F32), k=-1)
    within = jnp.einsum('ij,cjk->cik', tri, chunks)
    totals = jnp.sum(chunks, axis=1)
    before = jnp.cumsum(totals, axis=0) - totals
    rank = jnp.sum((within + before[:, None, :]) * chunks, axis=-1).reshape(n)
    return rank.astype(jnp.int32), jnp.sum(totals, axis=0).astype(jnp.int32)


def moe_swiglu(h, g_norm, w_router, w1, w3, w2, g_final, *, bsz, length):
    n_tok, d = h.shape
    n_assign = n_tok * TOP_K
    router_pad = 128
    wr = jnp.pad(w_router.astype(F32), ((0, 0), (0, router_pad - N_EXPERTS)))
    logits, hn = norm_router(h, g_norm, wr)
    top_val, top_idx = lax.top_k(logits[:, :N_EXPERTS], TOP_K)
    gate = jax.nn.softmax(top_val, axis=-1)
    expert = top_idx.astype(jnp.int32).reshape(n_assign)
    rank, counts = _expert_ranks(expert)
    padded = (counts + MOE_SUPER - 1) // MOE_SUPER * MOE_SUPER
    pad_end = jnp.cumsum(padded)
    pad_start = pad_end - padded
    slot = (pad_start[expert] + rank).reshape(n_tok, TOP_K)
    n_super = -(-n_assign // MOE_SUPER) + N_EXPERTS
    n_slots = n_super * MOE_SUPER
    token = jnp.repeat(jnp.arange(n_tok, dtype=jnp.int32), TOP_K)
    slot_token = jnp.zeros((n_slots,), jnp.int32).at[slot.reshape(n_assign)].set(token)
    n_used = pad_end[-1] // MOE_SUPER
    sb = jnp.arange(n_super, dtype=jnp.int32)
    block_expert = jnp.minimum(jnp.searchsorted(pad_end, jnp.minimum(sb, n_used - 1) * MOE_SUPER, side='right'),
                               N_EXPERTS - 1).astype(jnp.int32)
    rows_here = jnp.clip(counts[block_expert] - (sb * MOE_SUPER - pad_start[block_expert]), 0, MOE_SUPER)
    n_halves = jnp.where(sb < n_used, (rows_here + MOE_BLOCK - 1) // MOE_BLOCK, 0).astype(jnp.int32)
    yb = jnp.zeros((n_slots, d), BF16)
    sb0 = 0
    for part in MOE_CALL_SPLIT:
        sb1 = n_super if part is None else sb0 + part
        rows = slice(sb0 * MOE_SUPER, sb1 * MOE_SUPER)
        yb = moe_experts(hn[slot_token[rows]], sb0, block_expert[sb0:sb1], n_halves[sb0:sb1], w1, w3, w2, yb)
        sb0 = sb1
    if g_final is None:
        return moe_combine(h, yb[slot[:, 0]], yb[slot[:, 1]], gate[:, 0:1], gate[:, 1:2])
    keep = lambda a: a.reshape(bsz, length, -1)[:, N_META:].reshape(bsz * (length - N_META), -1)
    slot, gate = keep(slot), keep(gate)
    return moe_combine_final(h, yb[slot[:, 0]], yb[slot[:, 1]], gate[:, 0:1], gate[:, 1:2], g_final,
                             bsz=bsz, length=length)


def kernel(x, meta_tokens, norm_mix, w_in, na_rpb, na_meta_bias, sc_conv, hy_conv, hy_w1, hy_b1, hy_w2, hy_b2,
           hy_w3, hy_freq, hy_decay, hy_skip, w_na_out, w_sc_out, w_hy_out, w_o, norm_ffn, ffn_w1, ffn_w3,
           ffn_w2, router_w, moe_w1, moe_w3, moe_w2, norm_final):
    bsz, seq, d = x.shape
    length = N_META + seq
    depth = w_in.shape[0]
    meta = jnp.broadcast_to(meta_tokens.astype(x.dtype)[None], (bsz, N_META, d))
    h = jnp.concatenate([meta, x], axis=1).reshape(bsz * length, d)
    dft = _dft_matrices(length)
    kw = dict(bsz=bsz, length=length)
    w_in, w_na_out, w_sc_out, w_hy_out, w_o, ffn_w1, ffn_w3, ffn_w2 = map(
        to_bf16, (w_in, w_na_out, w_sc_out, w_hy_out, w_o, ffn_w1, ffn_w3, ffn_w2))
    for layer in range(depth):
        proj = norm_matmul(h, norm_mix[layer], w_in[layer], tn=2048)
        y_na = neighbourhood_attention(proj, na_rpb[layer], na_meta_bias[layer], **kw)
        y_sc = short_conv_mixer(proj, sc_conv[layer], **kw)
        y_hy = hyena_mixer(proj, hy_conv[layer], hy_w1[layer], hy_b1[layer], hy_w2[layer], hy_b2[layer],
                           hy_w3[layer], hy_freq[layer], hy_decay[layer], hy_skip[layer], dft, **kw)
        h = mixer_out(h, y_na, y_sc, y_hy, proj, w_na_out[layer], w_sc_out[layer], w_hy_out[layer], w_o[layer])
        j = layer // 2
        if layer % 2 == 0:
            h = dense_ffn(h, norm_ffn[layer], ffn_w1[j], ffn_w3[j], ffn_w2[j])
        else:
            g_final = norm_final if layer == depth - 1 else None
            h = moe_swiglu(h, norm_ffn[layer], router_w[j], moe_w1[j], moe_w3[j], moe_w2[j], g_final, **kw)
    if depth % 2 == 0:
        return h
    out = rms_norm_rows(h, norm_final, out_dtype=x.dtype)
    return out.reshape(bsz, length, d)[:, N_META:]
```

```python
import functools
import math

import numpy as np
import jax
import jax.numpy as jnp
from jax import lax
from jax.experimental import pallas as pl
from jax.experimental.pallas import tpu as pltpu

D_MODEL = 2048
N_META = 16
GRID_W = 64
GRID_ROWS = 32
NA_HEADS = 16
NA_HEAD_DIM = 64
NA_WIDTH = NA_HEADS * NA_HEAD_DIM
WIN_ROWS = 8
WIN_COLS = 16
SC_WIDTH = 512
HY_WIDTH = 512
HY_ORDER = 2
HY_EMB = 33
HY_HIDDEN = 64
HY_MOD_SHIFT = 0.05
PROJ_WIDTH = 3 * NA_WIDTH + 3 * SC_WIDTH + 3 * HY_WIDTH + 3 * D_MODEL
D_FF = 5632
N_EXPERTS = 8
TOP_K = 2
D_FF_EXPERT = 7168
MOE_BLOCK = 528
NORM_EPS = 1e-6
NEG_INF = -1e30

COL_Q, COL_K, COL_V = 0, NA_WIDTH, 2 * NA_WIDTH
COL_SCB = 3 * NA_WIDTH
COL_SCC = COL_SCB + SC_WIDTH
COL_SCU = COL_SCC + SC_WIDTH
COL_HYU = COL_SCU + SC_WIDTH
COL_GNA = COL_HYU + 3 * HY_WIDTH
COL_GSC = COL_GNA + D_MODEL
COL_GHY = COL_GSC + D_MODEL

V7X_VMEM_BYTES = 64 * 1024 * 1024
VMEM_LIMIT = V7X_VMEM_BYTES - 8 * 1024 * 1024

ROW_TILE = 768
NORM_CHUNK = 16
FFT_N = 2 * (N_META + GRID_ROWS * GRID_W)
FFT_KP = 2080
FFT_KB = 1040
INV_NB = 688

F32 = jnp.float32
BF16 = jnp.bfloat16


def _params(sem, vmem_limit=VMEM_LIMIT):
    return pltpu.CompilerParams(dimension_semantics=sem, vmem_limit_bytes=vmem_limit)


def _sigmoid(a):
    return 0.5 * jnp.tanh(0.5 * a) + 0.5


CAST_BLOCK_BYTES = 8 * 1024 * 1024


def _cast_kernel(x_ref, o_ref):
    o_ref[...] = x_ref[...].astype(o_ref.dtype)


def to_bf16(w):
    cols = w.shape[-1]
    w2 = w.reshape(-1, cols)
    rows = w2.shape[0]
    tr = min(rows, CAST_BLOCK_BYTES // (4 * cols)) // 16 * 16
    while rows % tr:
        tr -= 16
    out = pl.pallas_call(
        _cast_kernel,
        out_shape=jax.ShapeDtypeStruct((rows, cols), BF16),
        grid=(rows // tr,),
        in_specs=[pl.BlockSpec((tr, cols), lambda i: (i, 0))],
        out_specs=pl.BlockSpec((tr, cols), lambda i: (i, 0)),
        compiler_params=_params(("parallel",)),
        name="to_bf16",
    )(w2)
    return out.reshape(w.shape)


def _rms_norm_rows(x_ref, g_ref, xn_ref):
    rows = x_ref.shape[0]
    g = g_ref[...]

    def chunk(c, carry):
        r0 = pl.multiple_of(c * NORM_CHUNK, NORM_CHUNK)
        x = x_ref[pl.ds(r0, NORM_CHUNK), :]
        ms = jnp.mean(x * x, axis=-1, keepdims=True)
        xn_ref[pl.ds(r0, NORM_CHUNK), :] = (x * lax.rsqrt(ms + NORM_EPS) * g).astype(xn_ref.dtype)
        return carry

    lax.fori_loop(0, rows // NORM_CHUNK, chunk, 0, unroll=4)


def _norm_mm_kernel(x_ref, g_ref, w_ref, o_ref, xn_ref):
    @pl.when(pl.program_id(1) == 0)
    def _():
        _rms_norm_rows(x_ref, g_ref, xn_ref)

    o_ref[...] = jnp.dot(xn_ref[...], w_ref[...], preferred_element_type=F32).astype(o_ref.dtype)


def norm_matmul(x, g, w, *, tn, out_dtype=BF16, tm=ROW_TILE):
    t, k = x.shape
    n = w.shape[1]
    return pl.pallas_call(
        _norm_mm_kernel,
        out_shape=jax.ShapeDtypeStruct((t, n), out_dtype),
        grid=(t // tm, n // tn),
        in_specs=[
            pl.BlockSpec((tm, k), lambda i, j: (i, 0)),
            pl.BlockSpec((1, k), lambda i, j: (0, 0)),
            pl.BlockSpec((k, tn), lambda i, j: (0, j)),
        ],
        out_specs=pl.BlockSpec((tm, tn), lambda i, j: (i, j)),
        scratch_shapes=[pltpu.VMEM((tm, k), BF16)],
        compiler_params=_params(("parallel", "arbitrary")),
        name="norm_matmul",
    )(x, g.reshape(1, k), w)


def _norm_router_kernel(x_ref, g_ref, w_ref, o_ref, xb_ref, xn_ref):
    _rms_norm_rows(x_ref, g_ref, xn_ref)
    xn = xn_ref[...]
    x_hi = xn.astype(BF16)
    x_lo = (xn - x_hi.astype(F32)).astype(BF16)
    w = w_ref[...]
    w_hi = w.astype(BF16)
    w_lo = (w - w_hi.astype(F32)).astype(BF16)
    xb_ref[...] = x_hi
    logits = jnp.dot(x_hi, w_hi, preferred_element_type=F32)
    logits += jnp.dot(x_lo, w_hi, preferred_element_type=F32)
    logits += jnp.dot(x_hi, w_lo, preferred_element_type=F32)
    o_ref[...] = logits


def norm_router(x, g, w, *, tm=ROW_TILE):
    t, k = x.shape
    n = w.shape[1]
    return pl.pallas_call(
        _norm_router_kernel,
        out_shape=(jax.ShapeDtypeStruct((t, n), F32), jax.ShapeDtypeStruct((t, k), BF16)),
        grid=(t // tm,),
        in_specs=[
            pl.BlockSpec((tm, k), lambda i: (i, 0)),
            pl.BlockSpec((1, k), lambda i: (0, 0)),
            pl.BlockSpec((k, n), lambda i: (0, 0)),
        ],
        out_specs=(pl.BlockSpec((tm, n), lambda i: (i, 0)), pl.BlockSpec((tm, k), lambda i: (i, 0))),
        scratch_shapes=[pltpu.VMEM((tm, k), F32)],
        compiler_params=_params(("parallel",)),
        name="norm_router",
    )(x, g.reshape(1, k), w)


def _norm_kernel(x_ref, g_ref, o_ref):
    _rms_norm_rows(x_ref, g_ref, o_ref)


def rms_norm_rows(x, g, *, out_dtype, tm=ROW_TILE):
    t, k = x.shape
    return pl.pallas_call(
        _norm_kernel,
        out_shape=jax.ShapeDtypeStruct((t, k), out_dtype),
        grid=(t // tm,),
        in_specs=[pl.BlockSpec((tm, k), lambda i: (i, 0)), pl.BlockSpec((1, k), lambda i: (0, 0))],
        out_specs=pl.BlockSpec((tm, k), lambda i: (i, 0)),
        compiler_params=_params(("parallel",)),
        name="rms_norm",
    )(x, g.reshape(1, k))


def _dense_ffn_kernel(x_ref, g_ref, w1_ref, w3_ref, w2_ref, o_ref, xn_ref):
    @pl.when(pl.program_id(1) == 0)
    def _():
        _rms_norm_rows(x_ref, g_ref, xn_ref)
        o_ref[...] = x_ref[...]

    xn = xn_ref[...]
    a = jnp.dot(xn, w1_ref[...], preferred_element_type=F32)
    b = jnp.dot(xn, w3_ref[...], preferred_element_type=F32)
    hid = (a * _sigmoid(a) * b).astype(BF16)
    o_ref[...] += jnp.dot(hid, w2_ref[...], preferred_element_type=F32)


def dense_ffn(x, g, w1, w3, w2, *, fc=512, tm=ROW_TILE):
    t, d = x.shape
    f = w1.shape[1]
    return pl.pallas_call(
        _dense_ffn_kernel,
        out_shape=jax.ShapeDtypeStruct((t, d), F32),
        grid=(t // tm, f // fc),
        in_specs=[
            pl.BlockSpec((tm, d), lambda i, j: (i, 0)),
            pl.BlockSpec((1, d), lambda i, j: (0, 0)),
            pl.BlockSpec((d, fc), lambda i, j: (0, j)),
            pl.BlockSpec((d, fc), lambda i, j: (0, j)),
            pl.BlockSpec((fc, d), lambda i, j: (j, 0)),
        ],
        out_specs=pl.BlockSpec((tm, d), lambda i, j: (i, 0)),
        scratch_shapes=[pltpu.VMEM((tm, d), BF16)],
        compiler_params=_params(("parallel", "arbitrary")),
        name="dense_ffn",
    )(x, g.reshape(1, d), w1, w3, w2)


def _mixer_out_kernel(res_ref, yna_ref, ysc_ref, yhy_ref, gna_ref, gsc_ref, ghy_ref, wna_ref, wsc_ref, why_ref,
                      wo_ref, o_ref):
    @pl.when(pl.program_id(1) == 0)
    def _():
        o_ref[...] = res_ref[...]

    m = _sigmoid(gna_ref[...].astype(F32)) * jnp.dot(yna_ref[...], wna_ref[...], preferred_element_type=F32)
    m += _sigmoid(gsc_ref[...].astype(F32)) * jnp.dot(ysc_ref[...], wsc_ref[...], preferred_element_type=F32)
    m += _sigmoid(ghy_ref[...].astype(F32)) * jnp.dot(yhy_ref[...], why_ref[...], preferred_element_type=F32)
    o_ref[...] += jnp.dot(m.astype(BF16), wo_ref[...], preferred_element_type=F32)


MIXER_ROWS = 256


def mixer_out(res, y_na, y_sc, y_hy, proj, w_na, w_sc, w_hy, w_o, *, cw=D_MODEL, tm=MIXER_ROWS):
    t, d = res.shape
    return pl.pallas_call(
        _mixer_out_kernel,
        out_shape=jax.ShapeDtypeStruct((t, d), F32),
        grid=(t // tm, d // cw),
        in_specs=[
            pl.BlockSpec((tm, d), lambda i, c: (i, 0)),
            pl.BlockSpec((tm, NA_WIDTH), lambda i, c: (i, 0)),
            pl.BlockSpec((tm, SC_WIDTH), lambda i, c: (i, 0)),
            pl.BlockSpec((tm, HY_WIDTH), lambda i, c: (i, 0)),
            pl.BlockSpec((tm, cw), lambda i, c: (i, COL_GNA // cw + c)),
            pl.BlockSpec((tm, cw), lambda i, c: (i, COL_GSC // cw + c)),
            pl.BlockSpec((tm, cw), lambda i, c: (i, COL_GHY // cw + c)),
            pl.BlockSpec((NA_WIDTH, cw), lambda i, c: (0, c)),
            pl.BlockSpec((SC_WIDTH, cw), lambda i, c: (0, c)),
            pl.BlockSpec((HY_WIDTH, cw), lambda i, c: (0, c)),
            pl.BlockSpec((cw, d), lambda i, c: (c, 0)),
        ],
        out_specs=pl.BlockSpec((tm, d), lambda i, c: (i, 0)),
        compiler_params=_params(("parallel", "arbitrary")),
        name="mixer_out",
    )(res, y_na, y_sc, y_hy, proj, proj, proj, w_na, w_sc, w_hy, w_o)


def _na_bias_table(rpb):
    qc = np.arange(GRID_W)[:, None]
    kc = np.arange(GRID_W)[None, :]
    win_start = np.clip(qc - WIN_COLS // 2, 0, GRID_W - WIN_COLS)
    inside = (kc >= win_start) & (kc < win_start + WIN_COLS)
    dc = np.clip(kc - qc, -(WIN_COLS - 1), WIN_COLS - 1) + WIN_COLS - 1
    dr = np.arange(WIN_ROWS)[None, :] - np.arange(WIN_ROWS)[:, None] + WIN_ROWS - 1
    onehot = np.zeros((2 * WIN_COLS - 1, GRID_W * GRID_W), np.float32)
    onehot[dc.reshape(-1), np.arange(GRID_W * GRID_W)] = 1.0
    tbl = jnp.einsum('htid,dj->htij', rpb.astype(F32)[:, dr], jnp.asarray(onehot),
                     precision=lax.Precision.HIGHEST)
    tbl = tbl.reshape(NA_HEADS, WIN_ROWS, WIN_ROWS, GRID_W, GRID_W)
    tbl = jnp.where(jnp.asarray(inside)[None, None, None], tbl, NEG_INF)
    tbl = tbl.transpose(0, 1, 3, 2, 4)
    return tbl.reshape(NA_HEADS, WIN_ROWS, GRID_W, WIN_ROWS * GRID_W)


NA_SOFTMAX_LAG = 1
NA_PV_LAG = 2


def _na_kernel(q_ref, k_ref, v_ref, tbl_ref, mb_ref, o_ref):
    scale = NA_HEAD_DIM ** -0.5
    lane = lax.broadcasted_iota(jnp.int32, (1, 2 * NA_HEAD_DIM), 1)
    first = lane < NA_HEAD_DIM
    nt = (((1,), (1,)), ((), ()))
    k_meta = k_ref[0:N_META, :]
    v_meta = v_ref[0:N_META, :]
    mb_grid = mb_ref[0, 0]
    band = WIN_ROWS * GRID_W

    def stack_heads(q):
        q = q * scale
        zero = jnp.zeros_like(q)
        return jnp.concatenate([jnp.where(first, q, zero), jnp.where(first, zero, q)], axis=0)

    def unstack_heads(out, den, n):
        out = out / den
        return jnp.where(first, out[:n], out[n:])

    def softmax_parts(s_grid, s_meta):
        m = jnp.max(s_meta, axis=-1, keepdims=True)
        if s_grid is not None:
            m = jnp.maximum(m, jnp.max(s_grid, axis=-1, keepdims=True))
        p_meta = jnp.exp(s_meta - m)
        den = jnp.sum(p_meta, axis=-1, keepdims=True)
        p_grid = None
        if s_grid is not None:
            p_grid = jnp.exp(s_grid - m)
            den = den + jnp.sum(p_grid, axis=-1, keepdims=True)
            p_grid = p_grid.astype(BF16)
        return p_grid, p_meta.astype(BF16), den

    def geometry(r):
        rs = min(max(r - WIN_ROWS // 2, 0), GRID_ROWS - WIN_ROWS)
        return N_META + r * GRID_W, N_META + rs * GRID_W, r - rs

    def scores(r):
        if r is None:
            q2 = stack_heads(q_ref[0:N_META, :])
            return None, lax.dot_general(q2, k_meta, nt, preferred_element_type=F32) + mb_ref[0, 1, 0:2 * N_META, :]
        q0, k0, typ = geometry(r)
        q2 = stack_heads(q_ref[q0:q0 + GRID_W, :])
        s_grid = lax.dot_general(q2, k_ref[k0:k0 + band, :], nt, preferred_element_type=F32) + tbl_ref[0, typ]
        s_meta = lax.dot_general(q2, k_meta, nt, preferred_element_type=F32) + mb_grid
        return s_grid, s_meta

    def weighted_values(r, p_grid, p_meta, den):
        out = jnp.dot(p_meta, v_meta, preferred_element_type=F32)
        if r is None:
            o_ref[0:N_META, :] = unstack_heads(out, den, N_META).astype(o_ref.dtype)
            return
        q0, k0, _ = geometry(r)
        out += jnp.dot(p_grid, v_ref[k0:k0 + band, :], preferred_element_type=F32)
        o_ref[q0:q0 + GRID_W, :] = unstack_heads(out, den, GRID_W).astype(o_ref.dtype)

    units = [None] + list(range(GRID_ROWS))
    pending_scores, pending_probs = {}, {}
    for t in range(len(units) + NA_PV_LAG):
        if t < len(units):
            pending_scores[t] = scores(units[t])
        if 0 <= t - NA_SOFTMAX_LAG < len(units):
            pending_probs[t - NA_SOFTMAX_LAG] = softmax_parts(*pending_scores.pop(t - NA_SOFTMAX_LAG))
        if 0 <= t - NA_PV_LAG < len(units):
            weighted_values(units[t - NA_PV_LAG], *pending_probs.pop(t - NA_PV_LAG))


def neighbourhood_attention(proj, rpb, meta_bias, *, bsz, length):
    hp = 2 * NA_HEAD_DIM
    n_pairs = NA_HEADS // 2
    band = WIN_ROWS * GRID_W
    tbl = _na_bias_table(rpb).reshape(n_pairs, 2, WIN_ROWS, GRID_W, band).transpose(0, 2, 1, 3, 4)
    tbl = tbl.reshape(n_pairs, WIN_ROWS, 2 * GRID_W, band)
    mb = meta_bias.astype(F32).reshape(n_pairs, 2, 1, N_META)
    mb_grid = jnp.broadcast_to(mb, (n_pairs, 2, GRID_W, N_META)).reshape(n_pairs, 2 * GRID_W, N_META)
    mb_meta = jnp.broadcast_to(mb, (n_pairs, 2, N_META, N_META)).reshape(n_pairs, 2 * N_META, N_META)
    mb_meta = jnp.pad(mb_meta, ((0, 0), (0, 2 * GRID_W - 2 * N_META), (0, 0)))
    mb2 = jnp.stack([mb_grid, mb_meta], axis=1)
    return pl.pallas_call(
        _na_kernel,
        out_shape=jax.ShapeDtypeStruct((bsz * length, NA_WIDTH), BF16),
        grid=(n_pairs, bsz),
        in_specs=[
            pl.BlockSpec((length, hp), lambda h, b: (b, COL_Q // hp + h)),
            pl.BlockSpec((length, hp), lambda h, b: (b, COL_K // hp + h)),
            pl.BlockSpec((length, hp), lambda h, b: (b, COL_V // hp + h)),
            pl.BlockSpec((1, WIN_ROWS, 2 * GRID_W, band), lambda h, b: (h, 0, 0, 0)),
            pl.BlockSpec((1, 2, 2 * GRID_W, N_META), lambda h, b: (h, 0, 0, 0)),
        ],
        out_specs=pl.BlockSpec((length, hp), lambda h, b: (b, h)),
        compiler_params=_params(("parallel", "parallel")),
        name="neighbourhood_attention",
    )(proj, proj, proj, tbl, mb2)


def _dwconv3(u, w):
    length = u.shape[0]
    row = lax.broadcasted_iota(jnp.int32, u.shape, 0)
    prev = jnp.where(row == 0, 0.0, pltpu.roll(u, 1, 0))
    nxt = jnp.where(row == length - 1, 0.0, pltpu.roll(u, length - 1, 0))
    return prev * w[0:1] + u * w[1:2] + nxt * w[2:3]


def _gated_conv_kernel(b_ref, c_ref, u_ref, w_ref, o_ref):
    u = c_ref[...].astype(F32) * u_ref[...].astype(F32)
    o_ref[...] = (b_ref[...].astype(F32) * _dwconv3(u, w_ref[...])).astype(o_ref.dtype)


def _plain_conv_kernel(u_ref, w_ref, o_ref):
    o_ref[...] = _dwconv3(u_ref[...].astype(F32), w_ref[...]).astype(o_ref.dtype)


CONV_CB = 512


def short_conv_mixer(proj, w_conv, *, bsz, length):
    cb = CONV_CB
    return pl.pallas_call(
        _gated_conv_kernel,
        out_shape=jax.ShapeDtypeStruct((bsz * length, SC_WIDTH), BF16),
        grid=(bsz, SC_WIDTH // cb),
        in_specs=[
            pl.BlockSpec((length, cb), lambda b, c: (b, COL_SCB // cb + c)),
            pl.BlockSpec((length, cb), lambda b, c: (b, COL_SCC // cb + c)),
            pl.BlockSpec((length, cb), lambda b, c: (b, COL_SCU // cb + c)),
            pl.BlockSpec((3, cb), lambda b, c: (0, c)),
        ],
        out_specs=pl.BlockSpec((length, cb), lambda b, c: (b, c)),
        compiler_params=_params(("parallel", "parallel")),
        name="short_conv_mixer",
    )(proj, proj, proj, w_conv.astype(F32))


def hyena_in_conv(proj, w_conv, *, bsz, length):
    cb = CONV_CB
    return pl.pallas_call(
        _plain_conv_kernel,
        out_shape=jax.ShapeDtypeStruct((bsz * length, 3 * HY_WIDTH), BF16),
        grid=(bsz, 3 * HY_WIDTH // cb),
        in_specs=[
            pl.BlockSpec((length, cb), lambda b, c: (b, COL_HYU // cb + c)),
            pl.BlockSpec((3, cb), lambda b, c: (0, c)),
        ],
        out_specs=pl.BlockSpec((length, cb), lambda b, c: (b, c)),
        compiler_params=_params(("parallel", "parallel")),
        name="hyena_in_conv",
    )(proj, w_conv.astype(F32))


def _filter_kernel(z_ref, w1_ref, b1_ref, w2_ref, b2_ref, w3_ref, fr_ref, dec_ref, f_ref, s_ref):
    hi = lax.Precision.HIGHEST
    z = z_ref[...]
    fr = fr_ref[...]
    h = jnp.sin(fr * (jnp.dot(z, w1_ref[...], preferred_element_type=F32, precision=hi) + b1_ref[...]))
    h = jnp.sin(fr * (jnp.dot(h, w2_ref[...], preferred_element_type=F32, precision=hi) + b2_ref[...]))
    h = jnp.dot(h, w3_ref[...], preferred_element_type=F32, precision=hi)
    t = z[:, 0:1]
    f = h * (jnp.exp(-t * jnp.abs(dec_ref[0])) + HY_MOD_SHIFT)
    row = lax.broadcasted_iota(jnp.int32, f.shape, 0)
    backward = pl.program_id(0) % 2 == 1
    f = jnp.where(jnp.logical_and(row == 0, backward), 0.0, f)
    f_ref[0] = f.astype(f_ref.dtype)
    s_ref[0] = jnp.sum(jnp.abs(f), axis=0, keepdims=True)


def hyena_filters(length, w1, b1, w2, b2, w3, freq, decay):
    bands = (HY_EMB - 1) // 2
    t = jnp.linspace(0.0, 1.0, length, dtype=F32)[:, None]
    ang = (2.0 * math.pi / length) * jnp.arange(length, dtype=F32)[:, None]
    fb = jnp.linspace(1e-4, bands - 1, bands, dtype=F32)[None, :]
    z = jnp.concatenate([t, jnp.cos(fb * ang), -jnp.sin(fb * ang)], axis=-1)
    emb = 128
    z = jnp.pad(z, ((0, 0), (0, emb - HY_EMB)))
    w1p = jnp.pad(w1.astype(F32), ((0, emb - HY_EMB), (0, 0)))
    n_slab = HY_ORDER * 2
    row = lambda a: a.astype(F32).reshape(1, -1)
    return pl.pallas_call(
        _filter_kernel,
        out_shape=(jax.ShapeDtypeStruct((n_slab, length, HY_WIDTH), BF16),
                   jax.ShapeDtypeStruct((n_slab, 1, HY_WIDTH), F32)),
        grid=(n_slab,),
        in_specs=[
            pl.BlockSpec((length, emb), lambda j: (0, 0)),
            pl.BlockSpec((emb, HY_HIDDEN), lambda j: (0, 0)),
            pl.BlockSpec((1, HY_HIDDEN), lambda j: (0, 0)),
            pl.BlockSpec((HY_HIDDEN, HY_HIDDEN), lambda j: (0, 0)),
            pl.BlockSpec((1, HY_HIDDEN), lambda j: (0, 0)),
            pl.BlockSpec((HY_HIDDEN, HY_WIDTH), lambda j: (0, j)),
            pl.BlockSpec((1, HY_HIDDEN), lambda j: (0, 0)),
            pl.BlockSpec((1, 1, HY_WIDTH), lambda j: (j, 0, 0)),
        ],
        out_specs=(pl.BlockSpec((1, length, HY_WIDTH), lambda j: (j, 0, 0)),
                   pl.BlockSpec((1, 1, HY_WIDTH), lambda j: (j, 0, 0))),
        compiler_params=_params(("parallel",)),
        name="hyena_filters",
    )(z, w1p, row(b1), w2.astype(F32), row(b2), w3.astype(F32), row(freq),
      decay.astype(F32).reshape(n_slab, 1, HY_WIDTH))


def _dft_matrices(length):
    k = jnp.arange(FFT_KP, dtype=jnp.int32)[:, None]
    n = jnp.arange(length, dtype=jnp.int32)[None, :]
    theta = (2.0 * math.pi / FFT_N) * ((k * n) % FFT_N).astype(F32)
    valid = k <= FFT_N // 2
    fc = jnp.where(valid, jnp.cos(theta), 0.0).astype(BF16)
    fs = jnp.where(valid, -jnp.sin(theta), 0.0).astype(BF16)
    fc, fs = lax.optimization_barrier((fc, fs))
    return fc, fs, fc.T, fs.T


def _dft_raw_kernel(fc_ref, fs_ref, u_ref, re_ref, im_ref):
    u = u_ref[0]
    re_ref[0] = jnp.dot(fc_ref[...], u, preferred_element_type=F32)
    im_ref[0] = jnp.dot(fs_ref[...], u, preferred_element_type=F32)


def dft_filters(fc, fs, filt):
    n_slab, length, c = filt.shape
    out = jax.ShapeDtypeStruct((n_slab, FFT_KP, c), F32)
    return pl.pallas_call(
        _dft_raw_kernel,
        out_shape=(out, out),
        grid=(FFT_KP // FFT_KB, n_slab),
        in_specs=[
            pl.BlockSpec((FFT_KB, length), lambda kb, s: (kb, 0)),
            pl.BlockSpec((FFT_KB, length), lambda kb, s: (kb, 0)),
            pl.BlockSpec((1, length, c), lambda kb, s: (s, 0, 0)),
        ],
        out_specs=(pl.BlockSpec((1, FFT_KB, c), lambda kb, s: (s, kb, 0)),
                   pl.BlockSpec((1, FFT_KB, c), lambda kb, s: (s, kb, 0))),
        compiler_params=_params(("parallel", "parallel")),
        name="dft_filters",
    )(fc, fs, filt)


def _dft_mul_kernel(fc_ref, fs_ref, u_ref, are_ref, aim_ref, s_ref, yre_ref, yim_ref, g_ref):
    @pl.when(pl.program_id(1) == 0)
    def _():
        k = pl.program_id(0) * FFT_KB + lax.broadcasted_iota(jnp.int32, (FFT_KB, 1), 0)
        wk = jnp.where(jnp.logical_or(k == 0, k == FFT_N // 2), 1.0, 2.0)
        wk = jnp.where(k > FFT_N // 2, 0.0, wk) / FFT_N
        inv = wk / (s_ref[0] + s_ref[1])
        g_ref[0] = (are_ref[0] + are_ref[1]) * inv
        g_ref[1] = (aim_ref[0] - aim_ref[1]) * inv

    u = u_ref[...]
    re = jnp.dot(fc_ref[...], u, preferred_element_type=F32)
    im = jnp.dot(fs_ref[...], u, preferred_element_type=F32)
    gre = g_ref[0]
    gim = g_ref[1]
    yre_ref[0] = (re * gre - im * gim).astype(yre_ref.dtype)
    yim_ref[0] = (re * gim + im * gre).astype(yim_ref.dtype)


def dft_forward_mul(fc, fs, u, col, a_re, a_im, abs_sum, order, *, bsz, length):
    c = HY_WIDTH
    out = jax.ShapeDtypeStruct((bsz, FFT_KP, c), BF16)
    return pl.pallas_call(
        _dft_mul_kernel,
        out_shape=(out, out),
        grid=(FFT_KP // FFT_KB, bsz),
        in_specs=[
            pl.BlockSpec((FFT_KB, length), lambda kb, b: (kb, 0)),
            pl.BlockSpec((FFT_KB, length), lambda kb, b: (kb, 0)),
            pl.BlockSpec((length, c), lambda kb, b: (b, col)),
            pl.BlockSpec((2, FFT_KB, c), lambda kb, b: (order, kb, 0)),
            pl.BlockSpec((2, FFT_KB, c), lambda kb, b: (order, kb, 0)),
            pl.BlockSpec((2, 1, c), lambda kb, b: (order, 0, 0)),
        ],
        out_specs=(pl.BlockSpec((1, FFT_KB, c), lambda kb, b: (b, kb, 0)),
                   pl.BlockSpec((1, FFT_KB, c), lambda kb, b: (b, kb, 0))),
        scratch_shapes=[pltpu.VMEM((2, FFT_KB, c), F32)],
        compiler_params=_params(("parallel", "arbitrary")),
        name="dft_forward_mul",
    )(fc, fs, u, a_re, a_im, abs_sum)


def _dft_inv_kernel(ct_ref, st_ref, yre_ref, yim_ref, u_ref, gate_ref, skip_ref, o_ref):
    y = jnp.dot(ct_ref[...], yre_ref[0], preferred_element_type=F32)
    y += jnp.dot(st_ref[...], yim_ref[0], preferred_element_type=F32)
    y += u_ref[...].astype(F32) * skip_ref[0]
    o_ref[...] = (gate_ref[...].astype(F32) * y).astype(o_ref.dtype)


def dft_inverse_gate(fct, fst, yre, yim, u, u_col, gate, gate_col, skip, order, *, bsz, length):
    c = HY_WIDTH
    nblk = length // INV_NB
    return pl.pallas_call(
        _dft_inv_kernel,
        out_shape=jax.ShapeDtypeStruct((bsz * length, c), BF16),
        grid=(nblk, bsz),
        in_specs=[
            pl.BlockSpec((INV_NB, FFT_KP), lambda nb, b: (nb, 0)),
            pl.BlockSpec((INV_NB, FFT_KP), lambda nb, b: (nb, 0)),
            pl.BlockSpec((1, FFT_KP, c), lambda nb, b: (b, 0, 0)),
            pl.BlockSpec((1, FFT_KP, c), lambda nb, b: (b, 0, 0)),
            pl.BlockSpec((INV_NB, c), lambda nb, b: (b * nblk + nb, u_col)),
            pl.BlockSpec((INV_NB, c), lambda nb, b: (b * nblk + nb, gate_col)),
            pl.BlockSpec((1, 1, c), lambda nb, b: (order, 0, 0)),
        ],
        out_specs=pl.BlockSpec((INV_NB, c), lambda nb, b: (b * nblk + nb, 0)),
        compiler_params=_params(("parallel", "parallel")),
        name="dft_inverse_gate",
    )(fct, fst, yre, yim, u, gate, skip)


def hyena_mixer(proj, w_conv, w1, b1, w2, b2, w3, freq, decay, skip, dft, *, bsz, length):
    fc, fs, fct, fst = dft
    vxx = hyena_in_conv(proj, w_conv, bsz=bsz, length=length)
    filt, abs_sum = hyena_filters(length, w1, b1, w2, b2, w3, freq, decay)
    a_re, a_im = dft_filters(fc, fs, filt)
    skip = skip.astype(F32).reshape(HY_ORDER, 1, HY_WIDTH)
    kw = dict(bsz=bsz, length=length)
    yre, yim = dft_forward_mul(fc, fs, vxx, 0, a_re, a_im, abs_sum, 0, **kw)
    z = dft_inverse_gate(fct, fst, yre, yim, vxx, 0, vxx, 1, skip, 0, **kw)
    yre, yim = dft_forward_mul(fc, fs, z, 0, a_re, a_im, abs_sum, 1, **kw)
    return dft_inverse_gate(fct, fst, yre, yim, z, 0, vxx, 2, skip, 1, **kw)


MOE_SUPER = 2 * MOE_BLOCK
MOE_FC = 512
MOE_VMEM_LIMIT = V7X_VMEM_BYTES - 3 * 1024 * 1024


def _moe_kernel(be_ref, nh_ref, x_ref, w1_ref, w3_ref, w2_ref, yb_ref, o_ref, acc_ref):
    del be_ref, yb_ref
    i = pl.program_id(0)
    j = pl.program_id(1)
    halves = nh_ref[i]

    @pl.when(j == 0)
    def _():
        acc_ref[...] = jnp.zeros_like(acc_ref)

    def ffn(rows):
        x = x_ref[0:rows, :]
        a = jnp.dot(x, w1_ref[0].astype(BF16), preferred_element_type=F32)
        b = jnp.dot(x, w3_ref[0].astype(BF16), preferred_element_type=F32)
        hid = (a * _sigmoid(a) * b).astype(BF16)
        acc_ref[0:rows, :] += jnp.dot(hid, w2_ref[0].astype(BF16), preferred_element_type=F32)

    @pl.when(halves == 2)
    def _():
        ffn(MOE_SUPER)

    @pl.when(halves == 1)
    def _():
        ffn(MOE_BLOCK)

    @pl.when(j == pl.num_programs(1) - 1)
    def _():
        o_ref[...] = acc_ref[...].astype(o_ref.dtype)


def moe_experts(xb, sb_start, block_expert, n_halves, w1, w3, w2, yb):
    d = xb.shape[1]
    n_super = xb.shape[0] // MOE_SUPER
    fc = MOE_FC
    n_fc = D_FF_EXPERT // fc

    def chunk(i, j, nh):
        return jnp.where(nh[i] > 0, j, n_fc - 1)

    grid_spec = pltpu.PrefetchScalarGridSpec(
        num_scalar_prefetch=2,
        grid=(n_super, n_fc),
        in_specs=[
            pl.BlockSpec((MOE_SUPER, d), lambda i, j, be, nh: (i, 0)),
            pl.BlockSpec((1, d, fc), lambda i, j, be, nh: (be[i], 0, chunk(i, j, nh))),
            pl.BlockSpec((1, d, fc), lambda i, j, be, nh: (be[i], 0, chunk(i, j, nh))),
            pl.BlockSpec((1, fc, d), lambda i, j, be, nh: (be[i], chunk(i, j, nh), 0)),
            pl.BlockSpec(memory_space=pl.ANY),
        ],
        out_specs=pl.BlockSpec((MOE_SUPER, d), lambda i, j, be, nh: (sb_start + i, 0)),
        scratch_shapes=[pltpu.VMEM((MOE_SUPER, d), F32)],
    )
    return pl.pallas_call(
        _moe_kernel,
        out_shape=jax.ShapeDtypeStruct(yb.shape, yb.dtype),
        grid_spec=grid_spec,
        input_output_aliases={6: 0},
        compiler_params=_params(("parallel", "arbitrary"), MOE_VMEM_LIMIT),
        name="moe_experts",
    )(block_expert, n_halves, xb, w1, w3, w2, yb)


COMBINE_ROWS = 256


def _moe_combine_kernel(h_ref, y0_ref, y1_ref, g0_ref, g1_ref, gn_ref, o_ref, *, final):
    rows = o_ref.shape[0]
    gn = gn_ref[...]

    def chunk(c, carry):
        r0 = pl.multiple_of(c * NORM_CHUNK, NORM_CHUNK)
        sl = pl.ds(r0, NORM_CHUNK)
        h = h_ref[0, sl, :] if final else h_ref[sl, :]
        y = h + g0_ref[sl, :] * y0_ref[sl, :].astype(F32) + g1_ref[sl, :] * y1_ref[sl, :].astype(F32)
        if final:
            ms = jnp.mean(y * y, axis=-1, keepdims=True)
            y = y * lax.rsqrt(ms + NORM_EPS) * gn
        o_ref[sl, :] = y.astype(o_ref.dtype)
        return carry

    lax.fori_loop(0, rows // NORM_CHUNK, chunk, 0, unroll=4)


def moe_combine(h, y0, y1, g0, g1, *, tm=ROW_TILE):
    t, d = h.shape
    row = lambda w: pl.BlockSpec((tm, w), lambda i: (i, 0))
    return pl.pallas_call(
        functools.partial(_moe_combine_kernel, final=False),
        out_shape=jax.ShapeDtypeStruct((t, d), F32),
        grid=(t // tm,),
        in_specs=[row(d), row(d), row(d), row(1), row(1), pl.BlockSpec((1, d), lambda i: (0, 0))],
        out_specs=row(d),
        compiler_params=_params(("parallel",)),
        name="moe_combine",
    )(h, y0, y1, g0, g1, jnp.ones((1, d), F32))


def moe_combine_final(h, y0, y1, g0, g1, g_final, *, bsz, length):
    d = h.shape[1]
    seq = length - N_META
    r = COMBINE_ROWS
    nr = seq // r
    row = lambda w: pl.BlockSpec((r, w), lambda b, i: (b * nr + i, 0))
    h_spec = pl.BlockSpec((pl.Element(1), pl.Element(r), pl.Element(d)),
                          lambda b, i: (b, pl.multiple_of(N_META + i * r, N_META), 0))
    return pl.pallas_call(
        functools.partial(_moe_combine_kernel, final=True),
        out_shape=jax.ShapeDtypeStruct((bsz, seq, d), F32),
        grid=(bsz, nr),
        in_specs=[h_spec, row(d), row(d), row(1), row(1), pl.BlockSpec((1, d), lambda b, i: (0, 0))],
        out_specs=pl.BlockSpec((None, r, d), lambda b, i: (b, i, 0)),
        compiler_params=_params(("parallel", "parallel")),
        name="moe_combine_final",
    )(h.reshape(bsz, length, d), y0, y1, g0, g1, g_final.astype(F32).reshape(1, d))


SCAN_CHUNK = 256
MOE_CALL_SPLIT = (4, 16, None)


def _expert_ranks(expert):
    n = expert.shape[0]
    onehot = (expert[:, None] == jnp.arange(N_EXPERTS, dtype=jnp.int32)[None, :]).astype(F32)
    chunks = onehot.reshape(n // SCAN_CHUNK, SCAN_CHUNK, N_EXPERTS)
    tri = jnp.tril(jnp.ones((SCAN_CHUNK, SCAN_CHUNK), F32), k=-1)
    within = jnp.einsum('ij,cjk->cik', tri, chunks)
    totals = jnp.sum(chunks, axis=1)
    before = jnp.cumsum(totals, axis=0) - totals
    rank = jnp.sum((within + before[:, None, :]) * chunks, axis=-1).reshape(n)
    return rank.astype(jnp.int32), jnp.sum(totals, axis=0).astype(jnp.int32)


def moe_swiglu(h, g_norm, w_router, w1, w3, w2, g_final, *, bsz, length):
    n_tok, d = h.shape
    n_assign = n_tok * TOP_K
    router_pad = 128
    wr = jnp.pad(w_router.astype(F32), ((0, 0), (0, router_pad - N_EXPERTS)))
    logits, hn = norm_router(h, g_norm, wr)
    top_val, top_idx = lax.top_k(logits[:, :N_EXPERTS], TOP_K)
    gate = jax.nn.softmax(top_val, axis=-1)
    expert = top_idx.astype(jnp.int32).reshape(n_assign)
    rank, counts = _expert_ranks(expert)
    padded = (counts + MOE_SUPER - 1) // MOE_SUPER * MOE_SUPER
    pad_end = jnp.cumsum(padded)
    pad_start = pad_end - padded
    slot = (pad_start[expert] + rank).reshape(n_tok, TOP_K)
    n_super = -(-n_assign // MOE_SUPER) + N_EXPERTS
    n_slots = n_super * MOE_SUPER
    token = jnp.repeat(jnp.arange(n_tok, dtype=jnp.int32), TOP_K)
    slot_token = jnp.zeros((n_slots,), jnp.int32).at[slot.reshape(n_assign)].set(token)
    n_used = pad_end[-1] // MOE_SUPER
    sb = jnp.arange(n_super, dtype=jnp.int32)
    block_expert = jnp.minimum(jnp.searchsorted(pad_end, jnp.minimum(sb, n_used - 1) * MOE_SUPER, side='right'),
                               N_EXPERTS - 1).astype(jnp.int32)
    rows_here = jnp.clip(counts[block_expert] - (sb * MOE_SUPER - pad_start[block_expert]), 0, MOE_SUPER)
    n_halves = jnp.where(sb < n_used, (rows_here + MOE_BLOCK - 1) // MOE_BLOCK, 0).astype(jnp.int32)
    yb = jnp.zeros((n_slots, d), BF16)
    sb0 = 0
    for part in MOE_CALL_SPLIT:
        sb1 = n_super if part is None else sb0 + part
        rows = slice(sb0 * MOE_SUPER, sb1 * MOE_SUPER)
        yb = moe_experts(hn[slot_token[rows]], sb0, block_expert[sb0:sb1], n_halves[sb0:sb1], w1, w3, w2, yb)
        sb0 = sb1
    if g_final is None:
        return moe_combine(h, yb[slot[:, 0]], yb[slot[:, 1]], gate[:, 0:1], gate[:, 1:2])
    keep = lambda a: a.reshape(bsz, length, -1)[:, N_META:].reshape(bsz * (length - N_META), -1)
    slot, gate = keep(slot), keep(gate)
    return moe_combine_final(h, yb[slot[:, 0]], yb[slot[:, 1]], gate[:, 0:1], gate[:, 1:2], g_final,
                             bsz=bsz, length=length)


def kernel(x, meta_tokens, norm_mix, w_in, na_rpb, na_meta_bias, sc_conv, hy_conv, hy_w1, hy_b1, hy_w2, hy_b2,
           hy_w3, hy_freq, hy_decay, hy_skip, w_na_out, w_sc_out, w_hy_out, w_o, norm_ffn, ffn_w1, ffn_w3,
           ffn_w2, router_w, moe_w1, moe_w3, moe_w2, norm_final):
    bsz, seq, d = x.shape
    length = N_META + seq
    depth = w_in.shape[0]
    meta = jnp.broadcast_to(meta_tokens.astype(x.dtype)[None], (bsz, N_META, d))
    h = jnp.concatenate([meta, x], axis=1).reshape(bsz * length, d)
    dft = _dft_matrices(length)
    kw = dict(bsz=bsz, length=length)
    w_in, w_na_out, w_sc_out, w_hy_out, w_o, ffn_w1, ffn_w3, ffn_w2 = map(
        to_bf16, (w_in, w_na_out, w_sc_out, w_hy_out, w_o, ffn_w1, ffn_w3, ffn_w2))
    for layer in range(depth):
        proj = norm_matmul(h, norm_mix[layer], w_in[layer], tn=2048)
        y_na = neighbourhood_attention(proj, na_rpb[layer], na_meta_bias[layer], **kw)
        y_sc = short_conv_mixer(proj, sc_conv[layer], **kw)
        y_hy = hyena_mixer(proj, hy_conv[layer], hy_w1[layer], hy_b1[layer], hy_w2[layer], hy_b2[layer],
                           hy_w3[layer], hy_freq[layer], hy_decay[layer], hy_skip[layer], dft, **kw)
        h = mixer_out(h, y_na, y_sc, y_hy, proj, w_na_out[layer], w_sc_out[layer], w_hy_out[layer], w_o[layer])
        j = layer // 2
        if layer % 2 == 0:
            h = dense_ffn(h, norm_ffn[layer], ffn_w1[j], ffn_w3[j], ffn_w2[j])
        else:
            g_final = norm_final if layer == depth - 1 else None
            h = moe_swiglu(h, norm_ffn[layer], router_w[j], moe_w1[j], moe_w3[j], moe_w2[j], g_final, **kw)
    if depth % 2 == 0:
        return h
    out = rms_norm_rows(h, norm_final, out_dtype=x.dtype)
    return out.reshape(bsz, length, d)[:, N_META:]
```

```python
import functools
import math

import numpy as np
import jax
import jax.numpy as jnp
from jax import lax
from jax.experimental import pallas as pl
from jax.experimental.pallas import tpu as pltpu

D_MODEL = 2048
N_META = 16
GRID_W = 64
GRID_ROWS = 32
NA_HEADS = 16
NA_HEAD_DIM = 64
NA_WIDTH = NA_HEADS * NA_HEAD_DIM
WIN_ROWS = 8
WIN_COLS = 16
SC_WIDTH = 512
HY_WIDTH = 512
HY_ORDER = 2
HY_EMB = 33
HY_HIDDEN = 64
HY_MOD_SHIFT = 0.05
PROJ_WIDTH = 3 * NA_WIDTH + 3 * SC_WIDTH + 3 * HY_WIDTH + 3 * D_MODEL
D_FF = 5632
N_EXPERTS = 8
TOP_K = 2
D_FF_EXPERT = 7168
MOE_BLOCK = 512
NORM_EPS = 1e-6
NEG_INF = -1e30

COL_Q, COL_K, COL_V = 0, NA_WIDTH, 2 * NA_WIDTH
COL_SCB = 3 * NA_WIDTH
COL_SCC = COL_SCB + SC_WIDTH
COL_SCU = COL_SCC + SC_WIDTH
COL_HYU = COL_SCU + SC_WIDTH
COL_GNA = COL_HYU + 3 * HY_WIDTH
COL_GSC = COL_GNA + D_MODEL
COL_GHY = COL_GSC + D_MODEL

V7X_VMEM_BYTES = 64 * 1024 * 1024
VMEM_LIMIT = V7X_VMEM_BYTES - 8 * 1024 * 1024

ROW_TILE = 768
NORM_CHUNK = 16
FFT_N = 2 * (N_META + GRID_ROWS * GRID_W)
FFT_KP = 2080
FFT_KB = 1040
INV_NB = 688

F32 = jnp.float32
BF16 = jnp.bfloat16


def _params(sem, vmem_limit=VMEM_LIMIT):
    return pltpu.CompilerParams(dimension_semantics=sem, vmem_limit_bytes=vmem_limit)


def _sigmoid(a):
    return 0.5 * jnp.tanh(0.5 * a) + 0.5


CAST_BLOCK_BYTES = 8 * 1024 * 1024


def _cast_kernel(x_ref, o_ref):
    o_ref[...] = x_ref[...].astype(o_ref.dtype)


def to_bf16(w):
    cols = w.shape[-1]
    w2 = w.reshape(-1, cols)
    rows = w2.shape[0]
    tr = min(rows, CAST_BLOCK_BYTES // (4 * cols)) // 16 * 16
    while rows % tr:
        tr -= 16
    out = pl.pallas_call(
        _cast_kernel,
        out_shape=jax.ShapeDtypeStruct((rows, cols), BF16),
        grid=(rows // tr,),
        in_specs=[pl.BlockSpec((tr, cols), lambda i: (i, 0))],
        out_specs=pl.BlockSpec((tr, cols), lambda i: (i, 0)),
        compiler_params=_params(("parallel",)),
        name="to_bf16",
    )(w2)
    return out.reshape(w.shape)


def _rms_norm_rows(x_ref, g_ref, xn_ref):
    rows = x_ref.shape[0]
    g = g_ref[...]

    def chunk(c, carry):
        r0 = pl.multiple_of(c * NORM_CHUNK, NORM_CHUNK)
        x = x_ref[pl.ds(r0, NORM_CHUNK), :]
        ms = jnp.mean(x * x, axis=-1, keepdims=True)
        xn_ref[pl.ds(r0, NORM_CHUNK), :] = (x * lax.rsqrt(ms + NORM_EPS) * g).astype(xn_ref.dtype)
        return carry

    lax.fori_loop(0, rows // NORM_CHUNK, chunk, 0, unroll=4)


def _norm_mm_kernel(x_ref, g_ref, w_ref, o_ref, xn_ref):
    @pl.when(pl.program_id(1) == 0)
    def _():
        _rms_norm_rows(x_ref, g_ref, xn_ref)

    o_ref[...] = jnp.dot(xn_ref[...], w_ref[...], preferred_element_type=F32).astype(o_ref.dtype)


def norm_matmul(x, g, w, *, tn, out_dtype=BF16, tm=ROW_TILE):
    t, k = x.shape
    n = w.shape[1]
    return pl.pallas_call(
        _norm_mm_kernel,
        out_shape=jax.ShapeDtypeStruct((t, n), out_dtype),
        grid=(t // tm, n // tn),
        in_specs=[
            pl.BlockSpec((tm, k), lambda i, j: (i, 0)),
            pl.BlockSpec((1, k), lambda i, j: (0, 0)),
            pl.BlockSpec((k, tn), lambda i, j: (0, j)),
        ],
        out_specs=pl.BlockSpec((tm, tn), lambda i, j: (i, j)),
        scratch_shapes=[pltpu.VMEM((tm, k), BF16)],
        compiler_params=_params(("parallel", "arbitrary")),
        name="norm_matmul",
    )(x, g.reshape(1, k), w)


EMBED_TILES = 3


def _embed_norm_mm_kernel(x_ref, meta_ref, g_ref, w_ref, o_ref, h_ref, xn_ref):
    tm = h_ref.shape[0]
    first = pl.program_id(0) % EMBED_TILES == 0

    @pl.when(jnp.logical_and(pl.program_id(1) == 0, first))
    def _():
        h_ref[0:N_META, :] = meta_ref[...]
        h_ref[N_META:tm, :] = x_ref[0, 0:tm - N_META, :]

    @pl.when(jnp.logical_and(pl.program_id(1) == 0, jnp.logical_not(first)))
    def _():
        h_ref[...] = x_ref[0]

    @pl.when(pl.program_id(1) == 0)
    def _():
        _rms_norm_rows(h_ref, g_ref, xn_ref)

    o_ref[...] = jnp.dot(xn_ref[...], w_ref[...], preferred_element_type=F32).astype(o_ref.dtype)


def embed_norm_matmul(x, meta_tokens, g, w, *, tn):
    bsz, seq, k = x.shape
    length = N_META + seq
    tm = length // EMBED_TILES
    n = w.shape[1]

    def x_rows(i, j):
        tile = i % EMBED_TILES
        start = jnp.maximum(tile * tm - N_META, 0)
        return i // EMBED_TILES, pl.multiple_of(start, N_META), 0

    return pl.pallas_call(
        _embed_norm_mm_kernel,
        out_shape=(jax.ShapeDtypeStruct((bsz * length, n), BF16), jax.ShapeDtypeStruct((bsz * length, k), F32)),
        grid=(bsz * EMBED_TILES, n // tn),
        in_specs=[
            pl.BlockSpec((pl.Element(1), pl.Element(tm), pl.Element(k)), x_rows),
            pl.BlockSpec((N_META, k), lambda i, j: (0, 0)),
            pl.BlockSpec((1, k), lambda i, j: (0, 0)),
            pl.BlockSpec((k, tn), lambda i, j: (0, j)),
        ],
        out_specs=(pl.BlockSpec((tm, tn), lambda i, j: (i, j)), pl.BlockSpec((tm, k), lambda i, j: (i, 0))),
        scratch_shapes=[pltpu.VMEM((tm, k), BF16)],
        compiler_params=_params(("parallel", "arbitrary")),
        name="embed_norm_matmul",
    )(x, meta_tokens.astype(F32), g.reshape(1, k), w)


def _norm_router_kernel(x_ref, g_ref, w_ref, o_ref, xb_ref, xn_ref):
    _rms_norm_rows(x_ref, g_ref, xn_ref)
    xn = xn_ref[...]
    x_hi = xn.astype(BF16)
    x_lo = (xn - x_hi.astype(F32)).astype(BF16)
    w = w_ref[...]
    w_hi = w.astype(BF16)
    w_lo = (w - w_hi.astype(F32)).astype(BF16)
    xb_ref[...] = x_hi
    logits = jnp.dot(x_hi, w_hi, preferred_element_type=F32)
    logits += jnp.dot(x_lo, w_hi, preferred_element_type=F32)
    logits += jnp.dot(x_hi, w_lo, preferred_element_type=F32)
    o_ref[...] = logits


def norm_router(x, g, w, *, tm=ROW_TILE):
    t, k = x.shape
    n = w.shape[1]
    return pl.pallas_call(
        _norm_router_kernel,
        out_shape=(jax.ShapeDtypeStruct((t, n), F32), jax.ShapeDtypeStruct((t, k), BF16)),
        grid=(t // tm,),
        in_specs=[
            pl.BlockSpec((tm, k), lambda i: (i, 0)),
            pl.BlockSpec((1, k), lambda i: (0, 0)),
            pl.BlockSpec((k, n), lambda i: (0, 0)),
        ],
        out_specs=(pl.BlockSpec((tm, n), lambda i: (i, 0)), pl.BlockSpec((tm, k), lambda i: (i, 0))),
        scratch_shapes=[pltpu.VMEM((tm, k), F32)],
        compiler_params=_params(("parallel",)),
        name="norm_router",
    )(x, g.reshape(1, k), w)


def _norm_kernel(x_ref, g_ref, o_ref):
    _rms_norm_rows(x_ref, g_ref, o_ref)


def rms_norm_rows(x, g, *, out_dtype, tm=ROW_TILE):
    t, k = x.shape
    return pl.pallas_call(
        _norm_kernel,
        out_shape=jax.ShapeDtypeStruct((t, k), out_dtype),
        grid=(t // tm,),
        in_specs=[pl.BlockSpec((tm, k), lambda i: (i, 0)), pl.BlockSpec((1, k), lambda i: (0, 0))],
        out_specs=pl.BlockSpec((tm, k), lambda i: (i, 0)),
        compiler_params=_params(("parallel",)),
        name="rms_norm",
    )(x, g.reshape(1, k))


def _dense_ffn_kernel(x_ref, g_ref, w1_ref, w3_ref, w2_ref, o_ref, xn_ref):
    @pl.when(pl.program_id(1) == 0)
    def _():
        _rms_norm_rows(x_ref, g_ref, xn_ref)
        o_ref[...] = x_ref[...]

    xn = xn_ref[...]
    a = jnp.dot(xn, w1_ref[...], preferred_element_type=F32)
    b = jnp.dot(xn, w3_ref[...], preferred_element_type=F32)
    hid = (a * _sigmoid(a) * b).astype(BF16)
    o_ref[...] += jnp.dot(hid, w2_ref[...], preferred_element_type=F32)


def dense_ffn(x, g, w1, w3, w2, *, fc=512, tm=ROW_TILE):
    t, d = x.shape
    f = w1.shape[1]
    return pl.pallas_call(
        _dense_ffn_kernel,
        out_shape=jax.ShapeDtypeStruct((t, d), F32),
        grid=(t // tm, f // fc),
        in_specs=[
            pl.BlockSpec((tm, d), lambda i, j: (i, 0)),
            pl.BlockSpec((1, d), lambda i, j: (0, 0)),
            pl.BlockSpec((d, fc), lambda i, j: (0, j)),
            pl.BlockSpec((d, fc), lambda i, j: (0, j)),
            pl.BlockSpec((fc, d), lambda i, j: (j, 0)),
        ],
        out_specs=pl.BlockSpec((tm, d), lambda i, j: (i, 0)),
        scratch_shapes=[pltpu.VMEM((tm, d), BF16)],
        compiler_params=_params(("parallel", "arbitrary")),
        name="dense_ffn",
    )(x, g.reshape(1, d), w1, w3, w2)


def _mixer_out_kernel(res_ref, yna_ref, ysc_ref, yhy_ref, gna_ref, gsc_ref, ghy_ref, wna_ref, wsc_ref, why_ref,
                      wo_ref, o_ref):
    @pl.when(pl.program_id(1) == 0)
    def _():
        o_ref[...] = res_ref[...]

    m = _sigmoid(gna_ref[...].astype(F32)) * jnp.dot(yna_ref[...], wna_ref[...], preferred_element_type=F32)
    m += _sigmoid(gsc_ref[...].astype(F32)) * jnp.dot(ysc_ref[...], wsc_ref[...], preferred_element_type=F32)
    m += _sigmoid(ghy_ref[...].astype(F32)) * jnp.dot(yhy_ref[...], why_ref[...], preferred_element_type=F32)
    o_ref[...] += jnp.dot(m.astype(BF16), wo_ref[...], preferred_element_type=F32)


MIXER_ROWS = 256


def mixer_out(res, y_na, y_sc, y_hy, proj, w_na, w_sc, w_hy, w_o, *, cw=D_MODEL, tm=MIXER_ROWS):
    t, d = res.shape
    return pl.pallas_call(
        _mixer_out_kernel,
        out_shape=jax.ShapeDtypeStruct((t, d), F32),
        grid=(t // tm, d // cw),
        in_specs=[
            pl.BlockSpec((tm, d), lambda i, c: (i, 0)),
            pl.BlockSpec((tm, NA_WIDTH), lambda i, c: (i, 0)),
            pl.BlockSpec((tm, SC_WIDTH), lambda i, c: (i, 0)),
            pl.BlockSpec((tm, HY_WIDTH), lambda i, c: (i, 0)),
            pl.BlockSpec((tm, cw), lambda i, c: (i, COL_GNA // cw + c)),
            pl.BlockSpec((tm, cw), lambda i, c: (i, COL_GSC // cw + c)),
            pl.BlockSpec((tm, cw), lambda i, c: (i, COL_GHY // cw + c)),
            pl.BlockSpec((NA_WIDTH, cw), lambda i, c: (0, c)),
            pl.BlockSpec((SC_WIDTH, cw), lambda i, c: (0, c)),
            pl.BlockSpec((HY_WIDTH, cw), lambda i, c: (0, c)),
            pl.BlockSpec((cw, d), lambda i, c: (c, 0)),
        ],
        out_specs=pl.BlockSpec((tm, d), lambda i, c: (i, 0)),
        compiler_params=_params(("parallel", "arbitrary")),
        name="mixer_out",
    )(res, y_na, y_sc, y_hy, proj, proj, proj, w_na, w_sc, w_hy, w_o)


def _na_bias_table(rpb):
    qc = np.arange(GRID_W)[:, None]
    kc = np.arange(GRID_W)[None, :]
    win_start = np.clip(qc - WIN_COLS // 2, 0, GRID_W - WIN_COLS)
    inside = (kc >= win_start) & (kc < win_start + WIN_COLS)
    dc = np.clip(kc - qc, -(WIN_COLS - 1), WIN_COLS - 1) + WIN_COLS - 1
    dr = np.arange(WIN_ROWS)[None, :] - np.arange(WIN_ROWS)[:, None] + WIN_ROWS - 1
    onehot = np.zeros((2 * WIN_COLS - 1, GRID_W * GRID_W), np.float32)
    onehot[dc.reshape(-1), np.arange(GRID_W * GRID_W)] = 1.0
    tbl = jnp.einsum('htid,dj->htij', rpb.astype(F32)[:, dr], jnp.asarray(onehot),
                     precision=lax.Precision.HIGHEST)
    tbl = tbl.reshape(NA_HEADS, WIN_ROWS, WIN_ROWS, GRID_W, GRID_W)
    tbl = jnp.where(jnp.asarray(inside)[None, None, None], tbl, NEG_INF)
    tbl = tbl.transpose(0, 1, 3, 2, 4)
    return tbl.reshape(NA_HEADS, WIN_ROWS, GRID_W, WIN_ROWS * GRID_W)


NA_SOFTMAX_LAG = 1
NA_PV_LAG = 2


def _na_kernel(q_ref, k_ref, v_ref, tbl_ref, mb_ref, o_ref):
    scale = NA_HEAD_DIM ** -0.5
    lane = lax.broadcasted_iota(jnp.int32, (1, 2 * NA_HEAD_DIM), 1)
    first = lane < NA_HEAD_DIM
    nt = (((1,), (1,)), ((), ()))
    k_meta = k_ref[0:N_META, :]
    v_meta = v_ref[0:N_META, :]
    mb_grid = mb_ref[0, 0]
    band = WIN_ROWS * GRID_W

    def stack_heads(q):
        q = q * scale
        zero = jnp.zeros_like(q)
        return jnp.concatenate([jnp.where(first, q, zero), jnp.where(first, zero, q)], axis=0)

    def unstack_heads(out, den, n):
        out = out / den
        return jnp.where(first, out[:n], out[n:])

    def softmax_parts(s_grid, s_meta):
        m = jnp.max(s_meta, axis=-1, keepdims=True)
        if s_grid is not None:
            m = jnp.maximum(m, jnp.max(s_grid, axis=-1, keepdims=True))
        p_meta = jnp.exp(s_meta - m)
        den = jnp.sum(p_meta, axis=-1, keepdims=True)
        p_grid = None
        if s_grid is not None:
            p_grid = jnp.exp(s_grid - m)
            den = den + jnp.sum(p_grid, axis=-1, keepdims=True)
            p_grid = p_grid.astype(BF16)
        return p_grid, p_meta.astype(BF16), den

    def geometry(r):
        rs = min(max(r - WIN_ROWS // 2, 0), GRID_ROWS - WIN_ROWS)
        return N_META + r * GRID_W, N_META + rs * GRID_W, r - rs

    def scores(r):
        if r is None:
            q2 = stack_heads(q_ref[0:N_META, :])
            return None, lax.dot_general(q2, k_meta, nt, preferred_element_type=F32) + mb_ref[0, 1, 0:2 * N_META, :]
        q0, k0, typ = geometry(r)
        q2 = stack_heads(q_ref[q0:q0 + GRID_W, :])
        s_grid = lax.dot_general(q2, k_ref[k0:k0 + band, :], nt, preferred_element_type=F32) + tbl_ref[0, typ]
        s_meta = lax.dot_general(q2, k_meta, nt, preferred_element_type=F32) + mb_grid
        return s_grid, s_meta

    def weighted_values(r, p_grid, p_meta, den):
        out = jnp.dot(p_meta, v_meta, preferred_element_type=F32)
        if r is None:
            o_ref[0:N_META, :] = unstack_heads(out, den, N_META).astype(o_ref.dtype)
            return
        q0, k0, _ = geometry(r)
        out += jnp.dot(p_grid, v_ref[k0:k0 + band, :], preferred_element_type=F32)
        o_ref[q0:q0 + GRID_W, :] = unstack_heads(out, den, GRID_W).astype(o_ref.dtype)

    units = [None] + list(range(GRID_ROWS))
    pending_scores, pending_probs = {}, {}
    for t in range(len(units) + NA_PV_LAG):
        if t < len(units):
            pending_scores[t] = scores(units[t])
        if 0 <= t - NA_SOFTMAX_LAG < len(units):
            pending_probs[t - NA_SOFTMAX_LAG] = softmax_parts(*pending_scores.pop(t - NA_SOFTMAX_LAG))
        if 0 <= t - NA_PV_LAG < len(units):
            weighted_values(units[t - NA_PV_LAG], *pending_probs.pop(t - NA_PV_LAG))


def neighbourhood_attention(proj, rpb, meta_bias, *, bsz, length):
    hp = 2 * NA_HEAD_DIM
    n_pairs = NA_HEADS // 2
    band = WIN_ROWS * GRID_W
    tbl = _na_bias_table(rpb).reshape(n_pairs, 2, WIN_ROWS, GRID_W, band).transpose(0, 2, 1, 3, 4)
    tbl = tbl.reshape(n_pairs, WIN_ROWS, 2 * GRID_W, band)
    mb = meta_bias.astype(F32).reshape(n_pairs, 2, 1, N_META)
    mb_grid = jnp.broadcast_to(mb, (n_pairs, 2, GRID_W, N_META)).reshape(n_pairs, 2 * GRID_W, N_META)
    mb_meta = jnp.broadcast_to(mb, (n_pairs, 2, N_META, N_META)).reshape(n_pairs, 2 * N_META, N_META)
    mb_meta = jnp.pad(mb_meta, ((0, 0), (0, 2 * GRID_W - 2 * N_META), (0, 0)))
    mb2 = jnp.stack([mb_grid, mb_meta], axis=1)
    return pl.pallas_call(
        _na_kernel,
        out_shape=jax.ShapeDtypeStruct((bsz * length, NA_WIDTH), BF16),
        grid=(n_pairs, bsz),
        in_specs=[
            pl.BlockSpec((length, hp), lambda h, b: (b, COL_Q // hp + h)),
            pl.BlockSpec((length, hp), lambda h, b: (b, COL_K // hp + h)),
            pl.BlockSpec((length, hp), lambda h, b: (b, COL_V // hp + h)),
            pl.BlockSpec((1, WIN_ROWS, 2 * GRID_W, band), lambda h, b: (h, 0, 0, 0)),
            pl.BlockSpec((1, 2, 2 * GRID_W, N_META), lambda h, b: (h, 0, 0, 0)),
        ],
        out_specs=pl.BlockSpec((length, hp), lambda h, b: (b, h)),
        compiler_params=_params(("parallel", "parallel")),
        name="neighbourhood_attention",
    )(proj, proj, proj, tbl, mb2)


def _dwconv3(u, w):
    length = u.shape[0]
    row = lax.broadcasted_iota(jnp.int32, u.shape, 0)
    prev = jnp.where(row == 0, 0.0, pltpu.roll(u, 1, 0))
    nxt = jnp.where(row == length - 1, 0.0, pltpu.roll(u, length - 1, 0))
    return prev * w[0:1] + u * w[1:2] + nxt * w[2:3]


def _gated_conv_kernel(b_ref, c_ref, u_ref, w_ref, o_ref):
    u = c_ref[...].astype(F32) * u_ref[...].astype(F32)
    o_ref[...] = (b_ref[...].astype(F32) * _dwconv3(u, w_ref[...])).astype(o_ref.dtype)


def _plain_conv_kernel(u_ref, w_ref, o_ref):
    o_ref[...] = _dwconv3(u_ref[...].astype(F32), w_ref[...]).astype(o_ref.dtype)


CONV_CB = 512


def short_conv_mixer(proj, w_conv, *, bsz, length):
    cb = CONV_CB
    return pl.pallas_call(
        _gated_conv_kernel,
        out_shape=jax.ShapeDtypeStruct((bsz * length, SC_WIDTH), BF16),
        grid=(bsz, SC_WIDTH // cb),
        in_specs=[
            pl.BlockSpec((length, cb), lambda b, c: (b, COL_SCB // cb + c)),
            pl.BlockSpec((length, cb), lambda b, c: (b, COL_SCC // cb + c)),
            pl.BlockSpec((length, cb), lambda b, c: (b, COL_SCU // cb + c)),
            pl.BlockSpec((3, cb), lambda b, c: (0, c)),
        ],
        out_specs=pl.BlockSpec((length, cb), lambda b, c: (b, c)),
        compiler_params=_params(("parallel", "parallel")),
        name="short_conv_mixer",
    )(proj, proj, proj, w_conv.astype(F32))


def hyena_in_conv(proj, w_conv, *, bsz, length):
    cb = CONV_CB
    return pl.pallas_call(
        _plain_conv_kernel,
        out_shape=jax.ShapeDtypeStruct((bsz * length, 3 * HY_WIDTH), BF16),
        grid=(bsz, 3 * HY_WIDTH // cb),
        in_specs=[
            pl.BlockSpec((length, cb), lambda b, c: (b, COL_HYU // cb + c)),
            pl.BlockSpec((3, cb), lambda b, c: (0, c)),
        ],
        out_specs=pl.BlockSpec((length, cb), lambda b, c: (b, c)),
        compiler_params=_params(("parallel", "parallel")),
        name="hyena_in_conv",
    )(proj, w_conv.astype(F32))


def _filter_kernel(z_ref, w1_ref, b1_ref, w2_ref, b2_ref, w3_ref, fr_ref, dec_ref, f_ref, s_ref):
    hi = lax.Precision.HIGHEST
    z = z_ref[...]
    fr = fr_ref[...]
    h = jnp.sin(fr * (jnp.dot(z, w1_ref[...], preferred_element_type=F32, precision=hi) + b1_ref[...]))
    h = jnp.sin(fr * (jnp.dot(h, w2_ref[...], preferred_element_type=F32, precision=hi) + b2_ref[...]))
    h = jnp.dot(h, w3_ref[...], preferred_element_type=F32, precision=hi)
    t = z[:, 0:1]
    f = h * (jnp.exp(-t * jnp.abs(dec_ref[0])) + HY_MOD_SHIFT)
    row = lax.broadcasted_iota(jnp.int32, f.shape, 0)
    backward = pl.program_id(0) % 2 == 1
    f = jnp.where(jnp.logical_and(row == 0, backward), 0.0, f)
    f_ref[0] = f.astype(f_ref.dtype)
    s_ref[0] = jnp.sum(jnp.abs(f), axis=0, keepdims=True)


def hyena_filters(length, w1, b1, w2, b2, w3, freq, decay):
    bands = (HY_EMB - 1) // 2
    t = jnp.linspace(0.0, 1.0, length, dtype=F32)[:, None]
    ang = (2.0 * math.pi / length) * jnp.arange(length, dtype=F32)[:, None]
    fb = jnp.linspace(1e-4, bands - 1, bands, dtype=F32)[None, :]
    z = jnp.concatenate([t, jnp.cos(fb * ang), -jnp.sin(fb * ang)], axis=-1)
    emb = 128
    z = jnp.pad(z, ((0, 0), (0, emb - HY_EMB)))
    w1p = jnp.pad(w1.astype(F32), ((0, emb - HY_EMB), (0, 0)))
    n_slab = HY_ORDER * 2
    row = lambda a: a.astype(F32).reshape(1, -1)
    return pl.pallas_call(
        _filter_kernel,
        out_shape=(jax.ShapeDtypeStruct((n_slab, length, HY_WIDTH), BF16),
                   jax.ShapeDtypeStruct((n_slab, 1, HY_WIDTH), F32)),
        grid=(n_slab,),
        in_specs=[
            pl.BlockSpec((length, emb), lambda j: (0, 0)),
            pl.BlockSpec((emb, HY_HIDDEN), lambda j: (0, 0)),
            pl.BlockSpec((1, HY_HIDDEN), lambda j: (0, 0)),
            pl.BlockSpec((HY_HIDDEN, HY_HIDDEN), lambda j: (0, 0)),
            pl.BlockSpec((1, HY_HIDDEN), lambda j: (0, 0)),
            pl.BlockSpec((HY_HIDDEN, HY_WIDTH), lambda j: (0, j)),
            pl.BlockSpec((1, HY_HIDDEN), lambda j: (0, 0)),
            pl.BlockSpec((1, 1, HY_WIDTH), lambda j: (j, 0, 0)),
        ],
        out_specs=(pl.BlockSpec((1, length, HY_WIDTH), lambda j: (j, 0, 0)),
                   pl.BlockSpec((1, 1, HY_WIDTH), lambda j: (j, 0, 0))),
        compiler_params=_params(("parallel",)),
        name="hyena_filters",
    )(z, w1p, row(b1), w2.astype(F32), row(b2), w3.astype(F32), row(freq),
      decay.astype(F32).reshape(n_slab, 1, HY_WIDTH))


def _dft_matrices(length):
    k = jnp.arange(FFT_KP, dtype=jnp.int32)[:, None]
    n = jnp.arange(length, dtype=jnp.int32)[None, :]
    theta = (2.0 * math.pi / FFT_N) * ((k * n) % FFT_N).astype(F32)
    valid = k <= FFT_N // 2
    fc = jnp.where(valid, jnp.cos(theta), 0.0).astype(BF16)
    fs = jnp.where(valid, -jnp.sin(theta), 0.0).astype(BF16)
    fc, fs = lax.optimization_barrier((fc, fs))
    return fc, fs, fc.T, fs.T


def _dft_raw_kernel(fc_ref, fs_ref, u_ref, re_ref, im_ref):
    u = u_ref[0]
    re_ref[0] = jnp.dot(fc_ref[...], u, preferred_element_type=F32)
    im_ref[0] = jnp.dot(fs_ref[...], u, preferred_element_type=F32)


def dft_filters(fc, fs, filt):
    n_slab, length, c = filt.shape
    out = jax.ShapeDtypeStruct((n_slab, FFT_KP, c), F32)
    return pl.pallas_call(
        _dft_raw_kernel,
        out_shape=(out, out),
        grid=(FFT_KP // FFT_KB, n_slab),
        in_specs=[
            pl.BlockSpec((FFT_KB, length), lambda kb, s: (kb, 0)),
            pl.BlockSpec((FFT_KB, length), lambda kb, s: (kb, 0)),
            pl.BlockSpec((1, length, c), lambda kb, s: (s, 0, 0)),
        ],
        out_specs=(pl.BlockSpec((1, FFT_KB, c), lambda kb, s: (s, kb, 0)),
                   pl.BlockSpec((1, FFT_KB, c), lambda kb, s: (s, kb, 0))),
        compiler_params=_params(("parallel", "parallel")),
        name="dft_filters",
    )(fc, fs, filt)


def _dft_mul_kernel(fc_ref, fs_ref, u_ref, are_ref, aim_ref, s_ref, yre_ref, yim_ref, g_ref):
    @pl.when(pl.program_id(1) == 0)
    def _():
        k = pl.program_id(0) * FFT_KB + lax.broadcasted_iota(jnp.int32, (FFT_KB, 1), 0)
        wk = jnp.where(jnp.logical_or(k == 0, k == FFT_N // 2), 1.0, 2.0)
        wk = jnp.where(k > FFT_N // 2, 0.0, wk) / FFT_N
        inv = wk / (s_ref[0] + s_ref[1])
        g_ref[0] = (are_ref[0] + are_ref[1]) * inv
        g_ref[1] = (aim_ref[0] - aim_ref[1]) * inv

    u = u_ref[...]
    re = jnp.dot(fc_ref[...], u, preferred_element_type=F32)
    im = jnp.dot(fs_ref[...], u, preferred_element_type=F32)
    gre = g_ref[0]
    gim = g_ref[1]
    yre_ref[0] = (re * gre - im * gim).astype(yre_ref.dtype)
    yim_ref[0] = (re * gim + im * gre).astype(yim_ref.dtype)


def dft_forward_mul(fc, fs, u, col, a_re, a_im, abs_sum, order, *, bsz, length):
    c = HY_WIDTH
    out = jax.ShapeDtypeStruct((bsz, FFT_KP, c), BF16)
    return pl.pallas_call(
        _dft_mul_kernel,
        out_shape=(out, out),
        grid=(FFT_KP // FFT_KB, bsz),
        in_specs=[
            pl.BlockSpec((FFT_KB, length), lambda kb, b: (kb, 0)),
            pl.BlockSpec((FFT_KB, length), lambda kb, b: (kb, 0)),
            pl.BlockSpec((length, c), lambda kb, b: (b, col)),
            pl.BlockSpec((2, FFT_KB, c), lambda kb, b: (order, kb, 0)),
            pl.BlockSpec((2, FFT_KB, c), lambda kb, b: (order, kb, 0)),
            pl.BlockSpec((2, 1, c), lambda kb, b: (order, 0, 0)),
        ],
        out_specs=(pl.BlockSpec((1, FFT_KB, c), lambda kb, b: (b, kb, 0)),
                   pl.BlockSpec((1, FFT_KB, c), lambda kb, b: (b, kb, 0))),
        scratch_shapes=[pltpu.VMEM((2, FFT_KB, c), F32)],
        compiler_params=_params(("parallel", "arbitrary")),
        name="dft_forward_mul",
    )(fc, fs, u, a_re, a_im, abs_sum)


def _dft_inv_kernel(ct_ref, st_ref, yre_ref, yim_ref, u_ref, gate_ref, skip_ref, o_ref):
    y = jnp.dot(ct_ref[...], yre_ref[0], preferred_element_type=F32)
    y += jnp.dot(st_ref[...], yim_ref[0], preferred_element_type=F32)
    y += u_ref[...].astype(F32) * skip_ref[0]
    o_ref[...] = (gate_ref[...].astype(F32) * y).astype(o_ref.dtype)


def dft_inverse_gate(fct, fst, yre, yim, u, u_col, gate, gate_col, skip, order, *, bsz, length):
    c = HY_WIDTH
    nblk = length // INV_NB
    return pl.pallas_call(
        _dft_inv_kernel,
        out_shape=jax.ShapeDtypeStruct((bsz * length, c), BF16),
        grid=(nblk, bsz),
        in_specs=[
            pl.BlockSpec((INV_NB, FFT_KP), lambda nb, b: (nb, 0)),
            pl.BlockSpec((INV_NB, FFT_KP), lambda nb, b: (nb, 0)),
            pl.BlockSpec((1, FFT_KP, c), lambda nb, b: (b, 0, 0)),
            pl.BlockSpec((1, FFT_KP, c), lambda nb, b: (b, 0, 0)),
            pl.BlockSpec((INV_NB, c), lambda nb, b: (b * nblk + nb, u_col)),
            pl.BlockSpec((INV_NB, c), lambda nb, b: (b * nblk + nb, gate_col)),
            pl.BlockSpec((1, 1, c), lambda nb, b: (order, 0, 0)),
        ],
        out_specs=pl.BlockSpec((INV_NB, c), lambda nb, b: (b * nblk + nb, 0)),
        compiler_params=_params(("parallel", "parallel")),
        name="dft_inverse_gate",
    )(fct, fst, yre, yim, u, gate, skip)


def hyena_mixer(proj, w_conv, w1, b1, w2, b2, w3, freq, decay, skip, dft, *, bsz, length):
    fc, fs, fct, fst = dft
    vxx = hyena_in_conv(proj, w_conv, bsz=bsz, length=length)
    filt, abs_sum = hyena_filters(length, w1, b1, w2, b2, w3, freq, decay)
    a_re, a_im = dft_filters(fc, fs, filt)
    skip = skip.astype(F32).reshape(HY_ORDER, 1, HY_WIDTH)
    kw = dict(bsz=bsz, length=length)
    yre, yim = dft_forward_mul(fc, fs, vxx, 0, a_re, a_im, abs_sum, 0, **kw)
    z = dft_inverse_gate(fct, fst, yre, yim, vxx, 0, vxx, 1, skip, 0, **kw)
    yre, yim = dft_forward_mul(fc, fs, z, 0, a_re, a_im, abs_sum, 1, **kw)
    return dft_inverse_gate(fct, fst, yre, yim, z, 0, vxx, 2, skip, 1, **kw)


MOE_SUPER = 2 * MOE_BLOCK
MOE_FC = 512
MOE_VMEM_LIMIT = V7X_VMEM_BYTES - 3 * 1024 * 1024


def _moe_kernel(be_ref, nh_ref, x_ref, w1_ref, w3_ref, w2_ref, yb_ref, o_ref, acc_ref):
    del be_ref, yb_ref
    i = pl.program_id(0)
    j = pl.program_id(1)
    halves = nh_ref[i]

    @pl.when(j == 0)
    def _():
        acc_ref[...] = jnp.zeros_like(acc_ref)

    def ffn(rows):
        x = x_ref[0:rows, :]
        a = jnp.dot(x, w1_ref[0].astype(BF16), preferred_element_type=F32)
        b = jnp.dot(x, w3_ref[0].astype(BF16), preferred_element_type=F32)
        hid = (a * _sigmoid(a) * b).astype(BF16)
        acc_ref[0:rows, :] += jnp.dot(hid, w2_ref[0].astype(BF16), preferred_element_type=F32)

    @pl.when(halves == 2)
    def _():
        ffn(MOE_SUPER)

    @pl.when(halves == 1)
    def _():
        ffn(MOE_BLOCK)

    @pl.when(j == pl.num_programs(1) - 1)
    def _():
        o_ref[...] = acc_ref[...].astype(o_ref.dtype)


def moe_experts(xb, sb_start, block_expert, n_halves, w1, w3, w2, yb):
    d = xb.shape[1]
    n_super = xb.shape[0] // MOE_SUPER
    fc = MOE_FC
    n_fc = D_FF_EXPERT // fc

    def chunk(i, j, nh):
        return jnp.where(nh[i] > 0, j, n_fc - 1)

    grid_spec = pltpu.PrefetchScalarGridSpec(
        num_scalar_prefetch=2,
        grid=(n_super, n_fc),
        in_specs=[
            pl.BlockSpec((MOE_SUPER, d), lambda i, j, be, nh: (i, 0)),
            pl.BlockSpec((1, d, fc), lambda i, j, be, nh: (be[i], 0, chunk(i, j, nh))),
            pl.BlockSpec((1, d, fc), lambda i, j, be, nh: (be[i], 0, chunk(i, j, nh))),
            pl.BlockSpec((1, fc, d), lambda i, j, be, nh: (be[i], chunk(i, j, nh), 0)),
            pl.BlockSpec(memory_space=pl.ANY),
        ],
        out_specs=pl.BlockSpec((MOE_SUPER, d), lambda i, j, be, nh: (sb_start + i, 0)),
        scratch_shapes=[pltpu.VMEM((MOE_SUPER, d), F32)],
    )
    return pl.pallas_call(
        _moe_kernel,
        out_shape=jax.ShapeDtypeStruct(yb.shape, yb.dtype),
        grid_spec=grid_spec,
        input_output_aliases={6: 0},
        compiler_params=_params(("parallel", "arbitrary"), MOE_VMEM_LIMIT),
        name="moe_experts",
    )(block_expert, n_halves, xb, w1, w3, w2, yb)


COMBINE_ROWS = 256


def _moe_combine_kernel(h_ref, y0_ref, y1_ref, g0_ref, g1_ref, gn_ref, o_ref, *, final):
    rows = o_ref.shape[0]
    gn = gn_ref[...]

    def chunk(c, carry):
        r0 = pl.multiple_of(c * NORM_CHUNK, NORM_CHUNK)
        sl = pl.ds(r0, NORM_CHUNK)
        h = h_ref[0, sl, :] if final else h_ref[sl, :]
        y = h + g0_ref[sl, :] * y0_ref[sl, :].astype(F32) + g1_ref[sl, :] * y1_ref[sl, :].astype(F32)
        if final:
            ms = jnp.mean(y * y, axis=-1, keepdims=True)
            y = y * lax.rsqrt(ms + NORM_EPS) * gn
        o_ref[sl, :] = y.astype(o_ref.dtype)
        return carry

    lax.fori_loop(0, rows // NORM_CHUNK, chunk, 0, unroll=4)


def moe_combine(h, y0, y1, g0, g1, *, tm=ROW_TILE):
    t, d = h.shape
    row = lambda w: pl.BlockSpec((tm, w), lambda i: (i, 0))
    return pl.pallas_call(
        functools.partial(_moe_combine_kernel, final=False),
        out_shape=jax.ShapeDtypeStruct((t, d), F32),
        grid=(t // tm,),
        in_specs=[row(d), row(d), row(d), row(1), row(1), pl.BlockSpec((1, d), lambda i: (0, 0))],
        out_specs=row(d),
        compiler_params=_params(("parallel",)),
        name="moe_combine",
    )(h, y0, y1, g0, g1, jnp.ones((1, d), F32))


def moe_combine_final(h, y0, y1, g0, g1, g_final, *, bsz, length):
    d = h.shape[1]
    seq = length - N_META
    r = COMBINE_ROWS
    nr = seq // r
    row = lambda w: pl.BlockSpec((r, w), lambda b, i: (b * nr + i, 0))
    h_spec = pl.BlockSpec((pl.Element(1), pl.Element(r), pl.Element(d)),
                          lambda b, i: (b, pl.multiple_of(N_META + i * r, N_META), 0))
    return pl.pallas_call(
        functools.partial(_moe_combine_kernel, final=True),
        out_shape=jax.ShapeDtypeStruct((bsz, seq, d), F32),
        grid=(bsz, nr),
        in_specs=[h_spec, row(d), row(d), row(1), row(1), pl.BlockSpec((1, d), lambda b, i: (0, 0))],
        out_specs=pl.BlockSpec((None, r, d), lambda b, i: (b, i, 0)),
        compiler_params=_params(("parallel", "parallel")),
        name="moe_combine_final",
    )(h.reshape(bsz, length, d), y0, y1, g0, g1, g_final.astype(F32).reshape(1, d))


SCAN_CHUNK = 256
MOE_CALL_SPLIT = (4, 16, None)


def _expert_ranks(expert):
    n = expert.shape[0]
    onehot = (expert[:, None] == jnp.arange(N_EXPERTS, dtype=jnp.int32)[None, :]).astype(F32)
    chunks = onehot.reshape(n // SCAN_CHUNK, SCAN_CHUNK, N_EXPERTS)
    tri = jnp.tril(jnp.ones((SCAN_CHUNK, SCAN_CHUNK), F32), k=-1)
    within = jnp.einsum('ij,cjk->cik', tri, chunks)
    totals = jnp.sum(chunks, axis=1)
    before = jnp.cumsum(totals, axis=0) - totals
    rank = jnp.sum((within + before[:, None, :]) * chunks, axis=-1).reshape(n)
    return rank.astype(jnp.int32), jnp.sum(totals, axis=0).astype(jnp.int32)


def moe_swiglu(h, g_norm, w_router, w1, w3, w2, g_final, *, bsz, length):
    n_tok, d = h.shape
    n_assign = n_tok * TOP_K
    router_pad = 128
    wr = jnp.pad(w_router.astype(F32), ((0, 0), (0, router_pad - N_EXPERTS)))
    logits, hn = norm_router(h, g_norm, wr)
    top_val, top_idx = lax.top_k(logits[:, :N_EXPERTS], TOP_K)
    gate = jax.nn.softmax(top_val, axis=-1)
    expert = top_idx.astype(jnp.int32).reshape(n_assign)
    rank, counts = _expert_ranks(expert)
    padded = (counts + MOE_SUPER - 1) // MOE_SUPER * MOE_SUPER
    pad_end = jnp.cumsum(padded)
    pad_start = pad_end - padded
    slot = (pad_start[expert] + rank).reshape(n_tok, TOP_K)
    n_super = -(-n_assign // MOE_SUPER) + N_EXPERTS
    n_slots = n_super * MOE_SUPER
    token = jnp.repeat(jnp.arange(n_tok, dtype=jnp.int32), TOP_K)
    slot_token = jnp.zeros((n_slots,), jnp.int32).at[slot.reshape(n_assign)].set(token)
    n_used = pad_end[-1] // MOE_SUPER
    sb = jnp.arange(n_super, dtype=jnp.int32)
    block_expert = jnp.minimum(jnp.searchsorted(pad_end, jnp.minimum(sb, n_used - 1) * MOE_SUPER, side='right'),
                               N_EXPERTS - 1).astype(jnp.int32)
    rows_here = jnp.clip(counts[block_expert] - (sb * MOE_SUPER - pad_start[block_expert]), 0, MOE_SUPER)
    n_halves = jnp.where(sb < n_used, (rows_here + MOE_BLOCK - 1) // MOE_BLOCK, 0).astype(jnp.int32)
    yb = jnp.zeros((n_slots, d), BF16)
    sb0 = 0
    for part in MOE_CALL_SPLIT:
        sb1 = n_super if part is None else sb0 + part
        rows = slice(sb0 * MOE_SUPER, sb1 * MOE_SUPER)
        yb = moe_experts(hn[slot_token[rows]], sb0, block_expert[sb0:sb1], n_halves[sb0:sb1], w1, w3, w2, yb)
        sb0 = sb1
    if g_final is None:
        return moe_combine(h, yb[slot[:, 0]], yb[slot[:, 1]], gate[:, 0:1], gate[:, 1:2])
    keep = lambda a: a.reshape(bsz, length, -1)[:, N_META:].reshape(bsz * (length - N_META), -1)
    slot, gate = keep(slot), keep(gate)
    return moe_combine_final(h, yb[slot[:, 0]], yb[slot[:, 1]], gate[:, 0:1], gate[:, 1:2], g_final,
                             bsz=bsz, length=length)


def kernel(x, meta_tokens, norm_mix, w_in, na_rpb, na_meta_bias, sc_conv, hy_conv, hy_w1, hy_b1, hy_w2, hy_b2,
           hy_w3, hy_freq, hy_decay, hy_skip, w_na_out, w_sc_out, w_hy_out, w_o, norm_ffn, ffn_w1, ffn_w3,
           ffn_w2, router_w, moe_w1, moe_w3, moe_w2, norm_final):
    bsz, seq, d = x.shape
    length = N_META + seq
    depth = w_in.shape[0]
    dft = _dft_matrices(length)
    kw = dict(bsz=bsz, length=length)
    w_in, w_na_out, w_sc_out, w_hy_out, w_o, ffn_w1, ffn_w3, ffn_w2 = map(
        to_bf16, (w_in, w_na_out, w_sc_out, w_hy_out, w_o, ffn_w1, ffn_w3, ffn_w2))
    for layer in range(depth):
        if layer == 0:
            proj, h = embed_norm_matmul(x, meta_tokens, norm_mix[0], w_in[0], tn=2048)
        else:
            proj = norm_matmul(h, norm_mix[layer], w_in[layer], tn=2048)
        y_na = neighbourhood_attention(proj, na_rpb[layer], na_meta_bias[layer], **kw)
        y_sc = short_conv_mixer(proj, sc_conv[layer], **kw)
        y_hy = hyena_mixer(proj, hy_conv[layer], hy_w1[layer], hy_b1[layer], hy_w2[layer], hy_b2[layer],
                           hy_w3[layer], hy_freq[layer], hy_decay[layer], hy_skip[layer], dft, **kw)
        h = mixer_out(h, y_na, y_sc, y_hy, proj, w_na_out[layer], w_sc_out[layer], w_hy_out[layer], w_o[layer])
        j = layer // 2
        if layer % 2 == 0:
            h = dense_ffn(h, norm_ffn[layer], ffn_w1[j], ffn_w3[j], ffn_w2[j])
        else:
            g_final = norm_final if layer == depth - 1 else None
            h = moe_swiglu(h, norm_ffn[layer], router_w[j], moe_w1[j], moe_w3[j], moe_w2[j], g_final, **kw)
    if depth % 2 == 0:
        return h
    out = rms_norm_rows(h, norm_final, out_dtype=x.dtype)
    return out.reshape(bsz, length, d)[:, N_META:]
```

```python
import functools
import math

import numpy as np
import jax
import jax.numpy as jnp
from jax import lax
from jax.experimental import pallas as pl
from jax.experimental.pallas import tpu as pltpu

D_MODEL = 2048
N_META = 16
GRID_W = 64
GRID_ROWS = 32
NA_HEADS = 16
NA_HEAD_DIM = 64
NA_WIDTH = NA_HEADS * NA_HEAD_DIM
WIN_ROWS = 8
WIN_COLS = 16
SC_WIDTH = 512
HY_WIDTH = 512
HY_ORDER = 2
HY_EMB = 33
HY_HIDDEN = 64
HY_MOD_SHIFT = 0.05
PROJ_WIDTH = 3 * NA_WIDTH + 3 * SC_WIDTH + 3 * HY_WIDTH + 3 * D_MODEL
D_FF = 5632
N_EXPERTS = 8
TOP_K = 2
D_FF_EXPERT = 7168
MOE_BLOCK = 512
NORM_EPS = 1e-6
NEG_INF = -1e30

COL_Q, COL_K, COL_V = 0, NA_WIDTH, 2 * NA_WIDTH
COL_SCB = 3 * NA_WIDTH
COL_SCC = COL_SCB + SC_WIDTH
COL_SCU = COL_SCC + SC_WIDTH
COL_HYU = COL_SCU + SC_WIDTH
COL_GNA = COL_HYU + 3 * HY_WIDTH
COL_GSC = COL_GNA + D_MODEL
COL_GHY = COL_GSC + D_MODEL

V7X_VMEM_BYTES = 64 * 1024 * 1024
VMEM_LIMIT = V7X_VMEM_BYTES - 8 * 1024 * 1024

ROW_TILE = 768
NORM_CHUNK = 16
FFT_N = 2 * (N_META + GRID_ROWS * GRID_W)
FFT_KP = 2080
FFT_KB = 1040
INV_NB = 2064

F32 = jnp.float32
BF16 = jnp.bfloat16


def _params(sem, vmem_limit=VMEM_LIMIT):
    return pltpu.CompilerParams(dimension_semantics=sem, vmem_limit_bytes=vmem_limit)


def _sigmoid(a):
    return 0.5 * jnp.tanh(0.5 * a) + 0.5


CAST_BLOCK_BYTES = 8 * 1024 * 1024


def _cast_kernel(x_ref, o_ref):
    o_ref[...] = x_ref[...].astype(o_ref.dtype)


def to_bf16(w):
    cols = w.shape[-1]
    w2 = w.reshape(-1, cols)
    rows = w2.shape[0]
    tr = min(rows, CAST_BLOCK_BYTES // (4 * cols)) // 16 * 16
    while rows % tr:
        tr -= 16
    out = pl.pallas_call(
        _cast_kernel,
        out_shape=jax.ShapeDtypeStruct((rows, cols), BF16),
        grid=(rows // tr,),
        in_specs=[pl.BlockSpec((tr, cols), lambda i: (i, 0))],
        out_specs=pl.BlockSpec((tr, cols), lambda i: (i, 0)),
        compiler_params=_params(("parallel",)),
        name="to_bf16",
    )(w2)
    return out.reshape(w.shape)


def _rms_norm_rows(x_ref, g_ref, xn_ref):
    rows = x_ref.shape[0]
    g = g_ref[...]

    def chunk(c, carry):
        r0 = pl.multiple_of(c * NORM_CHUNK, NORM_CHUNK)
        x = x_ref[pl.ds(r0, NORM_CHUNK), :]
        ms = jnp.mean(x * x, axis=-1, keepdims=True)
        xn_ref[pl.ds(r0, NORM_CHUNK), :] = (x * lax.rsqrt(ms + NORM_EPS) * g).astype(xn_ref.dtype)
        return carry

    lax.fori_loop(0, rows // NORM_CHUNK, chunk, 0, unroll=4)


def _norm_mm_kernel(x_ref, g_ref, w_ref, o_ref, xn_ref):
    @pl.when(pl.program_id(1) == 0)
    def _():
        _rms_norm_rows(x_ref, g_ref, xn_ref)

    o_ref[...] = jnp.dot(xn_ref[...], w_ref[...], preferred_element_type=F32).astype(o_ref.dtype)


def norm_matmul(x, g, w, *, tn, out_dtype=BF16, tm=ROW_TILE):
    t, k = x.shape
    n = w.shape[1]
    return pl.pallas_call(
        _norm_mm_kernel,
        out_shape=jax.ShapeDtypeStruct((t, n), out_dtype),
        grid=(t // tm, n // tn),
        in_specs=[
            pl.BlockSpec((tm, k), lambda i, j: (i, 0)),
            pl.BlockSpec((1, k), lambda i, j: (0, 0)),
            pl.BlockSpec((k, tn), lambda i, j: (0, j)),
        ],
        out_specs=pl.BlockSpec((tm, tn), lambda i, j: (i, j)),
        scratch_shapes=[pltpu.VMEM((tm, k), BF16)],
        compiler_params=_params(("parallel", "arbitrary")),
        name="norm_matmul",
    )(x, g.reshape(1, k), w)


EMBED_TILES = 3


def _embed_norm_mm_kernel(x_ref, meta_ref, g_ref, w_ref, o_ref, h_ref, xn_ref):
    tm = h_ref.shape[0]
    first = pl.program_id(0) % EMBED_TILES == 0

    @pl.when(jnp.logical_and(pl.program_id(1) == 0, first))
    def _():
        h_ref[0:N_META, :] = meta_ref[...]
        h_ref[N_META:tm, :] = x_ref[0, 0:tm - N_META, :]

    @pl.when(jnp.logical_and(pl.program_id(1) == 0, jnp.logical_not(first)))
    def _():
        h_ref[...] = x_ref[0]

    @pl.when(pl.program_id(1) == 0)
    def _():
        _rms_norm_rows(h_ref, g_ref, xn_ref)

    o_ref[...] = jnp.dot(xn_ref[...], w_ref[...], preferred_element_type=F32).astype(o_ref.dtype)


def embed_norm_matmul(x, meta_tokens, g, w, *, tn):
    bsz, seq, k = x.shape
    length = N_META + seq
    tm = length // EMBED_TILES
    n = w.shape[1]

    def x_rows(i, j):
        tile = i % EMBED_TILES
        start = jnp.maximum(tile * tm - N_META, 0)
        return i // EMBED_TILES, pl.multiple_of(start, N_META), 0

    return pl.pallas_call(
        _embed_norm_mm_kernel,
        out_shape=(jax.ShapeDtypeStruct((bsz * length, n), BF16), jax.ShapeDtypeStruct((bsz * length, k), F32)),
        grid=(bsz * EMBED_TILES, n // tn),
        in_specs=[
            pl.BlockSpec((pl.Element(1), pl.Element(tm), pl.Element(k)), x_rows),
            pl.BlockSpec((N_META, k), lambda i, j: (0, 0)),
            pl.BlockSpec((1, k), lambda i, j: (0, 0)),
            pl.BlockSpec((k, tn), lambda i, j: (0, j)),
        ],
        out_specs=(pl.BlockSpec((tm, tn), lambda i, j: (i, j)), pl.BlockSpec((tm, k), lambda i, j: (i, 0))),
        scratch_shapes=[pltpu.VMEM((tm, k), BF16)],
        compiler_params=_params(("parallel", "arbitrary")),
        name="embed_norm_matmul",
    )(x, meta_tokens.astype(F32), g.reshape(1, k), w)


def _norm_router_kernel(x_ref, g_ref, w_ref, o_ref, xb_ref, xn_ref):
    _rms_norm_rows(x_ref, g_ref, xn_ref)
    xn = xn_ref[...]
    x_hi = xn.astype(BF16)
    x_lo = (xn - x_hi.astype(F32)).astype(BF16)
    w = w_ref[...]
    w_hi = w.astype(BF16)
    w_lo = (w - w_hi.astype(F32)).astype(BF16)
    xb_ref[...] = x_hi
    logits = jnp.dot(x_hi, w_hi, preferred_element_type=F32)
    logits += jnp.dot(x_lo, w_hi, preferred_element_type=F32)
    logits += jnp.dot(x_hi, w_lo, preferred_element_type=F32)
    o_ref[...] = logits


def norm_router(x, g, w, *, tm=ROW_TILE):
    t, k = x.shape
    n = w.shape[1]
    return pl.pallas_call(
        _norm_router_kernel,
        out_shape=(jax.ShapeDtypeStruct((t, n), F32), jax.ShapeDtypeStruct((t, k), BF16)),
        grid=(t // tm,),
        in_specs=[
            pl.BlockSpec((tm, k), lambda i: (i, 0)),
            pl.BlockSpec((1, k), lambda i: (0, 0)),
            pl.BlockSpec((k, n), lambda i: (0, 0)),
        ],
        out_specs=(pl.BlockSpec((tm, n), lambda i: (i, 0)), pl.BlockSpec((tm, k), lambda i: (i, 0))),
        scratch_shapes=[pltpu.VMEM((tm, k), F32)],
        compiler_params=_params(("parallel",)),
        name="norm_router",
    )(x, g.reshape(1, k), w)


def _norm_kernel(x_ref, g_ref, o_ref):
    _rms_norm_rows(x_ref, g_ref, o_ref)


def rms_norm_rows(x, g, *, out_dtype, tm=ROW_TILE):
    t, k = x.shape
    return pl.pallas_call(
        _norm_kernel,
        out_shape=jax.ShapeDtypeStruct((t, k), out_dtype),
        grid=(t // tm,),
        in_specs=[pl.BlockSpec((tm, k), lambda i: (i, 0)), pl.BlockSpec((1, k), lambda i: (0, 0))],
        out_specs=pl.BlockSpec((tm, k), lambda i: (i, 0)),
        compiler_params=_params(("parallel",)),
        name="rms_norm",
    )(x, g.reshape(1, k))


def _dense_ffn_kernel(x_ref, g_ref, w1_ref, w3_ref, w2_ref, o_ref, xn_ref):
    @pl.when(pl.program_id(1) == 0)
    def _():
        _rms_norm_rows(x_ref, g_ref, xn_ref)
        o_ref[...] = x_ref[...]

    xn = xn_ref[...]
    a = jnp.dot(xn, w1_ref[...], preferred_element_type=F32)
    b = jnp.dot(xn, w3_ref[...], preferred_element_type=F32)
    hid = (a * _sigmoid(a) * b).astype(BF16)
    o_ref[...] += jnp.dot(hid, w2_ref[...], preferred_element_type=F32)


def dense_ffn(x, g, w1, w3, w2, *, fc=512, tm=ROW_TILE):
    t, d = x.shape
    f = w1.shape[1]
    return pl.pallas_call(
        _dense_ffn_kernel,
        out_shape=jax.ShapeDtypeStruct((t, d), F32),
        grid=(t // tm, f // fc),
        in_specs=[
            pl.BlockSpec((tm, d), lambda i, j: (i, 0)),
            pl.BlockSpec((1, d), lambda i, j: (0, 0)),
            pl.BlockSpec((d, fc), lambda i, j: (0, j)),
            pl.BlockSpec((d, fc), lambda i, j: (0, j)),
            pl.BlockSpec((fc, d), lambda i, j: (j, 0)),
        ],
        out_specs=pl.BlockSpec((tm, d), lambda i, j: (i, 0)),
        scratch_shapes=[pltpu.VMEM((tm, d), BF16)],
        compiler_params=_params(("parallel", "arbitrary")),
        name="dense_ffn",
    )(x, g.reshape(1, d), w1, w3, w2)


def _mixer_out_kernel(res_ref, yna_ref, ysc_ref, yhy_ref, gna_ref, gsc_ref, ghy_ref, wna_ref, wsc_ref, why_ref,
                      wo_ref, o_ref):
    @pl.when(pl.program_id(1) == 0)
    def _():
        o_ref[...] = res_ref[...]

    m = _sigmoid(gna_ref[...].astype(F32)) * jnp.dot(yna_ref[...], wna_ref[...], preferred_element_type=F32)
    m += _sigmoid(gsc_ref[...].astype(F32)) * jnp.dot(ysc_ref[...], wsc_ref[...], preferred_element_type=F32)
    m += _sigmoid(ghy_ref[...].astype(F32)) * jnp.dot(yhy_ref[...], why_ref[...], preferred_element_type=F32)
    o_ref[...] += jnp.dot(m.astype(BF16), wo_ref[...], preferred_element_type=F32)


MIXER_ROWS = 256


def mixer_out(res, y_na, y_sc, y_hy, proj, w_na, w_sc, w_hy, w_o, *, cw=D_MODEL, tm=MIXER_ROWS):
    t, d = res.shape
    return pl.pallas_call(
        _mixer_out_kernel,
        out_shape=jax.ShapeDtypeStruct((t, d), F32),
        grid=(t // tm, d // cw),
        in_specs=[
            pl.BlockSpec((tm, d), lambda i, c: (i, 0)),
            pl.BlockSpec((tm, NA_WIDTH), lambda i, c: (i, 0)),
            pl.BlockSpec((tm, SC_WIDTH), lambda i, c: (i, 0)),
            pl.BlockSpec((tm, HY_WIDTH), lambda i, c: (i, 0)),
            pl.BlockSpec((tm, cw), lambda i, c: (i, COL_GNA // cw + c)),
            pl.BlockSpec((tm, cw), lambda i, c: (i, COL_GSC // cw + c)),
            pl.BlockSpec((tm, cw), lambda i, c: (i, COL_GHY // cw + c)),
            pl.BlockSpec((NA_WIDTH, cw), lambda i, c: (0, c)),
            pl.BlockSpec((SC_WIDTH, cw), lambda i, c: (0, c)),
            pl.BlockSpec((HY_WIDTH, cw), lambda i, c: (0, c)),
            pl.BlockSpec((cw, d), lambda i, c: (c, 0)),
        ],
        out_specs=pl.BlockSpec((tm, d), lambda i, c: (i, 0)),
        compiler_params=_params(("parallel", "arbitrary")),
        name="mixer_out",
    )(res, y_na, y_sc, y_hy, proj, proj, proj, w_na, w_sc, w_hy, w_o)


def _na_bias_table(rpb):
    qc = np.arange(GRID_W)[:, None]
    kc = np.arange(GRID_W)[None, :]
    win_start = np.clip(qc - WIN_COLS // 2, 0, GRID_W - WIN_COLS)
    inside = (kc >= win_start) & (kc < win_start + WIN_COLS)
    dc = np.clip(kc - qc, -(WIN_COLS - 1), WIN_COLS - 1) + WIN_COLS - 1
    dr = np.arange(WIN_ROWS)[None, :] - np.arange(WIN_ROWS)[:, None] + WIN_ROWS - 1
    onehot = np.zeros((2 * WIN_COLS - 1, GRID_W * GRID_W), np.float32)
    onehot[dc.reshape(-1), np.arange(GRID_W * GRID_W)] = 1.0
    tbl = jnp.einsum('htid,dj->htij', rpb.astype(F32)[:, dr], jnp.asarray(onehot),
                     precision=lax.Precision.HIGHEST)
    tbl = tbl.reshape(NA_HEADS, WIN_ROWS, WIN_ROWS, GRID_W, GRID_W)
    tbl = jnp.where(jnp.asarray(inside)[None, None, None], tbl, NEG_INF)
    tbl = tbl.transpose(0, 1, 3, 2, 4)
    return tbl.reshape(NA_HEADS, WIN_ROWS, GRID_W, WIN_ROWS * GRID_W)


NA_SOFTMAX_LAG = 1
NA_PV_LAG = 2


def _na_kernel(q_ref, k_ref, v_ref, tbl_ref, mb_ref, o_ref):
    scale = NA_HEAD_DIM ** -0.5
    lane = lax.broadcasted_iota(jnp.int32, (1, 2 * NA_HEAD_DIM), 1)
    first = lane < NA_HEAD_DIM
    nt = (((1,), (1,)), ((), ()))
    k_meta = k_ref[0:N_META, :]
    v_meta = v_ref[0:N_META, :]
    mb_grid = mb_ref[0, 0]
    band = WIN_ROWS * GRID_W

    def stack_heads(q):
        q = q * scale
        zero = jnp.zeros_like(q)
        return jnp.concatenate([jnp.where(first, q, zero), jnp.where(first, zero, q)], axis=0)

    def unstack_heads(out, den, n):
        out = out / den
        return jnp.where(first, out[:n], out[n:])

    def softmax_parts(s_grid, s_meta):
        m = jnp.max(s_meta, axis=-1, keepdims=True)
        if s_grid is not None:
            m = jnp.maximum(m, jnp.max(s_grid, axis=-1, keepdims=True))
        p_meta = jnp.exp(s_meta - m)
        den = jnp.sum(p_meta, axis=-1, keepdims=True)
        p_grid = None
        if s_grid is not None:
            p_grid = jnp.exp(s_grid - m)
            den = den + jnp.sum(p_grid, axis=-1, keepdims=True)
            p_grid = p_grid.astype(BF16)
        return p_grid, p_meta.astype(BF16), den

    def geometry(r):
        rs = min(max(r - WIN_ROWS // 2, 0), GRID_ROWS - WIN_ROWS)
        return N_META + r * GRID_W, N_META + rs * GRID_W, r - rs

    def scores(r):
        if r is None:
            q2 = stack_heads(q_ref[0:N_META, :])
            return None, lax.dot_general(q2, k_meta, nt, preferred_element_type=F32) + mb_ref[0, 1, 0:2 * N_META, :]
        q0, k0, typ = geometry(r)
        q2 = stack_heads(q_ref[q0:q0 + GRID_W, :])
        s_grid = lax.dot_general(q2, k_ref[k0:k0 + band, :], nt, preferred_element_type=F32) + tbl_ref[0, typ]
        s_meta = lax.dot_general(q2, k_meta, nt, preferred_element_type=F32) + mb_grid
        return s_grid, s_meta

    def weighted_values(r, p_grid, p_meta, den):
        out = jnp.dot(p_meta, v_meta, preferred_element_type=F32)
        if r is None:
            o_ref[0:N_META, :] = unstack_heads(out, den, N_META).astype(o_ref.dtype)
            return
        q0, k0, _ = geometry(r)
        out += jnp.dot(p_grid, v_ref[k0:k0 + band, :], preferred_element_type=F32)
        o_ref[q0:q0 + GRID_W, :] = unstack_heads(out, den, GRID_W).astype(o_ref.dtype)

    units = [None] + list(range(GRID_ROWS))
    pending_scores, pending_probs = {}, {}
    for t in range(len(units) + NA_PV_LAG):
        if t < len(units):
            pending_scores[t] = scores(units[t])
        if 0 <= t - NA_SOFTMAX_LAG < len(units):
            pending_probs[t - NA_SOFTMAX_LAG] = softmax_parts(*pending_scores.pop(t - NA_SOFTMAX_LAG))
        if 0 <= t - NA_PV_LAG < len(units):
            weighted_values(units[t - NA_PV_LAG], *pending_probs.pop(t - NA_PV_LAG))


def neighbourhood_attention(proj, rpb, meta_bias, *, bsz, length):
    hp = 2 * NA_HEAD_DIM
    n_pairs = NA_HEADS // 2
    band = WIN_ROWS * GRID_W
    tbl = _na_bias_table(rpb).reshape(n_pairs, 2, WIN_ROWS, GRID_W, band).transpose(0, 2, 1, 3, 4)
    tbl = tbl.reshape(n_pairs, WIN_ROWS, 2 * GRID_W, band)
    mb = meta_bias.astype(F32).reshape(n_pairs, 2, 1, N_META)
    mb_grid = jnp.broadcast_to(mb, (n_pairs, 2, GRID_W, N_META)).reshape(n_pairs, 2 * GRID_W, N_META)
    mb_meta = jnp.broadcast_to(mb, (n_pairs, 2, N_META, N_META)).reshape(n_pairs, 2 * N_META, N_META)
    mb_meta = jnp.pad(mb_meta, ((0, 0), (0, 2 * GRID_W - 2 * N_META), (0, 0)))
    mb2 = jnp.stack([mb_grid, mb_meta], axis=1)
    return pl.pallas_call(
        _na_kernel,
        out_shape=jax.ShapeDtypeStruct((bsz * length, NA_WIDTH), BF16),
        grid=(n_pairs, bsz),
        in_specs=[
            pl.BlockSpec((length, hp), lambda h, b: (b, COL_Q // hp + h)),
            pl.BlockSpec((length, hp), lambda h, b: (b, COL_K // hp + h)),
            pl.BlockSpec((length, hp), lambda h, b: (b, COL_V // hp + h)),
            pl.BlockSpec((1, WIN_ROWS, 2 * GRID_W, band), lambda h, b: (h, 0, 0, 0)),
            pl.BlockSpec((1, 2, 2 * GRID_W, N_META), lambda h, b: (h, 0, 0, 0)),
        ],
        out_specs=pl.BlockSpec((length, hp), lambda h, b: (b, h)),
        compiler_params=_params(("parallel", "parallel")),
        name="neighbourhood_attention",
    )(proj, proj, proj, tbl, mb2)


def _dwconv3(u, w):
    length = u.shape[0]
    row = lax.broadcasted_iota(jnp.int32, u.shape, 0)
    prev = jnp.where(row == 0, 0.0, pltpu.roll(u, 1, 0))
    nxt = jnp.where(row == length - 1, 0.0, pltpu.roll(u, length - 1, 0))
    return prev * w[0:1] + u * w[1:2] + nxt * w[2:3]


def _gated_conv_kernel(b_ref, c_ref, u_ref, w_ref, o_ref):
    u = c_ref[...].astype(F32) * u_ref[...].astype(F32)
    o_ref[...] = (b_ref[...].astype(F32) * _dwconv3(u, w_ref[...])).astype(o_ref.dtype)


def _plain_conv_kernel(u_ref, w_ref, o_ref):
    o_ref[...] = _dwconv3(u_ref[...].astype(F32), w_ref[...]).astype(o_ref.dtype)


CONV_CB = 512


def short_conv_mixer(proj, w_conv, *, bsz, length):
    cb = CONV_CB
    return pl.pallas_call(
        _gated_conv_kernel,
        out_shape=jax.ShapeDtypeStruct((bsz * length, SC_WIDTH), BF16),
        grid=(bsz, SC_WIDTH // cb),
        in_specs=[
            pl.BlockSpec((length, cb), lambda b, c: (b, COL_SCB // cb + c)),
            pl.BlockSpec((length, cb), lambda b, c: (b, COL_SCC // cb + c)),
            pl.BlockSpec((length, cb), lambda b, c: (b, COL_SCU // cb + c)),
            pl.BlockSpec((3, cb), lambda b, c: (0, c)),
        ],
        out_specs=pl.BlockSpec((length, cb), lambda b, c: (b, c)),
        compiler_params=_params(("parallel", "parallel")),
        name="short_conv_mixer",
    )(proj, proj, proj, w_conv.astype(F32))


def hyena_in_conv(proj, w_conv, *, bsz, length):
    cb = CONV_CB
    return pl.pallas_call(
        _plain_conv_kernel,
        out_shape=jax.ShapeDtypeStruct((bsz * length, 3 * HY_WIDTH), BF16),
        grid=(bsz, 3 * HY_WIDTH // cb),
        in_specs=[
            pl.BlockSpec((length, cb), lambda b, c: (b, COL_HYU // cb + c)),
            pl.BlockSpec((3, cb), lambda b, c: (0, c)),
        ],
        out_specs=pl.BlockSpec((length, cb), lambda b, c: (b, c)),
        compiler_params=_params(("parallel", "parallel")),
        name="hyena_in_conv",
    )(proj, w_conv.astype(F32))


def _filter_kernel(z_ref, w1_ref, b1_ref, w2_ref, b2_ref, w3_ref, fr_ref, dec_ref, f_ref, s_ref):
    hi = lax.Precision.HIGHEST
    z = z_ref[...]
    fr = fr_ref[...]
    h = jnp.sin(fr * (jnp.dot(z, w1_ref[...], preferred_element_type=F32, precision=hi) + b1_ref[...]))
    h = jnp.sin(fr * (jnp.dot(h, w2_ref[...], preferred_element_type=F32, precision=hi) + b2_ref[...]))
    h = jnp.dot(h, w3_ref[...], preferred_element_type=F32, precision=hi)
    t = z[:, 0:1]
    f = h * (jnp.exp(-t * jnp.abs(dec_ref[0])) + HY_MOD_SHIFT)
    row = lax.broadcasted_iota(jnp.int32, f.shape, 0)
    backward = pl.program_id(0) % 2 == 1
    f = jnp.where(jnp.logical_and(row == 0, backward), 0.0, f)
    f_ref[0] = f.astype(f_ref.dtype)
    s_ref[0] = jnp.sum(jnp.abs(f), axis=0, keepdims=True)


def hyena_filters(length, w1, b1, w2, b2, w3, freq, decay):
    bands = (HY_EMB - 1) // 2
    t = jnp.linspace(0.0, 1.0, length, dtype=F32)[:, None]
    ang = (2.0 * math.pi / length) * jnp.arange(length, dtype=F32)[:, None]
    fb = jnp.linspace(1e-4, bands - 1, bands, dtype=F32)[None, :]
    z = jnp.concatenate([t, jnp.cos(fb * ang), -jnp.sin(fb * ang)], axis=-1)
    emb = 128
    z = jnp.pad(z, ((0, 0), (0, emb - HY_EMB)))
    w1p = jnp.pad(w1.astype(F32), ((0, emb - HY_EMB), (0, 0)))
    n_slab = HY_ORDER * 2
    row = lambda a: a.astype(F32).reshape(1, -1)
    return pl.pallas_call(
        _filter_kernel,
        out_shape=(jax.ShapeDtypeStruct((n_slab, length, HY_WIDTH), BF16),
                   jax.ShapeDtypeStruct((n_slab, 1, HY_WIDTH), F32)),
        grid=(n_slab,),
        in_specs=[
            pl.BlockSpec((length, emb), lambda j: (0, 0)),
            pl.BlockSpec((emb, HY_HIDDEN), lambda j: (0, 0)),
            pl.BlockSpec((1, HY_HIDDEN), lambda j: (0, 0)),
            pl.BlockSpec((HY_HIDDEN, HY_HIDDEN), lambda j: (0, 0)),
            pl.BlockSpec((1, HY_HIDDEN), lambda j: (0, 0)),
            pl.BlockSpec((HY_HIDDEN, HY_WIDTH), lambda j: (0, j)),
            pl.BlockSpec((1, HY_HIDDEN), lambda j: (0, 0)),
            pl.BlockSpec((1, 1, HY_WIDTH), lambda j: (j, 0, 0)),
        ],
        out_specs=(pl.BlockSpec((1, length, HY_WIDTH), lambda j: (j, 0, 0)),
                   pl.BlockSpec((1, 1, HY_WIDTH), lambda j: (j, 0, 0))),
        compiler_params=_params(("parallel",)),
        name="hyena_filters",
    )(z, w1p, row(b1), w2.astype(F32), row(b2), w3.astype(F32), row(freq),
      decay.astype(F32).reshape(n_slab, 1, HY_WIDTH))


def _dft_matrices(length):
    k = jnp.arange(FFT_KP, dtype=jnp.int32)[:, None]
    n = jnp.arange(length, dtype=jnp.int32)[None, :]
    theta = (2.0 * math.pi / FFT_N) * ((k * n) % FFT_N).astype(F32)
    valid = k <= FFT_N // 2
    fc = jnp.where(valid, jnp.cos(theta), 0.0).astype(BF16)
    fs = jnp.where(valid, -jnp.sin(theta), 0.0).astype(BF16)
    fc, fs = lax.optimization_barrier((fc, fs))
    return fc, fs, fc.T, fs.T


def _dft_raw_kernel(fc_ref, fs_ref, u_ref, re_ref, im_ref):
    u = u_ref[0]
    re_ref[0] = jnp.dot(fc_ref[...], u, preferred_element_type=F32)
    im_ref[0] = jnp.dot(fs_ref[...], u, preferred_element_type=F32)


def dft_filters(fc, fs, filt):
    n_slab, length, c = filt.shape
    out = jax.ShapeDtypeStruct((n_slab, FFT_KP, c), F32)
    return pl.pallas_call(
        _dft_raw_kernel,
        out_shape=(out, out),
        grid=(FFT_KP // FFT_KB, n_slab),
        in_specs=[
            pl.BlockSpec((FFT_KB, length), lambda kb, s: (kb, 0)),
            pl.BlockSpec((FFT_KB, length), lambda kb, s: (kb, 0)),
            pl.BlockSpec((1, length, c), lambda kb, s: (s, 0, 0)),
        ],
        out_specs=(pl.BlockSpec((1, FFT_KB, c), lambda kb, s: (s, kb, 0)),
                   pl.BlockSpec((1, FFT_KB, c), lambda kb, s: (s, kb, 0))),
        compiler_params=_params(("parallel", "parallel")),
        name="dft_filters",
    )(fc, fs, filt)


def _dft_mul_kernel(fc_ref, fs_ref, u_ref, are_ref, aim_ref, s_ref, yre_ref, yim_ref, g_ref):
    @pl.when(pl.program_id(1) == 0)
    def _():
        k = pl.program_id(0) * FFT_KB + lax.broadcasted_iota(jnp.int32, (FFT_KB, 1), 0)
        wk = jnp.where(jnp.logical_or(k == 0, k == FFT_N // 2), 1.0, 2.0)
        wk = jnp.where(k > FFT_N // 2, 0.0, wk) / FFT_N
        inv = wk / (s_ref[0] + s_ref[1])
        g_ref[0] = (are_ref[0] + are_ref[1]) * inv
        g_ref[1] = (aim_ref[0] - aim_ref[1]) * inv

    u = u_ref[...]
    re = jnp.dot(fc_ref[...], u, preferred_element_type=F32)
    im = jnp.dot(fs_ref[...], u, preferred_element_type=F32)
    gre = g_ref[0]
    gim = g_ref[1]
    yre_ref[0] = (re * gre - im * gim).astype(yre_ref.dtype)
    yim_ref[0] = (re * gim + im * gre).astype(yim_ref.dtype)


def dft_forward_mul(fc, fs, u, col, a_re, a_im, abs_sum, order, *, bsz, length):
    c = HY_WIDTH
    out = jax.ShapeDtypeStruct((bsz, FFT_KP, c), BF16)
    return pl.pallas_call(
        _dft_mul_kernel,
        out_shape=(out, out),
        grid=(FFT_KP // FFT_KB, bsz),
        in_specs=[
            pl.BlockSpec((FFT_KB, length), lambda kb, b: (kb, 0)),
            pl.BlockSpec((FFT_KB, length), lambda kb, b: (kb, 0)),
            pl.BlockSpec((length, c), lambda kb, b: (b, col)),
            pl.BlockSpec((2, FFT_KB, c), lambda kb, b: (order, kb, 0)),
            pl.BlockSpec((2, FFT_KB, c), lambda kb, b: (order, kb, 0)),
            pl.BlockSpec((2, 1, c), lambda kb, b: (order, 0, 0)),
        ],
        out_specs=(pl.BlockSpec((1, FFT_KB, c), lambda kb, b: (b, kb, 0)),
                   pl.BlockSpec((1, FFT_KB, c), lambda kb, b: (b, kb, 0))),
        scratch_shapes=[pltpu.VMEM((2, FFT_KB, c), F32)],
        compiler_params=_params(("parallel", "arbitrary")),
        name="dft_forward_mul",
    )(fc, fs, u, a_re, a_im, abs_sum)


def _dft_inv_kernel(ct_ref, st_ref, yre_ref, yim_ref, u_ref, gate_ref, skip_ref, o_ref):
    y = jnp.dot(ct_ref[...], yre_ref[0], preferred_element_type=F32)
    y += jnp.dot(st_ref[...], yim_ref[0], preferred_element_type=F32)
    y += u_ref[...].astype(F32) * skip_ref[0]
    o_ref[...] = (gate_ref[...].astype(F32) * y).astype(o_ref.dtype)


def dft_inverse_gate(fct, fst, yre, yim, u, u_col, gate, gate_col, skip, order, *, bsz, length):
    c = HY_WIDTH
    nblk = length // INV_NB
    return pl.pallas_call(
        _dft_inv_kernel,
        out_shape=jax.ShapeDtypeStruct((bsz * length, c), BF16),
        grid=(nblk, bsz),
        in_specs=[
            pl.BlockSpec((INV_NB, FFT_KP), lambda nb, b: (nb, 0)),
            pl.BlockSpec((INV_NB, FFT_KP), lambda nb, b: (nb, 0)),
            pl.BlockSpec((1, FFT_KP, c), lambda nb, b: (b, 0, 0)),
            pl.BlockSpec((1, FFT_KP, c), lambda nb, b: (b, 0, 0)),
            pl.BlockSpec((INV_NB, c), lambda nb, b: (b * nblk + nb, u_col)),
            pl.BlockSpec((INV_NB, c), lambda nb, b: (b * nblk + nb, gate_col)),
            pl.BlockSpec((1, 1, c), lambda nb, b: (order, 0, 0)),
        ],
        out_specs=pl.BlockSpec((INV_NB, c), lambda nb, b: (b * nblk + nb, 0)),
        compiler_params=_params(("parallel", "parallel")),
        name="dft_inverse_gate",
    )(fct, fst, yre, yim, u, gate, skip)


def hyena_mixer(proj, w_conv, w1, b1, w2, b2, w3, freq, decay, skip, dft, *, bsz, length):
    fc, fs, fct, fst = dft
    vxx = hyena_in_conv(proj, w_conv, bsz=bsz, length=length)
    filt, abs_sum = hyena_filters(length, w1, b1, w2, b2, w3, freq, decay)
    a_re, a_im = dft_filters(fc, fs, filt)
    skip = skip.astype(F32).reshape(HY_ORDER, 1, HY_WIDTH)
    kw = dict(bsz=bsz, length=length)
    yre, yim = dft_forward_mul(fc, fs, vxx, 0, a_re, a_im, abs_sum, 0, **kw)
    z = dft_inverse_gate(fct, fst, yre, yim, vxx, 0, vxx, 1, skip, 0, **kw)
    yre, yim = dft_forward_mul(fc, fs, z, 0, a_re, a_im, abs_sum, 1, **kw)
    return dft_inverse_gate(fct, fst, yre, yim, z, 0, vxx, 2, skip, 1, **kw)


MOE_SUPER = 2 * MOE_BLOCK
MOE_FC = 512
MOE_VMEM_LIMIT = V7X_VMEM_BYTES - 3 * 1024 * 1024


def _moe_kernel(be_ref, nh_ref, x_ref, w1_ref, w3_ref, w2_ref, yb_ref, o_ref, acc_ref):
    del be_ref, yb_ref
    i = pl.program_id(0)
    j = pl.program_id(1)
    halves = nh_ref[i]

    @pl.when(j == 0)
    def _():
        acc_ref[...] = jnp.zeros_like(acc_ref)

    def ffn(rows):
        x = x_ref[0:rows, :]
        a = jnp.dot(x, w1_ref[0].astype(BF16), preferred_element_type=F32)
        b = jnp.dot(x, w3_ref[0].astype(BF16), preferred_element_type=F32)
        hid = (a * _sigmoid(a) * b).astype(BF16)
        acc_ref[0:rows, :] += jnp.dot(hid, w2_ref[0].astype(BF16), preferred_element_type=F32)

    @pl.when(halves == 2)
    def _():
        ffn(MOE_SUPER)

    @pl.when(halves == 1)
    def _():
        ffn(MOE_BLOCK)

    @pl.when(j == pl.num_programs(1) - 1)
    def _():
        o_ref[...] = acc_ref[...].astype(o_ref.dtype)


def moe_experts(xb, sb_start, block_expert, n_halves, w1, w3, w2, yb):
    d = xb.shape[1]
    n_super = xb.shape[0] // MOE_SUPER
    fc = MOE_FC
    n_fc = D_FF_EXPERT // fc

    def chunk(i, j, nh):
        return jnp.where(nh[i] > 0, j, n_fc - 1)

    grid_spec = pltpu.PrefetchScalarGridSpec(
        num_scalar_prefetch=2,
        grid=(n_super, n_fc),
        in_specs=[
            pl.BlockSpec((MOE_SUPER, d), lambda i, j, be, nh: (i, 0)),
            pl.BlockSpec((1, d, fc), lambda i, j, be, nh: (be[i], 0, chunk(i, j, nh))),
            pl.BlockSpec((1, d, fc), lambda i, j, be, nh: (be[i], 0, chunk(i, j, nh))),
            pl.BlockSpec((1, fc, d), lambda i, j, be, nh: (be[i], chunk(i, j, nh), 0)),
            pl.BlockSpec(memory_space=pl.ANY),
        ],
        out_specs=pl.BlockSpec((MOE_SUPER, d), lambda i, j, be, nh: (sb_start + i, 0)),
        scratch_shapes=[pltpu.VMEM((MOE_SUPER, d), F32)],
    )
    return pl.pallas_call(
        _moe_kernel,
        out_shape=jax.ShapeDtypeStruct(yb.shape, yb.dtype),
        grid_spec=grid_spec,
        input_output_aliases={6: 0},
        compiler_params=_params(("parallel", "arbitrary"), MOE_VMEM_LIMIT),
        name="moe_experts",
    )(block_expert, n_halves, xb, w1, w3, w2, yb)


COMBINE_ROWS = 512


def _moe_combine_kernel(h_ref, y0_ref, y1_ref, g0_ref, g1_ref, gn_ref, o_ref, *, final):
    rows = o_ref.shape[0]
    gn = gn_ref[...]

    def chunk(c, carry):
        r0 = pl.multiple_of(c * NORM_CHUNK, NORM_CHUNK)
        sl = pl.ds(r0, NORM_CHUNK)
        h = h_ref[0, sl, :] if final else h_ref[sl, :]
        y = h + g0_ref[sl, :] * y0_ref[sl, :].astype(F32) + g1_ref[sl, :] * y1_ref[sl, :].astype(F32)
        if final:
            ms = jnp.mean(y * y, axis=-1, keepdims=True)
            y = y * lax.rsqrt(ms + NORM_EPS) * gn
        o_ref[sl, :] = y.astype(o_ref.dtype)
        return carry

    lax.fori_loop(0, rows // NORM_CHUNK, chunk, 0, unroll=4)


def moe_combine(h, y0, y1, g0, g1, *, tm=ROW_TILE):
    t, d = h.shape
    row = lambda w: pl.BlockSpec((tm, w), lambda i: (i, 0))
    return pl.pallas_call(
        functools.partial(_moe_combine_kernel, final=False),
        out_shape=jax.ShapeDtypeStruct((t, d), F32),
        grid=(t // tm,),
        in_specs=[row(d), row(d), row(d), row(1), row(1), pl.BlockSpec((1, d), lambda i: (0, 0))],
        out_specs=row(d),
        compiler_params=_params(("parallel",)),
        name="moe_combine",
    )(h, y0, y1, g0, g1, jnp.ones((1, d), F32))


def moe_combine_final(h, y0, y1, g0, g1, g_final, *, bsz, length):
    d = h.shape[1]
    seq = length - N_META
    r = COMBINE_ROWS
    nr = seq // r
    row = lambda w: pl.BlockSpec((r, w), lambda b, i: (b * nr + i, 0))
    h_spec = pl.BlockSpec((pl.Element(1), pl.Element(r), pl.Element(d)),
                          lambda b, i: (b, pl.multiple_of(N_META + i * r, N_META), 0))
    return pl.pallas_call(
        functools.partial(_moe_combine_kernel, final=True),
        out_shape=jax.ShapeDtypeStruct((bsz, seq, d), F32),
        grid=(bsz, nr),
        in_specs=[h_spec, row(d), row(d), row(1), row(1), pl.BlockSpec((1, d), lambda b, i: (0, 0))],
        out_specs=pl.BlockSpec((None, r, d), lambda b, i: (b, i, 0)),
        compiler_params=_params(("parallel", "parallel")),
        name="moe_combine_final",
    )(h.reshape(bsz, length, d), y0, y1, g0, g1, g_final.astype(F32).reshape(1, d))


SCAN_CHUNK = 256
MOE_CALL_SPLIT = (4, 16, None)


def _expert_ranks(expert):
    n = expert.shape[0]
    onehot = (expert[:, None] == jnp.arange(N_EXPERTS, dtype=jnp.int32)[None, :]).astype(F32)
    chunks = onehot.reshape(n // SCAN_CHUNK, SCAN_CHUNK, N_EXPERTS)
    tri = jnp.tril(jnp.ones((SCAN_CHUNK, SCAN_CHUNK), F32), k=-1)
    within = jnp.einsum('ij,cjk->cik', tri, chunks)
    totals = jnp.sum(chunks, axis=1)
    before = jnp.cumsum(totals, axis=0) - totals
    rank = jnp.sum((within + before[:, None, :]) * chunks, axis=-1).reshape(n)
    return rank.astype(jnp.int32), jnp.sum(totals, axis=0).astype(jnp.int32)


def moe_swiglu(h, g_norm, w_router, w1, w3, w2, g_final, *, bsz, length):
    n_tok, d = h.shape
    n_assign = n_tok * TOP_K
    router_pad = 128
    wr = jnp.pad(w_router.astype(F32), ((0, 0), (0, router_pad - N_EXPERTS)))
    logits, hn = norm_router(h, g_norm, wr)
    top_val, top_idx = lax.top_k(logits[:, :N_EXPERTS], TOP_K)
    gate = jax.nn.softmax(top_val, axis=-1)
    expert = top_idx.astype(jnp.int32).reshape(n_assign)
    rank, counts = _expert_ranks(expert)
    padded = (counts + MOE_SUPER - 1) // MOE_SUPER * MOE_SUPER
    pad_end = jnp.cumsum(padded)
    pad_start = pad_end - padded
    slot = (pad_start[expert] + rank).reshape(n_tok, TOP_K)
    n_super = -(-n_assign // MOE_SUPER) + N_EXPERTS
    n_slots = n_super * MOE_SUPER
    token = jnp.repeat(jnp.arange(n_tok, dtype=jnp.int32), TOP_K)
    slot_token = jnp.zeros((n_slots,), jnp.int32).at[slot.reshape(n_assign)].set(token)
    n_used = pad_end[-1] // MOE_SUPER
    sb = jnp.arange(n_super, dtype=jnp.int32)
    block_expert = jnp.minimum(jnp.searchsorted(pad_end, jnp.minimum(sb, n_used - 1) * MOE_SUPER, side='right'),
                               N_EXPERTS - 1).astype(jnp.int32)
    rows_here = jnp.clip(counts[block_expert] - (sb * MOE_SUPER - pad_start[block_expert]), 0, MOE_SUPER)
    n_halves = jnp.where(sb < n_used, (rows_here + MOE_BLOCK - 1) // MOE_BLOCK, 0).astype(jnp.int32)
    yb = jnp.zeros((n_slots, d), BF16)
    sb0 = 0
    for part in MOE_CALL_SPLIT:
        sb1 = n_super if part is None else sb0 + part
        rows = slice(sb0 * MOE_SUPER, sb1 * MOE_SUPER)
        yb = moe_experts(hn[slot_token[rows]], sb0, block_expert[sb0:sb1], n_halves[sb0:sb1], w1, w3, w2, yb)
        sb0 = sb1
    if g_final is None:
        return moe_combine(h, yb[slot[:, 0]], yb[slot[:, 1]], gate[:, 0:1], gate[:, 1:2])
    keep = lambda a: a.reshape(bsz, length, -1)[:, N_META:].reshape(bsz * (length - N_META), -1)
    slot, gate = keep(slot), keep(gate)
    return moe_combine_final(h, yb[slot[:, 0]], yb[slot[:, 1]], gate[:, 0:1], gate[:, 1:2], g_final,
                             bsz=bsz, length=length)


def kernel(x, meta_tokens, norm_mix, w_in, na_rpb, na_meta_bias, sc_conv, hy_conv, hy_w1, hy_b1, hy_w2, hy_b2,
           hy_w3, hy_freq, hy_decay, hy_skip, w_na_out, w_sc_out, w_hy_out, w_o, norm_ffn, ffn_w1, ffn_w3,
           ffn_w2, router_w, moe_w1, moe_w3, moe_w2, norm_final):
    bsz, seq, d = x.shape
    length = N_META + seq
    depth = w_in.shape[0]
    dft = _dft_matrices(length)
    kw = dict(bsz=bsz, length=length)
    w_in, w_na_out, w_sc_out, w_hy_out, w_o, ffn_w1, ffn_w3, ffn_w2 = map(
        to_bf16, (w_in, w_na_out, w_sc_out, w_hy_out, w_o, ffn_w1, ffn_w3, ffn_w2))
    for layer in range(depth):
        if layer == 0:
            proj, h = embed_norm_matmul(x, meta_tokens, norm_mix[0], w_in[0], tn=2048)
        else:
            proj = norm_matmul(h, norm_mix[layer], w_in[layer], tn=2048)
        y_na = neighbourhood_attention(proj, na_rpb[layer], na_meta_bias[layer], **kw)
        y_sc = short_conv_mixer(proj, sc_conv[layer], **kw)
        y_hy = hyena_mixer(proj, hy_conv[layer], hy_w1[layer], hy_b1[layer], hy_w2[layer], hy_b2[layer],
                           hy_w3[layer], hy_freq[layer], hy_decay[layer], hy_skip[layer], dft, **kw)
        h = mixer_out(h, y_na, y_sc, y_hy, proj, w_na_out[layer], w_sc_out[layer], w_hy_out[layer], w_o[layer])
        j = layer // 2
        if layer % 2 == 0:
            h = dense_ffn(h, norm_ffn[layer], ffn_w1[j], ffn_w3[j], ffn_w2[j])
        else:
            g_final = norm_final if layer == depth - 1 else None
            h = moe_swiglu(h, norm_ffn[layer], router_w[j], moe_w1[j], moe_w3[j], moe_w2[j], g_final, **kw)
    if depth % 2 == 0:
        return h
    out = rms_norm_rows(h, norm_final, out_dtype=x.dtype)
    return out.reshape(bsz, length, d)[:, N_META:]
```

```python
import functools
import math

import numpy as np
import jax
import jax.numpy as jnp
from jax import lax
from jax.experimental import pallas as pl
from jax.experimental.pallas import tpu as pltpu

D_MODEL = 2048
N_META = 16
GRID_W = 64
GRID_ROWS = 32
NA_HEADS = 16
NA_HEAD_DIM = 64
NA_WIDTH = NA_HEADS * NA_HEAD_DIM
WIN_ROWS = 8
WIN_COLS = 16
SC_WIDTH = 512
HY_WIDTH = 512
HY_ORDER = 2
HY_EMB = 33
HY_HIDDEN = 64
HY_MOD_SHIFT = 0.05
PROJ_WIDTH = 3 * NA_WIDTH + 3 * SC_WIDTH + 3 * HY_WIDTH + 3 * D_MODEL
D_FF = 5632
N_EXPERTS = 8
TOP_K = 2
D_FF_EXPERT = 7168
MOE_BLOCK = 512
NORM_EPS = 1e-6
NEG_INF = -1e30

COL_Q, COL_K, COL_V = 0, NA_WIDTH, 2 * NA_WIDTH
COL_SCB = 3 * NA_WIDTH
COL_SCC = COL_SCB + SC_WIDTH
COL_SCU = COL_SCC + SC_WIDTH
COL_HYU = COL_SCU + SC_WIDTH
COL_GNA = COL_HYU + 3 * HY_WIDTH
COL_GSC = COL_GNA + D_MODEL
COL_GHY = COL_GSC + D_MODEL

V7X_VMEM_BYTES = 64 * 1024 * 1024
VMEM_LIMIT = V7X_VMEM_BYTES - 8 * 1024 * 1024

ROW_TILE = 768
NORM_CHUNK = 16
FFT_N = 2 * (N_META + GRID_ROWS * GRID_W)
FFT_KP = 2080
FFT_KB = 1040
INV_NB = 688

F32 = jnp.float32
BF16 = jnp.bfloat16


def _params(sem, vmem_limit=VMEM_LIMIT):
    return pltpu.CompilerParams(dimension_semantics=sem, vmem_limit_bytes=vmem_limit)


def _sigmoid(a):
    return 0.5 * jnp.tanh(0.5 * a) + 0.5


CAST_BLOCK_BYTES = 8 * 1024 * 1024


def _cast_kernel(x_ref, o_ref):
    o_ref[...] = x_ref[...].astype(o_ref.dtype)


def to_bf16(w):
    cols = w.shape[-1]
    w2 = w.reshape(-1, cols)
    rows = w2.shape[0]
    tr = min(rows, CAST_BLOCK_BYTES // (4 * cols)) // 16 * 16
    while rows % tr:
        tr -= 16
    out = pl.pallas_call(
        _cast_kernel,
        out_shape=jax.ShapeDtypeStruct((rows, cols), BF16),
        grid=(rows // tr,),
        in_specs=[pl.BlockSpec((tr, cols), lambda i: (i, 0))],
        out_specs=pl.BlockSpec((tr, cols), lambda i: (i, 0)),
        compiler_params=_params(("parallel",)),
        name="to_bf16",
    )(w2)
    return out.reshape(w.shape)


def _rms_norm_rows(x_ref, g_ref, xn_ref):
    rows = x_ref.shape[0]
    g = g_ref[...]

    def chunk(c, carry):
        r0 = pl.multiple_of(c * NORM_CHUNK, NORM_CHUNK)
        x = x_ref[pl.ds(r0, NORM_CHUNK), :]
        ms = jnp.mean(x * x, axis=-1, keepdims=True)
        xn_ref[pl.ds(r0, NORM_CHUNK), :] = (x * lax.rsqrt(ms + NORM_EPS) * g).astype(xn_ref.dtype)
        return carry

    lax.fori_loop(0, rows // NORM_CHUNK, chunk, 0, unroll=4)


def _norm_mm_kernel(x_ref, g_ref, w_ref, o_ref, xn_ref):
    @pl.when(pl.program_id(1) == 0)
    def _():
        _rms_norm_rows(x_ref, g_ref, xn_ref)

    o_ref[...] = jnp.dot(xn_ref[...], w_ref[...], preferred_element_type=F32).astype(o_ref.dtype)


def norm_matmul(x, g, w, layer, *, tn, out_dtype=BF16, tm=ROW_TILE):
    t, k = x.shape
    n = w.shape[2]
    return pl.pallas_call(
        _norm_mm_kernel,
        out_shape=jax.ShapeDtypeStruct((t, n), out_dtype),
        grid=(t // tm, n // tn),
        in_specs=[
            pl.BlockSpec((tm, k), lambda i, j: (i, 0)),
            pl.BlockSpec((1, k), lambda i, j: (0, 0)),
            pl.BlockSpec((None, k, tn), lambda i, j: (layer, 0, j)),
        ],
        out_specs=pl.BlockSpec((tm, tn), lambda i, j: (i, j)),
        scratch_shapes=[pltpu.VMEM((tm, k), BF16)],
        compiler_params=_params(("parallel", "arbitrary")),
        name="norm_matmul",
    )(x, g.reshape(1, k), w)


EMBED_TILES = 3


def _embed_norm_mm_kernel(x_ref, meta_ref, g_ref, w_ref, o_ref, h_ref, xn_ref):
    tm = h_ref.shape[0]
    first = pl.program_id(0) % EMBED_TILES == 0

    @pl.when(jnp.logical_and(pl.program_id(1) == 0, first))
    def _():
        h_ref[0:N_META, :] = meta_ref[...]
        h_ref[N_META:tm, :] = x_ref[0, 0:tm - N_META, :]

    @pl.when(jnp.logical_and(pl.program_id(1) == 0, jnp.logical_not(first)))
    def _():
        h_ref[...] = x_ref[0]

    @pl.when(pl.program_id(1) == 0)
    def _():
        _rms_norm_rows(h_ref, g_ref, xn_ref)

    o_ref[...] = jnp.dot(xn_ref[...], w_ref[...], preferred_element_type=F32).astype(o_ref.dtype)


def embed_norm_matmul(x, meta_tokens, g, w, *, tn):
    bsz, seq, k = x.shape
    length = N_META + seq
    tm = length // EMBED_TILES
    n = w.shape[2]

    def x_rows(i, j):
        tile = i % EMBED_TILES
        start = jnp.maximum(tile * tm - N_META, 0)
        return i // EMBED_TILES, pl.multiple_of(start, N_META), 0

    return pl.pallas_call(
        _embed_norm_mm_kernel,
        out_shape=(jax.ShapeDtypeStruct((bsz * length, n), BF16), jax.ShapeDtypeStruct((bsz * length, k), F32)),
        grid=(bsz * EMBED_TILES, n // tn),
        in_specs=[
            pl.BlockSpec((pl.Element(1), pl.Element(tm), pl.Element(k)), x_rows),
            pl.BlockSpec((N_META, k), lambda i, j: (0, 0)),
            pl.BlockSpec((1, k), lambda i, j: (0, 0)),
            pl.BlockSpec((None, k, tn), lambda i, j: (0, 0, j)),
        ],
        out_specs=(pl.BlockSpec((tm, tn), lambda i, j: (i, j)), pl.BlockSpec((tm, k), lambda i, j: (i, 0))),
        scratch_shapes=[pltpu.VMEM((tm, k), BF16)],
        compiler_params=_params(("parallel", "arbitrary")),
        name="embed_norm_matmul",
    )(x, meta_tokens.astype(F32), g.reshape(1, k), w)


def _norm_router_kernel(x_ref, g_ref, w_ref, o_ref, xb_ref, xn_ref):
    _rms_norm_rows(x_ref, g_ref, xn_ref)
    xn = xn_ref[...]
    x_hi = xn.astype(BF16)
    x_lo = (xn - x_hi.astype(F32)).astype(BF16)
    w = w_ref[...]
    w_hi = w.astype(BF16)
    w_lo = (w - w_hi.astype(F32)).astype(BF16)
    xb_ref[...] = x_hi
    logits = jnp.dot(x_hi, w_hi, preferred_element_type=F32)
    logits += jnp.dot(x_lo, w_hi, preferred_element_type=F32)
    logits += jnp.dot(x_hi, w_lo, preferred_element_type=F32)
    o_ref[...] = logits


def norm_router(x, g, w, *, tm=ROW_TILE):
    t, k = x.shape
    n = w.shape[1]
    return pl.pallas_call(
        _norm_router_kernel,
        out_shape=(jax.ShapeDtypeStruct((t, n), F32), jax.ShapeDtypeStruct((t, k), BF16)),
        grid=(t // tm,),
        in_specs=[
            pl.BlockSpec((tm, k), lambda i: (i, 0)),
            pl.BlockSpec((1, k), lambda i: (0, 0)),
            pl.BlockSpec((k, n), lambda i: (0, 0)),
        ],
        out_specs=(pl.BlockSpec((tm, n), lambda i: (i, 0)), pl.BlockSpec((tm, k), lambda i: (i, 0))),
        scratch_shapes=[pltpu.VMEM((tm, k), F32)],
        compiler_params=_params(("parallel",)),
        name="norm_router",
    )(x, g.reshape(1, k), w)


def _norm_kernel(x_ref, g_ref, o_ref):
    _rms_norm_rows(x_ref, g_ref, o_ref)


def rms_norm_rows(x, g, *, out_dtype, tm=ROW_TILE):
    t, k = x.shape
    return pl.pallas_call(
        _norm_kernel,
        out_shape=jax.ShapeDtypeStruct((t, k), out_dtype),
        grid=(t // tm,),
        in_specs=[pl.BlockSpec((tm, k), lambda i: (i, 0)), pl.BlockSpec((1, k), lambda i: (0, 0))],
        out_specs=pl.BlockSpec((tm, k), lambda i: (i, 0)),
        compiler_params=_params(("parallel",)),
        name="rms_norm",
    )(x, g.reshape(1, k))


def _dense_ffn_kernel(x_ref, g_ref, w1_ref, w3_ref, w2_ref, o_ref, xn_ref):
    @pl.when(pl.program_id(1) == 0)
    def _():
        _rms_norm_rows(x_ref, g_ref, xn_ref)
        o_ref[...] = x_ref[...]

    xn = xn_ref[...]
    a = jnp.dot(xn, w1_ref[...], preferred_element_type=F32)
    b = jnp.dot(xn, w3_ref[...], preferred_element_type=F32)
    hid = (a * _sigmoid(a) * b).astype(BF16)
    o_ref[...] += jnp.dot(hid, w2_ref[...], preferred_element_type=F32)


def dense_ffn(x, g, w1, w3, w2, *, fc=512, tm=ROW_TILE):
    t, d = x.shape
    f = w1.shape[1]
    return pl.pallas_call(
        _dense_ffn_kernel,
        out_shape=jax.ShapeDtypeStruct((t, d), F32),
        grid=(t // tm, f // fc),
        in_specs=[
            pl.BlockSpec((tm, d), lambda i, j: (i, 0)),
            pl.BlockSpec((1, d), lambda i, j: (0, 0)),
            pl.BlockSpec((d, fc), lambda i, j: (0, j)),
            pl.BlockSpec((d, fc), lambda i, j: (0, j)),
            pl.BlockSpec((fc, d), lambda i, j: (j, 0)),
        ],
        out_specs=pl.BlockSpec((tm, d), lambda i, j: (i, 0)),
        scratch_shapes=[pltpu.VMEM((tm, d), BF16)],
        compiler_params=_params(("parallel", "arbitrary")),
        name="dense_ffn",
    )(x, g.reshape(1, d), w1, w3, w2)


def _mixer_out_kernel(res_ref, yna_ref, ysc_ref, yhy_ref, gna_ref, gsc_ref, ghy_ref, wna_ref, wsc_ref, why_ref,
                      wo_ref, o_ref):
    @pl.when(pl.program_id(1) == 0)
    def _():
        o_ref[...] = res_ref[...]

    m = _sigmoid(gna_ref[...].astype(F32)) * jnp.dot(yna_ref[...], wna_ref[...], preferred_element_type=F32)
    m += _sigmoid(gsc_ref[...].astype(F32)) * jnp.dot(ysc_ref[...], wsc_ref[...], preferred_element_type=F32)
    m += _sigmoid(ghy_ref[...].astype(F32)) * jnp.dot(yhy_ref[...], why_ref[...], preferred_element_type=F32)
    o_ref[...] += jnp.dot(m.astype(BF16), wo_ref[...], preferred_element_type=F32)


MIXER_ROWS = 256


def mixer_out(res, y_na, y_sc, y_hy, proj, w_na, w_sc, w_hy, w_o, *, cw=D_MODEL, tm=MIXER_ROWS):
    t, d = res.shape
    return pl.pallas_call(
        _mixer_out_kernel,
        out_shape=jax.ShapeDtypeStruct((t, d), F32),
        grid=(t // tm, d // cw),
        in_specs=[
            pl.BlockSpec((tm, d), lambda i, c: (i, 0)),
            pl.BlockSpec((tm, NA_WIDTH), lambda i, c: (i, 0)),
            pl.BlockSpec((tm, SC_WIDTH), lambda i, c: (i, 0)),
            pl.BlockSpec((tm, HY_WIDTH), lambda i, c: (i, 0)),
            pl.BlockSpec((tm, cw), lambda i, c: (i, COL_GNA // cw + c)),
            pl.BlockSpec((tm, cw), lambda i, c: (i, COL_GSC // cw + c)),
            pl.BlockSpec((tm, cw), lambda i, c: (i, COL_GHY // cw + c)),
            pl.BlockSpec((NA_WIDTH, cw), lambda i, c: (0, c)),
            pl.BlockSpec((SC_WIDTH, cw), lambda i, c: (0, c)),
            pl.BlockSpec((HY_WIDTH, cw), lambda i, c: (0, c)),
            pl.BlockSpec((cw, d), lambda i, c: (c, 0)),
        ],
        out_specs=pl.BlockSpec((tm, d), lambda i, c: (i, 0)),
        compiler_params=_params(("parallel", "arbitrary")),
        name="mixer_out",
    )(res, y_na, y_sc, y_hy, proj, proj, proj, w_na, w_sc, w_hy, w_o)


def _na_bias_table(rpb):
    qc = np.arange(GRID_W)[:, None]
    kc = np.arange(GRID_W)[None, :]
    win_start = np.clip(qc - WIN_COLS // 2, 0, GRID_W - WIN_COLS)
    inside = (kc >= win_start) & (kc < win_start + WIN_COLS)
    dc = np.clip(kc - qc, -(WIN_COLS - 1), WIN_COLS - 1) + WIN_COLS - 1
    dr = np.arange(WIN_ROWS)[None, :] - np.arange(WIN_ROWS)[:, None] + WIN_ROWS - 1
    onehot = np.zeros((2 * WIN_COLS - 1, GRID_W * GRID_W), np.float32)
    onehot[dc.reshape(-1), np.arange(GRID_W * GRID_W)] = 1.0
    tbl = jnp.einsum('htid,dj->htij', rpb.astype(F32)[:, dr], jnp.asarray(onehot),
                     precision=lax.Precision.HIGHEST)
    tbl = tbl.reshape(NA_HEADS, WIN_ROWS, WIN_ROWS, GRID_W, GRID_W)
    tbl = jnp.where(jnp.asarray(inside)[None, None, None], tbl, NEG_INF)
    tbl = tbl.transpose(0, 1, 3, 2, 4)
    return tbl.reshape(NA_HEADS, WIN_ROWS, GRID_W, WIN_ROWS * GRID_W)


NA_SOFTMAX_LAG = 1
NA_PV_LAG = 2


def _na_kernel(q_ref, k_ref, v_ref, tbl_ref, mb_ref, o_ref):
    scale = NA_HEAD_DIM ** -0.5
    lane = lax.broadcasted_iota(jnp.int32, (1, 2 * NA_HEAD_DIM), 1)
    first = lane < NA_HEAD_DIM
    nt = (((1,), (1,)), ((), ()))
    k_meta = k_ref[0:N_META, :]
    v_meta = v_ref[0:N_META, :]
    mb_grid = mb_ref[0, 0]
    band = WIN_ROWS * GRID_W

    def stack_heads(q):
        q = q * scale
        zero = jnp.zeros_like(q)
        return jnp.concatenate([jnp.where(first, q, zero), jnp.where(first, zero, q)], axis=0)

    def unstack_heads(out, den, n):
        out = out / den
        return jnp.where(first, out[:n], out[n:])

    def softmax_parts(s_grid, s_meta):
        m = jnp.max(s_meta, axis=-1, keepdims=True)
        if s_grid is not None:
            m = jnp.maximum(m, jnp.max(s_grid, axis=-1, keepdims=True))
        p_meta = jnp.exp(s_meta - m)
        den = jnp.sum(p_meta, axis=-1, keepdims=True)
        p_grid = None
        if s_grid is not None:
            p_grid = jnp.exp(s_grid - m)
            den = den + jnp.sum(p_grid, axis=-1, keepdims=True)
            p_grid = p_grid.astype(BF16)
        return p_grid, p_meta.astype(BF16), den

    def geometry(r):
        rs = min(max(r - WIN_ROWS // 2, 0), GRID_ROWS - WIN_ROWS)
        return N_META + r * GRID_W, N_META + rs * GRID_W, r - rs

    def scores(r):
        if r is None:
            q2 = stack_heads(q_ref[0:N_META, :])
            return None, lax.dot_general(q2, k_meta, nt, preferred_element_type=F32) + mb_ref[0, 1, 0:2 * N_META, :]
        q0, k0, typ = geometry(r)
        q2 = stack_heads(q_ref[q0:q0 + GRID_W, :])
        s_grid = lax.dot_general(q2, k_ref[k0:k0 + band, :], nt, preferred_element_type=F32) + tbl_ref[0, typ]
        s_meta = lax.dot_general(q2, k_meta, nt, preferred_element_type=F32) + mb_grid
        return s_grid, s_meta

    def weighted_values(r, p_grid, p_meta, den):
        out = jnp.dot(p_meta, v_meta, preferred_element_type=F32)
        if r is None:
            o_ref[0:N_META, :] = unstack_heads(out, den, N_META).astype(o_ref.dtype)
            return
        q0, k0, _ = geometry(r)
        out += jnp.dot(p_grid, v_ref[k0:k0 + band, :], preferred_element_type=F32)
        o_ref[q0:q0 + GRID_W, :] = unstack_heads(out, den, GRID_W).astype(o_ref.dtype)

    units = [None] + list(range(GRID_ROWS))
    pending_scores, pending_probs = {}, {}
    for t in range(len(units) + NA_PV_LAG):
        if t < len(units):
            pending_scores[t] = scores(units[t])
        if 0 <= t - NA_SOFTMAX_LAG < len(units):
            pending_probs[t - NA_SOFTMAX_LAG] = softmax_parts(*pending_scores.pop(t - NA_SOFTMAX_LAG))
        if 0 <= t - NA_PV_LAG < len(units):
            weighted_values(units[t - NA_PV_LAG], *pending_probs.pop(t - NA_PV_LAG))


def neighbourhood_attention(proj, rpb, meta_bias, *, bsz, length):
    hp = 2 * NA_HEAD_DIM
    n_pairs = NA_HEADS // 2
    band = WIN_ROWS * GRID_W
    tbl = _na_bias_table(rpb).reshape(n_pairs, 2, WIN_ROWS, GRID_W, band).transpose(0, 2, 1, 3, 4)
    tbl = tbl.reshape(n_pairs, WIN_ROWS, 2 * GRID_W, band)
    mb = meta_bias.astype(F32).reshape(n_pairs, 2, 1, N_META)
    mb_grid = jnp.broadcast_to(mb, (n_pairs, 2, GRID_W, N_META)).reshape(n_pairs, 2 * GRID_W, N_META)
    mb_meta = jnp.broadcast_to(mb, (n_pairs, 2, N_META, N_META)).reshape(n_pairs, 2 * N_META, N_META)
    mb_meta = jnp.pad(mb_meta, ((0, 0), (0, 2 * GRID_W - 2 * N_META), (0, 0)))
    mb2 = jnp.stack([mb_grid, mb_meta], axis=1)
    return pl.pallas_call(
        _na_kernel,
        out_shape=jax.ShapeDtypeStruct((bsz * length, NA_WIDTH), BF16),
        grid=(n_pairs, bsz),
        in_specs=[
            pl.BlockSpec((length, hp), lambda h, b: (b, COL_Q // hp + h)),
            pl.BlockSpec((length, hp), lambda h, b: (b, COL_K // hp + h)),
            pl.BlockSpec((length, hp), lambda h, b: (b, COL_V // hp + h)),
            pl.BlockSpec((1, WIN_ROWS, 2 * GRID_W, band), lambda h, b: (h, 0, 0, 0)),
            pl.BlockSpec((1, 2, 2 * GRID_W, N_META), lambda h, b: (h, 0, 0, 0)),
        ],
        out_specs=pl.BlockSpec((length, hp), lambda h, b: (b, h)),
        compiler_params=_params(("parallel", "parallel")),
        name="neighbourhood_attention",
    )(proj, proj, proj, tbl, mb2)


def _dwconv3(u, w):
    length = u.shape[0]
    row = lax.broadcasted_iota(jnp.int32, u.shape, 0)
    prev = jnp.where(row == 0, 0.0, pltpu.roll(u, 1, 0))
    nxt = jnp.where(row == length - 1, 0.0, pltpu.roll(u, length - 1, 0))
    return prev * w[0:1] + u * w[1:2] + nxt * w[2:3]


def _gated_conv_kernel(b_ref, c_ref, u_ref, w_ref, o_ref):
    u = c_ref[...].astype(F32) * u_ref[...].astype(F32)
    o_ref[...] = (b_ref[...].astype(F32) * _dwconv3(u, w_ref[...])).astype(o_ref.dtype)


def _plain_conv_kernel(u_ref, w_ref, o_ref):
    o_ref[...] = _dwconv3(u_ref[...].astype(F32), w_ref[...]).astype(o_ref.dtype)


CONV_CB = 512


def short_conv_mixer(proj, w_conv, *, bsz, length):
    cb = CONV_CB
    return pl.pallas_call(
        _gated_conv_kernel,
        out_shape=jax.ShapeDtypeStruct((bsz * length, SC_WIDTH), BF16),
        grid=(bsz, SC_WIDTH // cb),
        in_specs=[
            pl.BlockSpec((length, cb), lambda b, c: (b, COL_SCB // cb + c)),
            pl.BlockSpec((length, cb), lambda b, c: (b, COL_SCC // cb + c)),
            pl.BlockSpec((length, cb), lambda b, c: (b, COL_SCU // cb + c)),
            pl.BlockSpec((3, cb), lambda b, c: (0, c)),
        ],
        out_specs=pl.BlockSpec((length, cb), lambda b, c: (b, c)),
        compiler_params=_params(("parallel", "parallel")),
        name="short_conv_mixer",
    )(proj, proj, proj, w_conv.astype(F32))


def hyena_in_conv(proj, w_conv, *, bsz, length):
    cb = CONV_CB
    return pl.pallas_call(
        _plain_conv_kernel,
        out_shape=jax.ShapeDtypeStruct((bsz * length, 3 * HY_WIDTH), BF16),
        grid=(bsz, 3 * HY_WIDTH // cb),
        in_specs=[
            pl.BlockSpec((length, cb), lambda b, c: (b, COL_HYU // cb + c)),
            pl.BlockSpec((3, cb), lambda b, c: (0, c)),
        ],
        out_specs=pl.BlockSpec((length, cb), lambda b, c: (b, c)),
        compiler_params=_params(("parallel", "parallel")),
        name="hyena_in_conv",
    )(proj, w_conv.astype(F32))


def _filter_kernel(z_ref, w1_ref, b1_ref, w2_ref, b2_ref, w3_ref, fr_ref, dec_ref, f_ref, s_ref):
    hi = lax.Precision.HIGHEST
    z = z_ref[...]
    fr = fr_ref[...]
    h = jnp.sin(fr * (jnp.dot(z, w1_ref[...], preferred_element_type=F32, precision=hi) + b1_ref[...]))
    h = jnp.sin(fr * (jnp.dot(h, w2_ref[...], preferred_element_type=F32, precision=hi) + b2_ref[...]))
    h = jnp.dot(h, w3_ref[...], preferred_element_type=F32, precision=hi)
    t = z[:, 0:1]
    f = h * (jnp.exp(-t * jnp.abs(dec_ref[0])) + HY_MOD_SHIFT)
    row = lax.broadcasted_iota(jnp.int32, f.shape, 0)
    backward = pl.program_id(0) % 2 == 1
    f = jnp.where(jnp.logical_and(row == 0, backward), 0.0, f)
    f_ref[0] = f.astype(f_ref.dtype)
    s_ref[0] = jnp.sum(jnp.abs(f), axis=0, keepdims=True)


def hyena_filters(length, w1, b1, w2, b2, w3, freq, decay):
    bands = (HY_EMB - 1) // 2
    t = jnp.linspace(0.0, 1.0, length, dtype=F32)[:, None]
    ang = (2.0 * math.pi / length) * jnp.arange(length, dtype=F32)[:, None]
    fb = jnp.linspace(1e-4, bands - 1, bands, dtype=F32)[None, :]
    z = jnp.concatenate([t, jnp.cos(fb * ang), -jnp.sin(fb * ang)], axis=-1)
    emb = 128
    z = jnp.pad(z, ((0, 0), (0, emb - HY_EMB)))
    w1p = jnp.pad(w1.astype(F32), ((0, emb - HY_EMB), (0, 0)))
    n_slab = HY_ORDER * 2
    row = lambda a: a.astype(F32).reshape(1, -1)
    return pl.pallas_call(
        _filter_kernel,
        out_shape=(jax.ShapeDtypeStruct((n_slab, length, HY_WIDTH), BF16),
                   jax.ShapeDtypeStruct((n_slab, 1, HY_WIDTH), F32)),
        grid=(n_slab,),
        in_specs=[
            pl.BlockSpec((length, emb), lambda j: (0, 0)),
            pl.BlockSpec((emb, HY_HIDDEN), lambda j: (0, 0)),
            pl.BlockSpec((1, HY_HIDDEN), lambda j: (0, 0)),
            pl.BlockSpec((HY_HIDDEN, HY_HIDDEN), lambda j: (0, 0)),
            pl.BlockSpec((1, HY_HIDDEN), lambda j: (0, 0)),
            pl.BlockSpec((HY_HIDDEN, HY_WIDTH), lambda j: (0, j)),
            pl.BlockSpec((1, HY_HIDDEN), lambda j: (0, 0)),
            pl.BlockSpec((1, 1, HY_WIDTH), lambda j: (j, 0, 0)),
        ],
        out_specs=(pl.BlockSpec((1, length, HY_WIDTH), lambda j: (j, 0, 0)),
                   pl.BlockSpec((1, 1, HY_WIDTH), lambda j: (j, 0, 0))),
        compiler_params=_params(("parallel",)),
        name="hyena_filters",
    )(z, w1p, row(b1), w2.astype(F32), row(b2), w3.astype(F32), row(freq),
      decay.astype(F32).reshape(n_slab, 1, HY_WIDTH))


def _dft_matrices(length):
    k = jnp.arange(FFT_KP, dtype=jnp.int32)[:, None]
    n = jnp.arange(length, dtype=jnp.int32)[None, :]
    theta = (2.0 * math.pi / FFT_N) * ((k * n) % FFT_N).astype(F32)
    valid = k <= FFT_N // 2
    fc = jnp.where(valid, jnp.cos(theta), 0.0).astype(BF16)
    fs = jnp.where(valid, -jnp.sin(theta), 0.0).astype(BF16)
    fc, fs = lax.optimization_barrier((fc, fs))
    return fc, fs, fc.T, fs.T


def _dft_raw_kernel(fc_ref, fs_ref, u_ref, re_ref, im_ref):
    u = u_ref[0]
    re_ref[0] = jnp.dot(fc_ref[...], u, preferred_element_type=F32)
    im_ref[0] = jnp.dot(fs_ref[...], u, preferred_element_type=F32)


def dft_filters(fc, fs, filt):
    n_slab, length, c = filt.shape
    out = jax.ShapeDtypeStruct((n_slab, FFT_KP, c), F32)
    return pl.pallas_call(
        _dft_raw_kernel,
        out_shape=(out, out),
        grid=(FFT_KP // FFT_KB, n_slab),
        in_specs=[
            pl.BlockSpec((FFT_KB, length), lambda kb, s: (kb, 0)),
            pl.BlockSpec((FFT_KB, length), lambda kb, s: (kb, 0)),
            pl.BlockSpec((1, length, c), lambda kb, s: (s, 0, 0)),
        ],
        out_specs=(pl.BlockSpec((1, FFT_KB, c), lambda kb, s: (s, kb, 0)),
                   pl.BlockSpec((1, FFT_KB, c), lambda kb, s: (s, kb, 0))),
        compiler_params=_params(("parallel", "parallel")),
        name="dft_filters",
    )(fc, fs, filt)


def _dft_mul_kernel(fc_ref, fs_ref, u_ref, are_ref, aim_ref, s_ref, yre_ref, yim_ref, g_ref):
    @pl.when(pl.program_id(1) == 0)
    def _():
        k = pl.program_id(0) * FFT_KB + lax.broadcasted_iota(jnp.int32, (FFT_KB, 1), 0)
        wk = jnp.where(jnp.logical_or(k == 0, k == FFT_N // 2), 1.0, 2.0)
        wk = jnp.where(k > FFT_N // 2, 0.0, wk) / FFT_N
        inv = wk / (s_ref[0] + s_ref[1])
        g_ref[0] = (are_ref[0] + are_ref[1]) * inv
        g_ref[1] = (aim_ref[0] - aim_ref[1]) * inv

    u = u_ref[...]
    re = jnp.dot(fc_ref[...], u, preferred_element_type=F32)
    im = jnp.dot(fs_ref[...], u, preferred_element_type=F32)
    gre = g_ref[0]
    gim = g_ref[1]
    yre_ref[0] = (re * gre - im * gim).astype(yre_ref.dtype)
    yim_ref[0] = (re * gim + im * gre).astype(yim_ref.dtype)


def dft_forward_mul(fc, fs, u, col, a_re, a_im, abs_sum, order, *, bsz, length):
    c = HY_WIDTH
    out = jax.ShapeDtypeStruct((bsz, FFT_KP, c), BF16)
    return pl.pallas_call(
        _dft_mul_kernel,
        out_shape=(out, out),
        grid=(FFT_KP // FFT_KB, bsz),
        in_specs=[
            pl.BlockSpec((FFT_KB, length), lambda kb, b: (kb, 0)),
            pl.BlockSpec((FFT_KB, length), lambda kb, b: (kb, 0)),
            pl.BlockSpec((length, c), lambda kb, b: (b, col)),
            pl.BlockSpec((2, FFT_KB, c), lambda kb, b: (order, kb, 0)),
            pl.BlockSpec((2, FFT_KB, c), lambda kb, b: (order, kb, 0)),
            pl.BlockSpec((2, 1, c), lambda kb, b: (order, 0, 0)),
        ],
        out_specs=(pl.BlockSpec((1, FFT_KB, c), lambda kb, b: (b, kb, 0)),
                   pl.BlockSpec((1, FFT_KB, c), lambda kb, b: (b, kb, 0))),
        scratch_shapes=[pltpu.VMEM((2, FFT_KB, c), F32)],
        compiler_params=_params(("parallel", "arbitrary")),
        name="dft_forward_mul",
    )(fc, fs, u, a_re, a_im, abs_sum)


def _dft_inv_kernel(ct_ref, st_ref, yre_ref, yim_ref, u_ref, gate_ref, skip_ref, o_ref):
    y = jnp.dot(ct_ref[...], yre_ref[0], preferred_element_type=F32)
    y += jnp.dot(st_ref[...], yim_ref[0], preferred_element_type=F32)
    y += u_ref[...].astype(F32) * skip_ref[0]
    o_ref[...] = (gate_ref[...].astype(F32) * y).astype(o_ref.dtype)


def dft_inverse_gate(fct, fst, yre, yim, u, u_col, gate, gate_col, skip, order, *, bsz, length):
    c = HY_WIDTH
    nblk = length // INV_NB
    return pl.pallas_call(
        _dft_inv_kernel,
        out_shape=jax.ShapeDtypeStruct((bsz * length, c), BF16),
        grid=(nblk, bsz),
        in_specs=[
            pl.BlockSpec((INV_NB, FFT_KP), lambda nb, b: (nb, 0)),
            pl.BlockSpec((INV_NB, FFT_KP), lambda nb, b: (nb, 0)),
            pl.BlockSpec((1, FFT_KP, c), lambda nb, b: (b, 0, 0)),
            pl.BlockSpec((1, FFT_KP, c), lambda nb, b: (b, 0, 0)),
            pl.BlockSpec((INV_NB, c), lambda nb, b: (b * nblk + nb, u_col)),
            pl.BlockSpec((INV_NB, c), lambda nb, b: (b * nblk + nb, gate_col)),
            pl.BlockSpec((1, 1, c), lambda nb, b: (order, 0, 0)),
        ],
        out_specs=pl.BlockSpec((INV_NB, c), lambda nb, b: (b * nblk + nb, 0)),
        compiler_params=_params(("parallel", "parallel")),
        name="dft_inverse_gate",
    )(fct, fst, yre, yim, u, gate, skip)


def hyena_mixer(proj, w_conv, w1, b1, w2, b2, w3, freq, decay, skip, dft, *, bsz, length):
    fc, fs, fct, fst = dft
    vxx = hyena_in_conv(proj, w_conv, bsz=bsz, length=length)
    filt, abs_sum = hyena_filters(length, w1, b1, w2, b2, w3, freq, decay)
    a_re, a_im = dft_filters(fc, fs, filt)
    skip = skip.astype(F32).reshape(HY_ORDER, 1, HY_WIDTH)
    kw = dict(bsz=bsz, length=length)
    yre, yim = dft_forward_mul(fc, fs, vxx, 0, a_re, a_im, abs_sum, 0, **kw)
    z = dft_inverse_gate(fct, fst, yre, yim, vxx, 0, vxx, 1, skip, 0, **kw)
    yre, yim = dft_forward_mul(fc, fs, z, 0, a_re, a_im, abs_sum, 1, **kw)
    return dft_inverse_gate(fct, fst, yre, yim, z, 0, vxx, 2, skip, 1, **kw)


MOE_SUPER = 2 * MOE_BLOCK
MOE_FC = 512
MOE_VMEM_LIMIT = V7X_VMEM_BYTES - 3 * 1024 * 1024


def _moe_kernel(be_ref, nh_ref, x_ref, w1_ref, w3_ref, w2_ref, yb_ref, o_ref, acc_ref):
    del be_ref, yb_ref
    i = pl.program_id(0)
    j = pl.program_id(1)
    halves = nh_ref[i]

    @pl.when(j == 0)
    def _():
        acc_ref[...] = jnp.zeros_like(acc_ref)

    def ffn(rows):
        x = x_ref[0:rows, :]
        a = jnp.dot(x, w1_ref[0].astype(BF16), preferred_element_type=F32)
        b = jnp.dot(x, w3_ref[0].astype(BF16), preferred_element_type=F32)
        hid = (a * _sigmoid(a) * b).astype(BF16)
        acc_ref[0:rows, :] += jnp.dot(hid, w2_ref[0].astype(BF16), preferred_element_type=F32)

    @pl.when(halves == 2)
    def _():
        ffn(MOE_SUPER)

    @pl.when(halves == 1)
    def _():
        ffn(MOE_BLOCK)

    @pl.when(j == pl.num_programs(1) - 1)
    def _():
        o_ref[...] = acc_ref[...].astype(o_ref.dtype)


def moe_experts(xb, sb_start, block_expert, n_halves, w1, w3, w2, yb):
    d = xb.shape[1]
    n_super = xb.shape[0] // MOE_SUPER
    fc = MOE_FC
    n_fc = D_FF_EXPERT // fc

    def chunk(i, j, nh):
        return jnp.where(nh[i] > 0, j, n_fc - 1)

    grid_spec = pltpu.PrefetchScalarGridSpec(
        num_scalar_prefetch=2,
        grid=(n_super, n_fc),
        in_specs=[
            pl.BlockSpec((MOE_SUPER, d), lambda i, j, be, nh: (i, 0)),
            pl.BlockSpec((1, d, fc), lambda i, j, be, nh: (be[i], 0, chunk(i, j, nh))),
            pl.BlockSpec((1, d, fc), lambda i, j, be, nh: (be[i], 0, chunk(i, j, nh))),
            pl.BlockSpec((1, fc, d), lambda i, j, be, nh: (be[i], chunk(i, j, nh), 0)),
            pl.BlockSpec(memory_space=pl.ANY),
        ],
        out_specs=pl.BlockSpec((MOE_SUPER, d), lambda i, j, be, nh: (sb_start + i, 0)),
        scratch_shapes=[pltpu.VMEM((MOE_SUPER, d), F32)],
    )
    return pl.pallas_call(
        _moe_kernel,
        out_shape=jax.ShapeDtypeStruct(yb.shape, yb.dtype),
        grid_spec=grid_spec,
        input_output_aliases={6: 0},
        compiler_params=_params(("parallel", "arbitrary"), MOE_VMEM_LIMIT),
        name="moe_experts",
    )(block_expert, n_halves, xb, w1, w3, w2, yb)


COMBINE_ROWS = 256


def _moe_combine_kernel(h_ref, y0_ref, y1_ref, g0_ref, g1_ref, gn_ref, o_ref, *, final):
    rows = o_ref.shape[0]
    gn = gn_ref[...]

    def chunk(c, carry):
        r0 = pl.multiple_of(c * NORM_CHUNK, NORM_CHUNK)
        sl = pl.ds(r0, NORM_CHUNK)
        h = h_ref[0, sl, :] if final else h_ref[sl, :]
        y = h + g0_ref[sl, :] * y0_ref[sl, :].astype(F32) + g1_ref[sl, :] * y1_ref[sl, :].astype(F32)
        if final:
            ms = jnp.mean(y * y, axis=-1, keepdims=True)
            y = y * lax.rsqrt(ms + NORM_EPS) * gn
        o_ref[sl, :] = y.astype(o_ref.dtype)
        return carry

    lax.fori_loop(0, rows // NORM_CHUNK, chunk, 0, unroll=4)


def moe_combine(h, y0, y1, g0, g1, *, tm=ROW_TILE):
    t, d = h.shape
    row = lambda w: pl.BlockSpec((tm, w), lambda i: (i, 0))
    return pl.pallas_call(
        functools.partial(_moe_combine_kernel, final=False),
        out_shape=jax.ShapeDtypeStruct((t, d), F32),
        grid=(t // tm,),
        in_specs=[row(d), row(d), row(d), row(1), row(1), pl.BlockSpec((1, d), lambda i: (0, 0))],
        out_specs=row(d),
        compiler_params=_params(("parallel",)),
        name="moe_combine",
    )(h, y0, y1, g0, g1, jnp.ones((1, d), F32))


def moe_combine_final(h, y0, y1, g0, g1, g_final, *, bsz, length):
    d = h.shape[1]
    seq = length - N_META
    r = COMBINE_ROWS
    nr = seq // r
    row = lambda w: pl.BlockSpec((r, w), lambda b, i: (b * nr + i, 0))
    h_spec = pl.BlockSpec((pl.Element(1), pl.Element(r), pl.Element(d)),
                          lambda b, i: (b, pl.multiple_of(N_META + i * r, N_META), 0))
    return pl.pallas_call(
        functools.partial(_moe_combine_kernel, final=True),
        out_shape=jax.ShapeDtypeStruct((bsz, seq, d), F32),
        grid=(bsz, nr),
        in_specs=[h_spec, row(d), row(d), row(1), row(1), pl.BlockSpec((1, d), lambda b, i: (0, 0))],
        out_specs=pl.BlockSpec((None, r, d), lambda b, i: (b, i, 0)),
        compiler_params=_params(("parallel", "parallel")),
        name="moe_combine_final",
    )(h.reshape(bsz, length, d), y0, y1, g0, g1, g_final.astype(F32).reshape(1, d))


SCAN_CHUNK = 256
MOE_CALL_SPLIT = (4, 16, None)


def _expert_ranks(expert):
    n = expert.shape[0]
    onehot = (expert[:, None] == jnp.arange(N_EXPERTS, dtype=jnp.int32)[None, :]).astype(F32)
    chunks = onehot.reshape(n // SCAN_CHUNK, SCAN_CHUNK, N_EXPERTS)
    tri = jnp.tril(jnp.ones((SCAN_CHUNK, SCAN_CHUNK), F32), k=-1)
    within = jnp.einsum('ij,cjk->cik', tri, chunks)
    totals = jnp.sum(chunks, axis=1)
    before = jnp.cumsum(totals, axis=0) - totals
    rank = jnp.sum((within + before[:, None, :]) * chunks, axis=-1).reshape(n)
    return rank.astype(jnp.int32), jnp.sum(totals, axis=0).astype(jnp.int32)


def moe_swiglu(h, g_norm, w_router, w1, w3, w2, g_final, *, bsz, length):
    n_tok, d = h.shape
    n_assign = n_tok * TOP_K
    router_pad = 128
    wr = jnp.pad(w_router.astype(F32), ((0, 0), (0, router_pad - N_EXPERTS)))
    logits, hn = norm_router(h, g_norm, wr)
    top_val, top_idx = lax.top_k(logits[:, :N_EXPERTS], TOP_K)
    gate = jax.nn.softmax(top_val, axis=-1)
    expert = top_idx.astype(jnp.int32).reshape(n_assign)
    rank, counts = _expert_ranks(expert)
    padded = (counts + MOE_SUPER - 1) // MOE_SUPER * MOE_SUPER
    pad_end = jnp.cumsum(padded)
    pad_start = pad_end - padded
    slot = (pad_start[expert] + rank).reshape(n_tok, TOP_K)
    n_super = -(-n_assign // MOE_SUPER) + N_EXPERTS
    n_slots = n_super * MOE_SUPER
    token = jnp.repeat(jnp.arange(n_tok, dtype=jnp.int32), TOP_K)
    slot_token = jnp.zeros((n_slots,), jnp.int32).at[slot.reshape(n_assign)].set(token)
    n_used = pad_end[-1] // MOE_SUPER
    sb = jnp.arange(n_super, dtype=jnp.int32)
    block_expert = jnp.minimum(jnp.searchsorted(pad_end, jnp.minimum(sb, n_used - 1) * MOE_SUPER, side='right'),
                               N_EXPERTS - 1).astype(jnp.int32)
    rows_here = jnp.clip(counts[block_expert] - (sb * MOE_SUPER - pad_start[block_expert]), 0, MOE_SUPER)
    n_halves = jnp.where(sb < n_used, (rows_here + MOE_BLOCK - 1) // MOE_BLOCK, 0).astype(jnp.int32)
    yb = jnp.zeros((n_slots, d), BF16)
    sb0 = 0
    for part in MOE_CALL_SPLIT:
        sb1 = n_super if part is None else sb0 + part
        rows = slice(sb0 * MOE_SUPER, sb1 * MOE_SUPER)
        yb = moe_experts(hn[slot_token[rows]], sb0, block_expert[sb0:sb1], n_halves[sb0:sb1], w1, w3, w2, yb)
        sb0 = sb1
    if g_final is None:
        return moe_combine(h, yb[slot[:, 0]], yb[slot[:, 1]], gate[:, 0:1], gate[:, 1:2])
    keep = lambda a: a.reshape(bsz, length, -1)[:, N_META:].reshape(bsz * (length - N_META), -1)
    slot, gate = keep(slot), keep(gate)
    return moe_combine_final(h, yb[slot[:, 0]], yb[slot[:, 1]], gate[:, 0:1], gate[:, 1:2], g_final,
                             bsz=bsz, length=length)


def kernel(x, meta_tokens, norm_mix, w_in, na_rpb, na_meta_bias, sc_conv, hy_conv, hy_w1, hy_b1, hy_w2, hy_b2,
           hy_w3, hy_freq, hy_decay, hy_skip, w_na_out, w_sc_out, w_hy_out, w_o, norm_ffn, ffn_w1, ffn_w3,
           ffn_w2, router_w, moe_w1, moe_w3, moe_w2, norm_final):
    bsz, seq, d = x.shape
    length = N_META + seq
    depth = w_in.shape[0]
    dft = _dft_matrices(length)
    kw = dict(bsz=bsz, length=length)
    w_in, w_na_out, w_sc_out, w_hy_out, w_o, ffn_w1, ffn_w3, ffn_w2 = map(
        to_bf16, (w_in, w_na_out, w_sc_out, w_hy_out, w_o, ffn_w1, ffn_w3, ffn_w2))
    for layer in range(depth):
        if layer == 0:
            proj, h = embed_norm_matmul(x, meta_tokens, norm_mix[0], w_in, tn=2048)
        else:
            proj = norm_matmul(h, norm_mix[layer], w_in, layer, tn=2048)
        y_na = neighbourhood_attention(proj, na_rpb[layer], na_meta_bias[layer], **kw)
        y_sc = short_conv_mixer(proj, sc_conv[layer], **kw)
        y_hy = hyena_mixer(proj, hy_conv[layer], hy_w1[layer], hy_b1[layer], hy_w2[layer], hy_b2[layer],
                           hy_w3[layer], hy_freq[layer], hy_decay[layer], hy_skip[layer], dft, **kw)
        h = mixer_out(h, y_na, y_sc, y_hy, proj, w_na_out[layer], w_sc_out[layer], w_hy_out[layer], w_o[layer])
        j = layer // 2
        if layer % 2 == 0:
            h = dense_ffn(h, norm_ffn[layer], ffn_w1[j], ffn_w3[j], ffn_w2[j])
        else:
            g_final = norm_final if layer == depth - 1 else None
            h = moe_swiglu(h, norm_ffn[layer], router_w[j], moe_w1[j], moe_w3[j], moe_w2[j], g_final, **kw)
    if depth % 2 == 0:
        return h
    out = rms_norm_rows(h, norm_final, out_dtype=x.dtype)
    return out.reshape(bsz, length, d)[:, N_META:]
```
